```python
import math
import jax, jax.numpy as jnp
from jax import lax
import numpy as np

D_MODEL = 1024
BATCH = 8
SEQ = 2048
DEPTH = 1

GRID_W = 64
NA_HEADS = 8
NA_HEAD_DIM = 64
NA_WIN_ROWS = 8
NA_WIN_COLS = 16
NA_WIDTH = NA_HEADS * NA_HEAD_DIM
DN_HEADS = 8
DN_HEAD_DIM = 64
DN_WIDTH = DN_HEADS * DN_HEAD_DIM
DN_CONV = 5
DN_CHUNK = 64
N_GROUPS = 4
EXPERTS_PER_GROUP = 8
N_EXPERTS = N_GROUPS * EXPERTS_PER_GROUP
TOP_K = 2
D_EXPERT = 256
DEEPNORM_ALPHA = (2.0 * DEPTH) ** 0.25
DEEPNORM_BETA = (8.0 * DEPTH) ** -0.25
LN_EPS = 1e-5
RMS_EPS = 1e-6

IN_SPLITS = (NA_WIDTH, NA_WIDTH, NA_WIDTH, DN_WIDTH, DN_WIDTH, DN_WIDTH, DN_WIDTH,
             DN_HEADS, DN_HEADS, DN_HEADS, DN_HEADS, D_MODEL, D_MODEL)
D_IN = sum(IN_SPLITS)
VALUE_SLOTS = (2, 5)

kernel_name = "hybrid_natten_gdn_hiermoe_deepnorm"


def layer_norm(x, g, b):
    xf = x.astype(jnp.float32)
    mu = jnp.mean(xf, -1, keepdims=True)
    var = jnp.mean(jnp.square(xf - mu), -1, keepdims=True)
    return ((xf - mu) * lax.rsqrt(var + LN_EPS) * g + b).astype(x.dtype)


def l2norm(a):
    return a * lax.rsqrt(jnp.sum(a * a, -1, keepdims=True) + RMS_EPS)


def neighborhood_attention(q, k, v, rpb):
    B, T, _ = q.shape
    rows = T // GRID_W
    kr = min(NA_WIN_ROWS, rows)
    to_grid = lambda a: a.reshape(B, rows, GRID_W, NA_HEADS, NA_HEAD_DIM).transpose(0, 3, 1, 2, 4)
    qg, kg, vg = to_grid(q), to_grid(k), to_grid(v)
    r = np.arange(rows)
    row_start = np.clip(r - kr // 2, 0, rows - kr)
    row_idx = row_start[:, None] + np.arange(kr)[None, :]
    dr_idx = row_idx - r[:, None] + (NA_WIN_ROWS - 1)
    c = np.arange(GRID_W)
    col_start = np.clip(c - NA_WIN_COLS // 2, 0, GRID_W - NA_WIN_COLS)
    col_mask = (c[None, :] >= col_start[:, None]) & (c[None, :] < col_start[:, None] + NA_WIN_COLS)
    dc_idx = np.clip(c[None, :] - c[:, None], -(NA_WIN_COLS - 1), NA_WIN_COLS - 1) + (NA_WIN_COLS - 1)
    k_blk = kg[:, :, row_idx]
    v_blk = vg[:, :, row_idx]
    scale = NA_HEAD_DIM ** -0.5
    s = jnp.einsum('bhrcd,bhrjkd->bhrcjk', qg * scale, k_blk).astype(jnp.float32)
    bias = rpb[:, dr_idx[:, None, :, None], dc_idx[None, :, None, :]]
    s = s + bias[None].astype(jnp.float32)
    s = jnp.where(col_mask[:, None, :], s, -1e30)
    p = jax.nn.softmax(s.reshape(B, NA_HEADS, rows, GRID_W, kr * GRID_W), axis=-1)
    p = p.reshape(s.shape).astype(v.dtype)
    o = jnp.einsum('bhrcjk,bhrjkd->bhrcd', p, v_blk)
    return o.transpose(0, 2, 3, 1, 4).reshape(B, T, NA_WIDTH)


def short_conv(x, w):
    y = lax.conv_general_dilated(x, w[:, None, :].astype(x.dtype), window_strides=(1,),
                                 padding=[(DN_CONV // 2, DN_CONV // 2)],
                                 dimension_numbers=('NWC', 'WIO', 'NWC'),
                                 feature_group_count=x.shape[-1])
    return jax.nn.silu(y)


def chunk_gated_delta(q, k, v, beta, g):
    B, H, T, dk = q.shape
    dv = v.shape[-1]
    c = DN_CHUNK
    n = T // c
    q = q * dk ** -0.5
    blk = lambda a: a.reshape(B, H, n, c, *a.shape[3:])
    q, k, v, beta, g = blk(q), blk(k), blk(v), blk(beta), blk(g)
    gc = jnp.cumsum(g, axis=-1)
    incl = np.tril(np.ones((c, c), bool))
    strict = np.tril(np.ones((c, c), bool), -1)
    decay = jnp.exp(jnp.where(incl, gc[..., :, None] - gc[..., None, :], -jnp.inf))
    kb = k * beta[..., None]
    kk = jnp.einsum('bhnid,bhnjd->bhnij', kb, k) * decay
    a_mat = jnp.where(strict, kk, 0.0) + np.eye(c, dtype=np.float32)
    rhs = jnp.concatenate([v * beta[..., None], kb * jnp.exp(gc)[..., None]], axis=-1)
    sol = lax.linalg.triangular_solve(a_mat, rhs, left_side=True, lower=True, unit_diagonal=True)
    u, w = sol[..., :dv], sol[..., dv:]
    intra = jnp.einsum('bhnid,bhnjd->bhnij', q, k) * decay
    q_dec = q * jnp.exp(gc)[..., None]
    g_last = gc[..., -1]
    k_dec = k * jnp.exp(g_last[..., None] - gc)[..., None]

    def step(S, xs):
        qe, ke, u_c, w_c, a_c, gl = xs
        v_new = u_c - jnp.einsum('bhik,bhkv->bhiv', w_c, S)
        o = jnp.einsum('bhik,bhkv->bhiv', qe, S) + jnp.einsum('bhij,bhjv->bhiv', a_c, v_new)
        S = S * jnp.exp(gl)[..., None, None] + jnp.einsum('bhik,bhiv->bhkv', ke, v_new)
        return S, o

    xs = tuple(jnp.moveaxis(a, 2, 0) for a in (q_dec, k_dec, u, w, intra, g_last))
    S0 = jnp.zeros((B, H, dk, dv), jnp.float32)
    _, o = lax.scan(step, S0, xs)
    return jnp.moveaxis(o, 0, 2).reshape(B, H, T, dv)


def gated_deltanet_bidir(q, k, v, z, beta_f, beta_b, a_f, a_b, conv_w,
                         a_log_f, a_log_b, dt_bias_f, dt_bias_b, norm_w):
    B, T, _ = q.shape
    qkv = short_conv(jnp.concatenate([q, k, v], axis=-1), conv_w)
    q, k, v = jnp.split(qkv, 3, axis=-1)
    heads = lambda a: a.reshape(B, T, DN_HEADS, DN_HEAD_DIM).transpose(0, 2, 1, 3).astype(jnp.float32)
    q, k, v, zh = heads(q), heads(k), heads(v), heads(z)
    q, k = l2norm(q), l2norm(k)

    def gates(beta_raw, a_raw, a_log, dt_bias):
        bt = jax.nn.sigmoid(beta_raw.astype(jnp.float32))
        gl = -jnp.exp(a_log.astype(jnp.float32)) * jax.nn.softplus(
            a_raw.astype(jnp.float32) + dt_bias.astype(jnp.float32))
        return bt.transpose(0, 2, 1), gl.transpose(0, 2, 1)

    bf, gf = gates(beta_f, a_f, a_log_f, dt_bias_f)
    bb, gb = gates(beta_b, a_b, a_log_b, dt_bias_b)
    flip = lambda a: jnp.flip(a, axis=2)
    o_fwd = chunk_gated_delta(q, k, v, bf, gf)
    o_bwd = flip(chunk_gated_delta(flip(q), flip(k), flip(v), flip(bb), flip(gb)))
    o = o_fwd + o_bwd
    o = o * lax.rsqrt(jnp.mean(o * o, -1, keepdims=True) + RMS_EPS) * norm_w.astype(jnp.float32)
    o = o * jax.nn.silu(zh)
    return o.transpose(0, 2, 1, 3).reshape(B, T, DN_WIDTH).astype(z.dtype)


def token_mixer(x, w_in, na_rpb, dn_conv_w, dn_a_log_f, dn_a_log_b, dn_dt_bias_f, dn_dt_bias_b,
                dn_norm_w, w_proj_na, w_proj_dn, w_out):
    h = jnp.einsum('btd,de->bte', x, w_in)
    offs = np.cumsum(IN_SPLITS)[:-1].tolist()
    (q_na, k_na, v_na, q_dn, k_dn, v_dn, z_dn, b_f, b_b, a_f, a_b,
     gate_na, gate_dn) = jnp.split(h, offs, axis=-1)
    y_na = neighborhood_attention(q_na, k_na, v_na, na_rpb) @ w_proj_na
    y_dn = gated_deltanet_bidir(q_dn, k_dn, v_dn, z_dn, b_f, b_b, a_f, a_b, dn_conv_w,
                                dn_a_log_f, dn_a_log_b, dn_dt_bias_f, dn_dt_bias_b, dn_norm_w) @ w_proj_dn
    merged = jax.nn.sigmoid(gate_na) * y_na + jax.nn.sigmoid(gate_dn) * y_dn
    return merged @ w_out


def hierarchical_moe(x, w_router_group, b_router_group, w_router_expert, b_router_expert,
                     w_expert_gate_up, w_expert_down):
    B, T, D = x.shape
    N = B * T
    xt = x.reshape(N, D)
    group_logits = (xt @ w_router_group + b_router_group).astype(jnp.float32)
    group_p, group_idx = lax.top_k(jax.nn.softmax(group_logits, axis=-1), 1)
    expert_logits = (xt @ w_router_expert + b_router_expert).astype(jnp.float32)
    expert_logits = expert_logits.reshape(N, N_GROUPS, EXPERTS_PER_GROUP)
    sel = jnp.take_along_axis(expert_logits, group_idx[:, :, None], axis=1)[:, 0]
    top_l, top_i = lax.top_k(sel, TOP_K)
    top_p = jax.nn.softmax(top_l, axis=-1) * group_p
    expert_id = group_idx * EXPERTS_PER_GROUP + top_i
    combine = jnp.sum(jax.nn.one_hot(expert_id, N_EXPERTS, dtype=jnp.float32) * top_p[..., None], axis=1)
    hgu = jnp.einsum('nd,edf->nef', xt, w_expert_gate_up)
    hg, hu = jnp.split(hgu, 2, axis=-1)
    hid = jax.nn.silu(hg) * hu * combine[:, :, None].astype(x.dtype)
    y = jnp.einsum('nef,efd->nd', hid, w_expert_down)
    return y.reshape(B, T, D)


def setup_inputs(seed: int = 0) -> dict:
    key = jax.random.key(seed)
    ks = jax.random.split(key, 24)
    nrm = lambda k, shape, s: jax.random.normal(k, shape, jnp.float32) * s
    col_scale = jnp.concatenate([
        jnp.full((w,), DEEPNORM_BETA if i in VALUE_SLOTS else 1.0, jnp.float32)
        for i, w in enumerate(IN_SPLITS)])
    x = nrm(ks[0], (BATCH, SEQ, D_MODEL), 1.0)
    w_in = nrm(ks[1], (DEPTH, D_MODEL, D_IN), D_MODEL ** -0.5) * col_scale
    na_rpb = nrm(ks[2], (DEPTH, NA_HEADS, 2 * NA_WIN_ROWS - 1, 2 * NA_WIN_COLS - 1), 0.5)
    dn_conv_w = nrm(ks[3], (DEPTH, DN_CONV, 3 * DN_WIDTH), DN_CONV ** -0.5)
    dn_a_log_f = jnp.log(jax.random.uniform(ks[4], (DEPTH, DN_HEADS), jnp.float32, 1.0, 16.0))
    dn_a_log_b = jnp.log(jax.random.uniform(ks[5], (DEPTH, DN_HEADS), jnp.float32, 1.0, 16.0))

    def dt_bias(k):
        u = jax.random.uniform(k, (DEPTH, DN_HEADS), jnp.float32)
        dt = jnp.exp(u * (math.log(0.1) - math.log(0.001)) + math.log(0.001))
        return dt + jnp.log(-jnp.expm1(-dt))

    dn_dt_bias_f = dt_bias(ks[6])
    dn_dt_bias_b = dt_bias(ks[7])
    dn_norm_w = 1.0 + nrm(ks[8], (DEPTH, DN_HEAD_DIM), 0.02)
    w_proj_na = nrm(ks[9], (DEPTH, NA_WIDTH, D_MODEL), NA_WIDTH ** -0.5 * DEEPNORM_BETA)
    w_proj_dn = nrm(ks[10], (DEPTH, DN_WIDTH, D_MODEL), DN_WIDTH ** -0.5 * DEEPNORM_BETA)
    w_out = nrm(ks[11], (DEPTH, D_MODEL, D_MODEL), D_MODEL ** -0.5 * DEEPNORM_BETA)
    ln1_g = 1.0 + nrm(ks[12], (DEPTH, D_MODEL), 0.02)
    ln1_b = nrm(ks[13], (DEPTH, D_MODEL), 0.02)
    w_router_group = nrm(ks[14], (DEPTH, D_MODEL, N_GROUPS), D_MODEL ** -0.5)
    b_router_group = nrm(ks[15], (DEPTH, N_GROUPS), 0.01)
    w_router_expert = nrm(ks[16], (DEPTH, D_MODEL, N_EXPERTS), D_MODEL ** -0.5)
    b_router_expert = nrm(ks[17], (DEPTH, N_EXPERTS), 0.01)
    w_expert_gate_up = nrm(ks[18], (DEPTH, N_EXPERTS, D_MODEL, 2 * D_EXPERT), D_MODEL ** -0.5)
    w_expert_down = nrm(ks[19], (DEPTH, N_EXPERTS, D_EXPERT, D_MODEL), D_EXPERT ** -0.5 * DEEPNORM_BETA)
    ln2_g = 1.0 + nrm(ks[20], (DEPTH, D_MODEL), 0.02)
    ln2_b = nrm(ks[21], (DEPTH, D_MODEL), 0.02)
    return {"x": x, "w_in": w_in, "na_rpb": na_rpb, "dn_conv_w": dn_conv_w,
            "dn_a_log_f": dn_a_log_f, "dn_a_log_b": dn_a_log_b,
            "dn_dt_bias_f": dn_dt_bias_f, "dn_dt_bias_b": dn_dt_bias_b, "dn_norm_w": dn_norm_w,
            "w_proj_na": w_proj_na, "w_proj_dn": w_proj_dn, "w_out": w_out,
            "ln1_g": ln1_g, "ln1_b": ln1_b,
            "w_router_group": w_router_group, "b_router_group": b_router_group,
            "w_router_expert": w_router_expert, "b_router_expert": b_router_expert,
            "w_expert_gate_up": w_expert_gate_up, "w_expert_down": w_expert_down,
            "ln2_g": ln2_g, "ln2_b": ln2_b}


def reference(x, w_in, na_rpb, dn_conv_w, dn_a_log_f, dn_a_log_b, dn_dt_bias_f, dn_dt_bias_b,
              dn_norm_w, w_proj_na, w_proj_dn, w_out, ln1_g, ln1_b,
              w_router_group, b_router_group, w_router_expert, b_router_expert,
              w_expert_gate_up, w_expert_down, ln2_g, ln2_b):
    for l in range(DEPTH):
        mix = token_mixer(x, w_in[l], na_rpb[l], dn_conv_w[l], dn_a_log_f[l], dn_a_log_b[l],
                          dn_dt_bias_f[l], dn_dt_bias_b[l], dn_norm_w[l],
                          w_proj_na[l], w_proj_dn[l], w_out[l])
        x = layer_norm(DEEPNORM_ALPHA * x + mix, ln1_g[l], ln1_b[l])
        ffn = hierarchical_moe(x, w_router_group[l], b_router_group[l], w_router_expert[l],
                               b_router_expert[l], w_expert_gate_up[l], w_expert_down[l])
        x = layer_norm(DEEPNORM_ALPHA * x + ffn, ln2_g[l], ln2_b[l])
    return x
```

```python
import functools

import numpy as np
import jax
import jax.numpy as jnp
from jax import lax
from jax.experimental import pallas as pl
from jax.experimental.pallas import tpu as pltpu

F32 = jnp.float32
BF16 = jnp.bfloat16

D_MODEL = 1024
GRID_W = 64
NA_HEADS = 8
NA_HEAD_DIM = 64
NA_WIN_ROWS = 8
NA_WIN_COLS = 16
NA_WIDTH = NA_HEADS * NA_HEAD_DIM
DN_HEADS = 8
DN_HEAD_DIM = 64
DN_WIDTH = DN_HEADS * DN_HEAD_DIM
DN_CONV = 5
CHUNK = 64
N_GROUPS = 4
EXPERTS_PER_GROUP = 8
N_EXPERTS = N_GROUPS * EXPERTS_PER_GROUP
D_EXPERT = 256
DEPTH = 1
DEEPNORM_ALPHA = (2.0 * DEPTH) ** 0.25
LN_EPS = 1e-5
RMS_EPS = 1e-6

LANES = 128
VMEM_LIMIT_BYTES = 56 * 1024 * 1024

COL_GNA, COL_GDN = 0, 1024
COL_QNA, COL_KNA, COL_VNA = 2048, 2560, 3072
COL_QDN, COL_KDN, COL_VDN, COL_ZDN = 3584, 4096, 4608, 5120
H_COLS = 5632
PROJ_CHUNK = 512

NA_QROWS = 4
NA_KROWS = 12
NA_TQ = NA_QROWS * GRID_W
NA_TK = NA_KROWS * GRID_W

DN_HG = 4
DN_GW = DN_HG * DN_HEAD_DIM
ROUTER_LANES = 128
NEG_BIG = -1e30


def _sigmoid(x):
    return 1.0 / (1.0 + jnp.exp(-x))


def _silu(x):
    return x * _sigmoid(x)


def _softplus(x):
    return jnp.maximum(x, 0.0) + jnp.log(1.0 + jnp.exp(-jnp.abs(x)))


def _split3(x):
    x1 = x.astype(BF16)
    r1 = x - x1.astype(F32)
    x2 = r1.astype(BF16)
    r2 = r1 - x2.astype(F32)
    return x1, x2, r2.astype(BF16)


def _dot(a, b):
    return jnp.dot(a, b, preferred_element_type=F32)


def _dot_nt(a, b):
    return lax.dot_general(a, b, (((1,), (1,)), ((), ())), preferred_element_type=F32)


def _dot_exact_lhs(a_bf16_exact, x):
    x1, x2, x3 = _split3(x)
    return _dot(a_bf16_exact, x1) + _dot(a_bf16_exact, x2) + _dot(a_bf16_exact, x3)


def _dot_exact_rhs(x, a_bf16_exact):
    x1, x2, x3 = _split3(x)
    return _dot(x1, a_bf16_exact) + _dot(x2, a_bf16_exact) + _dot(x3, a_bf16_exact)


def _dot_f32(a, b):
    a1, a2, a3 = _split3(a)
    b1, b2, b3 = _split3(b)
    return (_dot(a1, b1) + (_dot(a1, b2) + _dot(a2, b1))
            + (_dot(a1, b3) + _dot(a3, b1) + _dot(a2, b2)))


def _layer_norm(r, g, b):
    mu = jnp.mean(r, -1, keepdims=True)
    d = r - mu
    var = jnp.mean(d * d, -1, keepdims=True)
    return d * lax.rsqrt(var + LN_EPS) * g + b


def _proj_kernel(x_ref, w_ref, wg_ref, wgt_ref, h_ref, g_ref, gt_ref):
    xb = x_ref[...].astype(BF16)
    for c in range(H_COLS // PROJ_CHUNK):
        cs = slice(c * PROJ_CHUNK, (c + 1) * PROJ_CHUNK)
        h_ref[:, cs] = _dot(xb, w_ref[:, cs]).astype(BF16)
    for hg in range(DN_HEADS // DN_HG):
        g_ref[hg] = _dot(xb, wg_ref[hg])
        gt_ref[hg] = _dot_nt(wgt_ref[hg], xb)


def _proj_call(x2, w_main, w_g, w_gt, tm=512):
    n = x2.shape[0]
    ng = DN_HEADS // DN_HG
    gw = 4 * DN_HG
    return pl.pallas_call(
        _proj_kernel,
        grid=(n // tm,),
        in_specs=[
            pl.BlockSpec((tm, D_MODEL), lambda i: (i, 0)),
            pl.BlockSpec((D_MODEL, H_COLS), lambda i: (0, 0)),
            pl.BlockSpec((ng, D_MODEL, gw), lambda i: (0, 0, 0)),
            pl.BlockSpec((ng, gw, D_MODEL), lambda i: (0, 0, 0)),
        ],
        out_specs=[
            pl.BlockSpec((tm, H_COLS), lambda i: (i, 0)),
            pl.BlockSpec((ng, tm, gw), lambda i: (0, i, 0)),
            pl.BlockSpec((ng, gw, tm), lambda i: (0, 0, i)),
        ],
        out_shape=[
            jax.ShapeDtypeStruct((n, H_COLS), BF16),
            jax.ShapeDtypeStruct((ng, n, gw), F32),
            jax.ShapeDtypeStruct((ng, gw, n), F32),
        ],
        compiler_params=pltpu.CompilerParams(
            dimension_semantics=("arbitrary",), vmem_limit_bytes=VMEM_LIMIT_BYTES),
        name="proj",
    )(x2, w_main, w_g, w_gt)


def _na_bias_tables(rpb, rows):
    kr_win = min(NA_WIN_ROWS, rows)
    tables = []
    for r0 in (0, 2 * NA_QROWS, rows - NA_QROWS):
        kw0 = int(np.clip(r0 - NA_WIN_ROWS // 2, 0, rows - NA_KROWS))
        qi = np.arange(NA_TQ)
        ki = np.arange(NA_TK)
        qr, qc = r0 + qi // GRID_W, qi % GRID_W
        kr, kc = kw0 + ki // GRID_W, ki % GRID_W
        row_start = np.clip(qr - kr_win // 2, 0, rows - kr_win)
        col_start = np.clip(qc - NA_WIN_COLS // 2, 0, GRID_W - NA_WIN_COLS)
        valid = ((kr[None, :] >= row_start[:, None]) & (kr[None, :] < row_start[:, None] + kr_win)
                 & (kc[None, :] >= col_start[:, None]) & (kc[None, :] < col_start[:, None] + NA_WIN_COLS))
        dr = np.clip(kr[None, :] - qr[:, None] + (NA_WIN_ROWS - 1), 0, 2 * NA_WIN_ROWS - 2)
        dc = np.clip(kc[None, :] - qc[:, None], -(NA_WIN_COLS - 1), NA_WIN_COLS - 1) + (NA_WIN_COLS - 1)
        b = rpb[:, dr, dc]
        tables.append(jnp.where(valid[None], b, NEG_BIG))
    return jnp.stack(tables, 0).astype(F32)


def _na_kernel(q_ref, k_ref, v_ref, bias_ref, o_ref, *, rows):
    qt = pl.program_id(2)
    kw0 = jnp.clip(qt * NA_QROWS - NA_WIN_ROWS // 2, 0, rows - NA_KROWS) * GRID_W
    kw0 = pl.multiple_of(kw0, GRID_W)
    scale = NA_HEAD_DIM ** -0.5
    for hh in range(LANES // NA_HEAD_DIM):
        ls = slice(hh * NA_HEAD_DIM, (hh + 1) * NA_HEAD_DIM)
        q = q_ref[:, ls]
        k = k_ref[pl.ds(kw0, NA_TK), ls]
        v = v_ref[pl.ds(kw0, NA_TK), ls]
        s = _dot_nt(q, k) * scale + bias_ref[0, hh]
        m = jnp.max(s, -1, keepdims=True)
        p = jnp.exp(s - m)
        l = jnp.sum(p, -1, keepdims=True)
        o = _dot(p.astype(BF16), v) / l
        o_ref[:, ls] = o.astype(BF16)


def _na_call(h_all, bias, batch, seq):
    rows = seq // GRID_W
    nqt = rows // NA_QROWS
    hp = NA_WIDTH // LANES
    n = batch * seq

    def bias_idx(b, p, t):
        return (jnp.where(t == 0, 0, jnp.where(t == nqt - 1, 2, 1)), p, 0, 0)

    return pl.pallas_call(
        functools.partial(_na_kernel, rows=rows),
        grid=(batch, hp, nqt),
        in_specs=[
            pl.BlockSpec((NA_TQ, LANES), lambda b, p, t: (b * nqt + t, COL_QNA // LANES + p)),
            pl.BlockSpec((seq, LANES), lambda b, p, t: (b, COL_KNA // LANES + p)),
            pl.BlockSpec((seq, LANES), lambda b, p, t: (b, COL_VNA // LANES + p)),
            pl.BlockSpec((1, LANES // NA_HEAD_DIM, NA_TQ, NA_TK), bias_idx),
        ],
        out_specs=pl.BlockSpec((NA_TQ, LANES), lambda b, p, t: (b * nqt + t, p)),
        out_shape=jax.ShapeDtypeStruct((n, NA_WIDTH), BF16),
        compiler_params=pltpu.CompilerParams(
            dimension_semantics=("arbitrary", "arbitrary", "arbitrary"),
            vmem_limit_bytes=VMEM_LIMIT_BYTES),
        name="natten",
    )(h_all, h_all, h_all, bias)


def _unit_tri_inverse(nmat, eye, level_masks):
    t = eye - nmat * level_masks[0]
    for m in level_masks[1:]:
        x = _dot((nmat * m).astype(BF16), t.astype(BF16))
        t = t - _dot(t.astype(BF16), x.astype(BF16))
    return t


def _dn_kernel(q_ref, k_ref, v_ref, z_ref, g_ref, gt_ref, cwq_ref, cwk_ref, cwv_ref,
               prow_ref, pcol_ref, nw_ref, o_ref,
               qs, ks, vs, gn_s, gt_s, wq_s, ik_s, u_s, eg_s, st_s, *, seq):
    nchunk = seq // CHUNK
    hd = DN_HEAD_DIM
    rb = 256

    lane16 = lax.broadcasted_iota(jnp.int32, (1, 4 * DN_HG), 1)
    graw = g_ref[0]
    gl = -jnp.exp(prow_ref[0, 0:1, :]) * _softplus(graw + prow_ref[0, 1:2, :])
    gn_s[...] = jnp.where(lane16 < 2 * DN_HG, _sigmoid(graw), gl)
    sub16 = lax.broadcasted_iota(jnp.int32, (4 * DN_HG, 1), 0)
    grawt = gt_ref[0]
    glt = -jnp.exp(pcol_ref[0, :, 0:1]) * _softplus(grawt + pcol_ref[0, :, 1:2])
    gtt = jnp.where(sub16 < 2 * DN_HG, _sigmoid(grawt), glt)
    for n in range(nchunk):
        gt_s[n] = gtt[:, n * CHUNK:(n + 1) * CHUNK]

    pad = 16
    half = DN_CONV // 2
    for src, cw_ref, dst, norm, mul in ((q_ref, cwq_ref, qs, True, hd ** -0.5),
                                        (k_ref, cwk_ref, ks, True, 1.0),
                                        (v_ref, cwv_ref, vs, False, 1.0)):
        cw = cw_ref[...]
        for r0 in range(0, seq, rb):
            lo, hi = r0 - pad, r0 + rb + pad
            parts = []
            if lo < 0:
                parts.append(jnp.zeros((pad, DN_GW), F32))
            parts.append(src[max(lo, 0):min(hi, seq), :].astype(F32))
            if hi > seq:
                parts.append(jnp.zeros((pad, DN_GW), F32))
            xin = jnp.concatenate(parts, 0) if len(parts) > 1 else parts[0]
            nrow = rb + 2 * pad
            y = jnp.zeros((rb, DN_GW), F32)
            for i in range(DN_CONV):
                sh = (half - i) % nrow
                xs = xin if sh == 0 else pltpu.roll(xin, sh, 0)
                y = y + xs[pad:pad + rb, :] * cw[i:i + 1, :]
            y = _silu(y)
            for hl in range(DN_HG):
                seg = y[:, hl * hd:(hl + 1) * hd]
                if norm:
                    seg = seg * (lax.rsqrt(jnp.sum(seg * seg, -1, keepdims=True) + RMS_EPS) * mul)
                dst[hl, r0:r0 + rb, :] = seg.astype(BF16)

    ri = lax.broadcasted_iota(jnp.int32, (CHUNK, CHUNK), 0)
    ci = lax.broadcasted_iota(jnp.int32, (CHUNK, CHUNK), 1)
    eye = (ri == ci).astype(F32)
    tril = (ri >= ci).astype(BF16)
    triu = (ri <= ci).astype(BF16)
    lower_incl, lower_strict = ri >= ci, ri > ci
    upper_incl, upper_strict = ri <= ci, ri < ci
    level_masks = []
    s = 1
    while s < CHUNK:
        sh = s.bit_length() - 1
        level_masks.append((((ri >> (sh + 1)) == (ci >> (sh + 1))) & ((ri >> sh) != (ci >> sh))).astype(F32))
        s *= 2

    def prep(n, carry):
        r0 = pl.multiple_of(n * CHUNK, CHUNK)
        gcol = gn_s[pl.ds(r0, CHUNK), :]
        grow = gt_s[n]
        cs_col_f = _dot_exact_lhs(tril, gcol)
        cs_col_b = _dot_exact_lhs(triu, gcol)
        cs_row_f = _dot_exact_rhs(grow, triu)
        cs_row_b = _dot_exact_rhs(grow, tril)
        for hl in range(DN_HG):
            q = qs[hl, pl.ds(r0, CHUNK), :]
            k = ks[hl, pl.ds(r0, CHUNK), :]
            v = vs[hl, pl.ds(r0, CHUNK), :]
            kf, vf, qf = k.astype(F32), v.astype(F32), q.astype(F32)
            pr = _dot_nt(jnp.concatenate([q, k], 0), k)
            qk, kk = pr[:CHUNK], pr[CHUNK:]
            for d, (cs_col, cs_row, incl, strict, last) in enumerate((
                    (cs_col_f, cs_row_f, lower_incl, lower_strict, CHUNK - 1),
                    (cs_col_b, cs_row_b, upper_incl, upper_strict, 0))):
                bcol = gcol[:, d * DN_HG + hl:d * DN_HG + hl + 1]
                ccol = cs_col[:, 2 * DN_HG + d * DN_HG + hl:2 * DN_HG + d * DN_HG + hl + 1]
                crow = cs_row[2 * DN_HG + d * DN_HG + hl:2 * DN_HG + d * DN_HG + hl + 1, :]
                dec = jnp.exp(jnp.where(incl, ccol - crow, -jnp.inf))
                nmat = jnp.where(strict, bcol * kk * dec, 0.0)
                intra = qk * dec
                tinv = _unit_tri_inverse(nmat, eye, level_masks)
                e = jnp.exp(ccol)
                rhs = jnp.concatenate([vf * bcol, kf * (bcol * e)], 1).astype(BF16)
                sol = _dot(tinv.astype(BF16), rhs)
                glast = ccol[last:last + 1, :]
                qdec = qf * e
                kdec = kf * jnp.exp(glast - ccol)
                u_s[d, hl, pl.ds(r0, CHUNK), :] = sol[:, :hd]
                wq_s[d, hl, n, 0:CHUNK, :] = sol[:, hd:].astype(BF16)
                wq_s[d, hl, n, CHUNK:2 * CHUNK, :] = qdec.astype(BF16)
                ik_s[d, hl, n, 0:CHUNK, :] = intra.astype(BF16)
                ik_s[d, hl, n, CHUNK:2 * CHUNK, :] = kdec.T.astype(BF16)
                eg_s[d, hl, n] = jnp.broadcast_to(jnp.exp(glast), (8, LANES))
        return carry

    lax.fori_loop(0, nchunk, prep, 0)

    st_s[...] = jnp.zeros_like(st_s)

    def scan(i, carry):
        for d in range(2):
            n = i if d == 0 else nchunk - 1 - i
            r0 = pl.multiple_of(n * CHUNK, CHUNK)
            for hl in range(DN_HG):
                st = st_s[d, hl]
                t = _dot(wq_s[d, hl, n], st.astype(BF16))
                vnew = u_s[d, hl, pl.ds(r0, CHUNK), :] - t[:CHUNK]
                t2 = _dot(ik_s[d, hl, n], vnew.astype(BF16))
                u_s[d, hl, pl.ds(r0, CHUNK), :] = t[CHUNK:] + t2[:CHUNK]
                st_s[d, hl] = st * eg_s[d, hl, n][0:1, 0:hd] + t2[CHUNK:]
        return carry

    lax.fori_loop(0, nchunk, scan, 0)

    nw = nw_ref[...]
    for r0 in range(0, seq, rb):
        for hl in range(DN_HG):
            o = u_s[0, hl, r0:r0 + rb, :] + u_s[1, hl, r0:r0 + rb, :]
            o = o * lax.rsqrt(jnp.mean(o * o, -1, keepdims=True) + RMS_EPS) * nw
            zz = z_ref[r0:r0 + rb, hl * hd:(hl + 1) * hd].astype(F32)
            o_ref[r0:r0 + rb, hl * hd:(hl + 1) * hd] = (o * _silu(zz)).astype(BF16)


def _dn_call(h_all, g_nat, g_t, cw, prow, pcol, norm_w, batch, seq):
    n = batch * seq
    ng = DN_HEADS // DN_HG
    gw = 4 * DN_HG
    nchunk = seq // CHUNK
    cb = DN_GW // LANES

    def col(c0):
        return lambda b, g: (b, c0 // DN_GW + g)

    return pl.pallas_call(
        functools.partial(_dn_kernel, seq=seq),
        grid=(batch, ng),
        in_specs=[
            pl.BlockSpec((seq, DN_GW), col(COL_QDN)),
            pl.BlockSpec((seq, DN_GW), col(COL_KDN)),
            pl.BlockSpec((seq, DN_GW), col(COL_VDN)),
            pl.BlockSpec((seq, DN_GW), col(COL_ZDN)),
            pl.BlockSpec((1, seq, gw), lambda b, g: (g, b, 0)),
            pl.BlockSpec((1, gw, seq), lambda b, g: (g, 0, b)),
            pl.BlockSpec((DN_CONV, DN_GW), lambda b, g: (0, g)),
            pl.BlockSpec((DN_CONV, DN_GW), lambda b, g: (0, DN_WIDTH // DN_GW + g)),
            pl.BlockSpec((DN_CONV, DN_GW), lambda b, g: (0, 2 * DN_WIDTH // DN_GW + g)),
            pl.BlockSpec((1, 2, gw), lambda b, g: (g, 0, 0)),
            pl.BlockSpec((1, gw, 2), lambda b, g: (g, 0, 0)),
            pl.BlockSpec((1, DN_HEAD_DIM), lambda b, g: (0, 0)),
        ],
        out_specs=pl.BlockSpec((seq, DN_GW), lambda b, g: (b, g)),
        out_shape=jax.ShapeDtypeStruct((n, DN_WIDTH), BF16),
        scratch_shapes=[
            pltpu.VMEM((DN_HG, seq, DN_HEAD_DIM), BF16),
            pltpu.VMEM((DN_HG, seq, DN_HEAD_DIM), BF16),
            pltpu.VMEM((DN_HG, seq, DN_HEAD_DIM), BF16),
            pltpu.VMEM((seq, gw), F32),
            pltpu.VMEM((nchunk, gw, CHUNK), F32),
            pltpu.VMEM((2, DN_HG, nchunk, 2 * CHUNK, DN_HEAD_DIM), BF16),
            pltpu.VMEM((2, DN_HG, nchunk, 2 * CHUNK, DN_HEAD_DIM), BF16),
            pltpu.VMEM((2, DN_HG, seq, DN_HEAD_DIM), F32),
            pltpu.VMEM((2, DN_HG, nchunk, 8, LANES), F32),
            pltpu.VMEM((2, DN_HG, CHUNK, DN_HEAD_DIM), F32),
        ],
        compiler_params=pltpu.CompilerParams(
            dimension_semantics=("arbitrary", "arbitrary"), vmem_limit_bytes=VMEM_LIMIT_BYTES),
        name="deltanet",
    )(h_all, h_all, h_all, h_all, g_nat, g_t, cw, cw, cw, prow, pcol, norm_w)


def _merge_kernel(x_ref, yna_ref, ydn_ref, gna_ref, gdn_ref, wpn_ref, wpd_ref, wo_ref,
                  g1_ref, b1_ref, wr_ref, br_ref, x1_ref, x1b_ref, comb_ref):
    a = _dot(yna_ref[...], wpn_ref[...])
    b = _dot(ydn_ref[...], wpd_ref[...])
    merged = _sigmoid(gna_ref[...].astype(F32)) * a + _sigmoid(gdn_ref[...].astype(F32)) * b
    mix = _dot(merged.astype(BF16), wo_ref[...])
    x1 = _layer_norm(DEEPNORM_ALPHA * x_ref[...] + mix, g1_ref[...], b1_ref[...])
    x1_ref[...] = x1
    x1b_ref[...] = x1.astype(BF16)

    logits = _dot_f32(x1, wr_ref[...]) + br_ref[...]
    lane = lax.broadcasted_iota(jnp.int32, logits.shape, 1)
    lanef = lane.astype(F32)
    big = float(ROUTER_LANES)
    gmask = lane < N_GROUPS
    gl = jnp.where(gmask, logits, -jnp.inf)
    gmax = jnp.max(gl, -1, keepdims=True)
    gidx = jnp.min(jnp.where(gl == gmax, lanef, big), -1, keepdims=True)
    pg = 1.0 / jnp.sum(jnp.where(gmask, jnp.exp(gl - gmax), 0.0), -1, keepdims=True)
    egrp = jnp.floor((lanef - N_GROUPS) * (1.0 / EXPERTS_PER_GROUP))
    emask = (lane >= N_GROUPS) & (lane < N_GROUPS + N_EXPERTS) & (egrp == gidx)
    el = jnp.where(emask, logits, -jnp.inf)
    m1 = jnp.max(el, -1, keepdims=True)
    i1 = jnp.min(jnp.where(el == m1, lanef, big), -1, keepdims=True)
    el2 = jnp.where(lanef == i1, -jnp.inf, el)
    m2 = jnp.max(el2, -1, keepdims=True)
    i2 = jnp.min(jnp.where(el2 == m2, lanef, big), -1, keepdims=True)
    t = jnp.exp(m2 - m1)
    p1 = pg / (1.0 + t)
    p2 = pg * t / (1.0 + t)
    comb_ref[...] = jnp.where(lanef == i1, p1, 0.0) + jnp.where(lanef == i2, p2, 0.0)


def _merge_call(x2, y_na, y_dn, h_all, wpn, wpd, wo, g1, b1, wr, br, tm=512):
    n = x2.shape[0]
    const = lambda i: (0, 0)
    return pl.pallas_call(
        _merge_kernel,
        grid=(n // tm,),
        in_specs=[
            pl.BlockSpec((tm, D_MODEL), lambda i: (i, 0)),
            pl.BlockSpec((tm, NA_WIDTH), lambda i: (i, 0)),
            pl.BlockSpec((tm, DN_WIDTH), lambda i: (i, 0)),
            pl.BlockSpec((tm, D_MODEL), lambda i: (i, COL_GNA // D_MODEL)),
            pl.BlockSpec((tm, D_MODEL), lambda i: (i, COL_GDN // D_MODEL)),
            pl.BlockSpec((NA_WIDTH, D_MODEL), const),
            pl.BlockSpec((DN_WIDTH, D_MODEL), const),
            pl.BlockSpec((D_MODEL, D_MODEL), const),
            pl.BlockSpec((1, D_MODEL), const),
            pl.BlockSpec((1, D_MODEL), const),
            pl.BlockSpec((D_MODEL, ROUTER_LANES), const),
            pl.BlockSpec((1, ROUTER_LANES), const),
        ],
        out_specs=[
            pl.BlockSpec((tm, D_MODEL), lambda i: (i, 0)),
            pl.BlockSpec((tm, D_MODEL), lambda i: (i, 0)),
            pl.BlockSpec((tm, ROUTER_LANES), lambda i: (i, 0)),
        ],
        out_shape=[
            jax.ShapeDtypeStruct((n, D_MODEL), F32),
            jax.ShapeDtypeStruct((n, D_MODEL), BF16),
            jax.ShapeDtypeStruct((n, ROUTER_LANES), F32),
        ],
        compiler_params=pltpu.CompilerParams(
            dimension_semantics=("arbitrary",), vmem_limit_bytes=VMEM_LIMIT_BYTES),
        name="merge_ln1_router",
    )(x2, y_na, y_dn, h_all, h_all, wpn, wpd, wo, g1, b1, wr, br)


def _moe_kernel(x1_ref, x1b_ref, comb_ref, wgu_ref, wd_ref, g2_ref, b2_ref, o_ref, acc_ref):
    e = pl.program_id(1)

    @pl.when(e == 0)
    def _():
        acc_ref[...] = jnp.zeros_like(acc_ref)

    comb = comb_ref[...]
    lane = lax.broadcasted_iota(jnp.int32, comb.shape, 1)
    ce = jnp.sum(jnp.where(lane == e + N_GROUPS, comb, 0.0), -1, keepdims=True)
    hgu = _dot(x1b_ref[...], wgu_ref[0])
    hid = _silu(hgu[:, :D_EXPERT]) * hgu[:, D_EXPERT:] * ce
    acc_ref[...] += _dot(hid.astype(BF16), wd_ref[0])

    @pl.when(e == N_EXPERTS - 1)
    def _():
        o_ref[...] = _layer_norm(DEEPNORM_ALPHA * x1_ref[...] + acc_ref[...], g2_ref[...], b2_ref[...])


def _moe_call(x1, x1b, comb, wgu, wd, g2, b2, tm=1024):
    n = x1.shape[0]
    return pl.pallas_call(
        _moe_kernel,
        grid=(n // tm, N_EXPERTS),
        in_specs=[
            pl.BlockSpec((tm, D_MODEL), lambda i, e: (i, 0)),
            pl.BlockSpec((tm, D_MODEL), lambda i, e: (i, 0)),
            pl.BlockSpec((tm, ROUTER_LANES), lambda i, e: (i, 0)),
            pl.BlockSpec((1, D_MODEL, 2 * D_EXPERT), lambda i, e: (e, 0, 0)),
            pl.BlockSpec((1, D_EXPERT, D_MODEL), lambda i, e: (e, 0, 0)),
            pl.BlockSpec((1, D_MODEL), lambda i, e: (0, 0)),
            pl.BlockSpec((1, D_MODEL), lambda i, e: (0, 0)),
        ],
        out_specs=pl.BlockSpec((tm, D_MODEL), lambda i, e: (i, 0)),
        out_shape=jax.ShapeDtypeStruct((n, D_MODEL), F32),
        scratch_shapes=[pltpu.VMEM((tm, D_MODEL), F32)],
        compiler_params=pltpu.CompilerParams(
            dimension_semantics=("arbitrary", "arbitrary"), vmem_limit_bytes=VMEM_LIMIT_BYTES),
        name="moe_ln2",
    )(x1, x1b, comb, wgu, wd, g2, b2)


def _layer(x2, batch, seq, w_in, na_rpb, dn_conv_w, a_log_f, a_log_b, dt_bias_f, dt_bias_b, dn_norm_w,
           w_proj_na, w_proj_dn, w_out, ln1_g, ln1_b, w_rg, b_rg, w_re, b_re, w_gu, w_dn, ln2_g, ln2_b):
    n_act = 3 * NA_WIDTH + 4 * DN_WIDTH
    n_small = 4 * DN_HEADS
    w_main = jnp.concatenate([w_in[:, n_act + n_small:], w_in[:, :n_act]], 1).astype(BF16)
    ws = w_in[:, n_act:n_act + n_small].reshape(D_MODEL, 4, DN_HEADS // DN_HG, DN_HG)
    w_g = ws.transpose(2, 0, 1, 3).reshape(DN_HEADS // DN_HG, D_MODEL, 4 * DN_HG).astype(BF16)
    w_gt = w_g.transpose(0, 2, 1)

    def per_group(f, b):
        z = jnp.zeros((DN_HEADS // DN_HG, 2 * DN_HG), F32)
        return jnp.concatenate([z, f.reshape(-1, DN_HG), b.reshape(-1, DN_HG)], 1)

    pa, pd = per_group(a_log_f, a_log_b), per_group(dt_bias_f, dt_bias_b)
    prow = jnp.stack([pa, pd], 1)
    pcol = jnp.stack([pa, pd], 2)

    h_all, g_nat, g_t = _proj_call(x2, w_main, w_g, w_gt)
    bias = _na_bias_tables(na_rpb, seq // GRID_W)
    y_na = _na_call(h_all, bias, batch, seq)
    y_dn = _dn_call(h_all, g_nat, g_t, dn_conv_w, prow, pcol, dn_norm_w.reshape(1, DN_HEAD_DIM), batch, seq)

    wr = jnp.zeros((D_MODEL, ROUTER_LANES), F32)
    wr = wr.at[:, :N_GROUPS].set(w_rg).at[:, N_GROUPS:N_GROUPS + N_EXPERTS].set(w_re)
    br = jnp.zeros((1, ROUTER_LANES), F32)
    br = br.at[0, :N_GROUPS].set(b_rg).at[0, N_GROUPS:N_GROUPS + N_EXPERTS].set(b_re)
    x1, x1b, comb = _merge_call(x2, y_na, y_dn, h_all, w_proj_na.astype(BF16), w_proj_dn.astype(BF16),
                                w_out.astype(BF16), ln1_g.reshape(1, -1), ln1_b.reshape(1, -1), wr, br)
    return _moe_call(x1, x1b, comb, w_gu.astype(BF16), w_dn.astype(BF16),
                     ln2_g.reshape(1, -1), ln2_b.reshape(1, -1))


def kernel(x, w_in, na_rpb, dn_conv_w, dn_a_log_f, dn_a_log_b, dn_dt_bias_f, dn_dt_bias_b, dn_norm_w, w_proj_na, w_proj_dn, w_out, ln1_g, ln1_b, w_router_group, b_router_group, w_router_expert, b_router_expert, w_expert_gate_up, w_expert_down, ln2_g, ln2_b):
    batch, seq, d = x.shape
    x2 = x.reshape(batch * seq, d)
    for l in range(w_in.shape[0]):
        x2 = _layer(x2, batch, seq, w_in[l], na_rpb[l], dn_conv_w[l], dn_a_log_f[l], dn_a_log_b[l],
                    dn_dt_bias_f[l], dn_dt_bias_b[l], dn_norm_w[l], w_proj_na[l], w_proj_dn[l], w_out[l],
                    ln1_g[l], ln1_b[l], w_router_group[l], b_router_group[l], w_router_expert[l],
                    b_router_expert[l], w_expert_gate_up[l], w_expert_down[l], ln2_g[l], ln2_b[l])
    return x2.reshape(batch, seq, d)
```

```python
import functools

import numpy as np
import jax
import jax.numpy as jnp
from jax import lax
from jax.experimental import pallas as pl
from jax.experimental.pallas import tpu as pltpu

F32 = jnp.float32
BF16 = jnp.bfloat16

D_MODEL = 1024
GRID_W = 64
NA_HEADS = 8
NA_HEAD_DIM = 64
NA_WIN_ROWS = 8
NA_WIN_COLS = 16
NA_WIDTH = NA_HEADS * NA_HEAD_DIM
DN_HEADS = 8
DN_HEAD_DIM = 64
DN_WIDTH = DN_HEADS * DN_HEAD_DIM
DN_CONV = 5
CHUNK = 64
N_GROUPS = 4
EXPERTS_PER_GROUP = 8
N_EXPERTS = N_GROUPS * EXPERTS_PER_GROUP
D_EXPERT = 256
DEPTH = 1
DEEPNORM_ALPHA = (2.0 * DEPTH) ** 0.25
LN_EPS = 1e-5
RMS_EPS = 1e-6

LANES = 128
VMEM_LIMIT_BYTES = 56 * 1024 * 1024

COL_GNA, COL_GDN = 0, 1024
COL_QNA, COL_KNA, COL_VNA = 2048, 2560, 3072
COL_QDN, COL_KDN, COL_VDN, COL_ZDN = 3584, 4096, 4608, 5120
H_COLS = 5632
PROJ_CHUNK = 512

NA_QROWS = 4
NA_KROWS = 12
NA_TQ = NA_QROWS * GRID_W
NA_TK = NA_KROWS * GRID_W

DN_HG = 4
DN_GW = DN_HG * DN_HEAD_DIM
DN_PREP_UNROLL = 2
ROUTER_LANES = 128
NEG_BIG = -1e30


def _sigmoid(x):
    return 1.0 / (1.0 + jnp.exp(-x))


def _silu(x):
    return x * _sigmoid(x)


def _softplus(x):
    return jnp.maximum(x, 0.0) + jnp.log(1.0 + jnp.exp(-jnp.abs(x)))


def _split3(x):
    x1 = x.astype(BF16)
    r1 = x - x1.astype(F32)
    x2 = r1.astype(BF16)
    r2 = r1 - x2.astype(F32)
    return x1, x2, r2.astype(BF16)


def _dot(a, b):
    return jnp.dot(a, b, preferred_element_type=F32)


def _dot_nt(a, b):
    return lax.dot_general(a, b, (((1,), (1,)), ((), ())), preferred_element_type=F32)


def _dot_exact_lhs(a_bf16_exact, x):
    x1, x2, x3 = _split3(x)
    return _dot(a_bf16_exact, x1) + _dot(a_bf16_exact, x2) + _dot(a_bf16_exact, x3)


def _dot_exact_rhs(x, a_bf16_exact):
    x1, x2, x3 = _split3(x)
    return _dot(x1, a_bf16_exact) + _dot(x2, a_bf16_exact) + _dot(x3, a_bf16_exact)


def _dot_f32(a, b):
    a1, a2, a3 = _split3(a)
    b1, b2, b3 = _split3(b)
    return (_dot(a1, b1) + (_dot(a1, b2) + _dot(a2, b1))
            + (_dot(a1, b3) + _dot(a3, b1) + _dot(a2, b2)))


def _layer_norm(r, g, b):
    mu = jnp.mean(r, -1, keepdims=True)
    d = r - mu
    var = jnp.mean(d * d, -1, keepdims=True)
    return d * lax.rsqrt(var + LN_EPS) * g + b


def _proj_kernel(x_ref, w_ref, wg_ref, wgt_ref, h_ref, g_ref, gt_ref):
    xb = x_ref[...].astype(BF16)
    for c in range(H_COLS // PROJ_CHUNK):
        cs = slice(c * PROJ_CHUNK, (c + 1) * PROJ_CHUNK)
        h_ref[:, cs] = _dot(xb, w_ref[:, cs]).astype(BF16)
    for hg in range(DN_HEADS // DN_HG):
        g_ref[hg] = _dot(xb, wg_ref[hg])
        gt_ref[hg] = _dot_nt(wgt_ref[hg], xb)


def _proj_call(x2, w_main, w_g, w_gt, tm=512):
    n = x2.shape[0]
    ng = DN_HEADS // DN_HG
    gw = 4 * DN_HG
    return pl.pallas_call(
        _proj_kernel,
        grid=(n // tm,),
        in_specs=[
            pl.BlockSpec((tm, D_MODEL), lambda i: (i, 0)),
            pl.BlockSpec((D_MODEL, H_COLS), lambda i: (0, 0)),
            pl.BlockSpec((ng, D_MODEL, gw), lambda i: (0, 0, 0)),
            pl.BlockSpec((ng, gw, D_MODEL), lambda i: (0, 0, 0)),
        ],
        out_specs=[
            pl.BlockSpec((tm, H_COLS), lambda i: (i, 0)),
            pl.BlockSpec((ng, tm, gw), lambda i: (0, i, 0)),
            pl.BlockSpec((ng, gw, tm), lambda i: (0, 0, i)),
        ],
        out_shape=[
            jax.ShapeDtypeStruct((n, H_COLS), BF16),
            jax.ShapeDtypeStruct((ng, n, gw), F32),
            jax.ShapeDtypeStruct((ng, gw, n), F32),
        ],
        compiler_params=pltpu.CompilerParams(
            dimension_semantics=("arbitrary",), vmem_limit_bytes=VMEM_LIMIT_BYTES),
        name="proj",
    )(x2, w_main, w_g, w_gt)


def _na_bias_tables(rpb, rows):
    kr_win = min(NA_WIN_ROWS, rows)
    c = np.arange(GRID_W)
    dc = np.clip(c[None, :] - c[:, None], -(NA_WIN_COLS - 1), NA_WIN_COLS - 1) + (NA_WIN_COLS - 1)
    onehot = (dc[None] == np.arange(2 * NA_WIN_COLS - 1)[:, None, None]).astype(np.float32)
    tz = jnp.einsum('hrd,dqk->hrqk', rpb, onehot, precision=lax.Precision.HIGHEST)
    col_start = np.clip(c - NA_WIN_COLS // 2, 0, GRID_W - NA_WIN_COLS)
    col_ok = (c[None, :] >= col_start[:, None]) & (c[None, :] < col_start[:, None] + NA_WIN_COLS)
    tables = []
    for r0 in (0, 2 * NA_QROWS, rows - NA_QROWS):
        kw0 = int(np.clip(r0 - NA_WIN_ROWS // 2, 0, rows - NA_KROWS))
        qr = r0 + np.arange(NA_QROWS)
        kr = kw0 + np.arange(NA_KROWS)
        row_start = np.clip(qr - kr_win // 2, 0, rows - kr_win)
        row_ok = (kr[None, :] >= row_start[:, None]) & (kr[None, :] < row_start[:, None] + kr_win)
        dr = np.clip(kr[None, :] - qr[:, None] + (NA_WIN_ROWS - 1), 0, 2 * NA_WIN_ROWS - 2)
        blk = jnp.stack([jnp.stack([tz[:, dr[i, j]] for j in range(NA_KROWS)], 1)
                         for i in range(NA_QROWS)], 1)
        valid = row_ok[:, :, None, None] & col_ok[None, None]
        b = jnp.where(valid[None], blk, NEG_BIG).transpose(0, 1, 3, 2, 4)
        tables.append(b.reshape(rpb.shape[0], NA_TQ, NA_TK))
    return jnp.stack(tables, 0).astype(F32)


def _na_kernel(q_ref, k_ref, v_ref, bias_ref, o_ref, *, rows):
    qt = pl.program_id(2)
    kw0 = jnp.clip(qt * NA_QROWS - NA_WIN_ROWS // 2, 0, rows - NA_KROWS) * GRID_W
    kw0 = pl.multiple_of(kw0, GRID_W)
    scale = NA_HEAD_DIM ** -0.5
    for hh in range(LANES // NA_HEAD_DIM):
        ls = slice(hh * NA_HEAD_DIM, (hh + 1) * NA_HEAD_DIM)
        q = q_ref[:, ls]
        k = k_ref[pl.ds(kw0, NA_TK), ls]
        v = v_ref[pl.ds(kw0, NA_TK), ls]
        s = _dot_nt(q, k) * scale + bias_ref[0, hh]
        m = jnp.max(s, -1, keepdims=True)
        p = jnp.exp(s - m)
        l = jnp.sum(p, -1, keepdims=True)
        o = _dot(p.astype(BF16), v) / l
        o_ref[:, ls] = o.astype(BF16)


def _na_call(h_all, bias, batch, seq):
    rows = seq // GRID_W
    nqt = rows // NA_QROWS
    hp = NA_WIDTH // LANES
    n = batch * seq

    def bias_idx(b, p, t):
        return (jnp.where(t == 0, 0, jnp.where(t == nqt - 1, 2, 1)), p, 0, 0)

    return pl.pallas_call(
        functools.partial(_na_kernel, rows=rows),
        grid=(batch, hp, nqt),
        in_specs=[
            pl.BlockSpec((NA_TQ, LANES), lambda b, p, t: (b * nqt + t, COL_QNA // LANES + p)),
            pl.BlockSpec((seq, LANES), lambda b, p, t: (b, COL_KNA // LANES + p)),
            pl.BlockSpec((seq, LANES), lambda b, p, t: (b, COL_VNA // LANES + p)),
            pl.BlockSpec((1, LANES // NA_HEAD_DIM, NA_TQ, NA_TK), bias_idx),
        ],
        out_specs=pl.BlockSpec((NA_TQ, LANES), lambda b, p, t: (b * nqt + t, p)),
        out_shape=jax.ShapeDtypeStruct((n, NA_WIDTH), BF16),
        compiler_params=pltpu.CompilerParams(
            dimension_semantics=("arbitrary", "arbitrary", "arbitrary"),
            vmem_limit_bytes=VMEM_LIMIT_BYTES),
        name="natten",
    )(h_all, h_all, h_all, bias)


def _dn_kernel(q_ref, k_ref, v_ref, z_ref, g_ref, gt_ref, cwq_ref, cwk_ref, cwv_ref,
               prow_ref, pcol_ref, nw_ref, o_ref,
               qs, ks, vs, gn_s, gt_s, wq_s, ik_s, u_s, eg_s, st_s, *, seq):
    nchunk = seq // CHUNK
    hd = DN_HEAD_DIM
    rb = 256

    lane16 = lax.broadcasted_iota(jnp.int32, (1, 4 * DN_HG), 1)
    graw = g_ref[0]
    gl = -jnp.exp(prow_ref[0, 0:1, :]) * _softplus(graw + prow_ref[0, 1:2, :])
    gn_s[...] = jnp.where(lane16 < 2 * DN_HG, _sigmoid(graw), gl)
    sub16 = lax.broadcasted_iota(jnp.int32, (4 * DN_HG, 1), 0)
    grawt = gt_ref[0]
    glt = -jnp.exp(pcol_ref[0, :, 0:1]) * _softplus(grawt + pcol_ref[0, :, 1:2])
    gtt = jnp.where(sub16 < 2 * DN_HG, _sigmoid(grawt), glt)
    for n in range(nchunk):
        gt_s[n] = gtt[:, n * CHUNK:(n + 1) * CHUNK]

    pad = 16
    half = DN_CONV // 2
    for src, cw_ref, dst, norm, mul in ((q_ref, cwq_ref, qs, True, hd ** -0.5),
                                        (k_ref, cwk_ref, ks, True, 1.0),
                                        (v_ref, cwv_ref, vs, False, 1.0)):
        cw = cw_ref[...]
        for r0 in range(0, seq, rb):
            lo, hi = r0 - pad, r0 + rb + pad
            parts = []
            if lo < 0:
                parts.append(jnp.zeros((pad, DN_GW), F32))
            parts.append(src[max(lo, 0):min(hi, seq), :].astype(F32))
            if hi > seq:
                parts.append(jnp.zeros((pad, DN_GW), F32))
            xin = jnp.concatenate(parts, 0) if len(parts) > 1 else parts[0]
            nrow = rb + 2 * pad
            y = jnp.zeros((rb, DN_GW), F32)
            for i in range(DN_CONV):
                sh = (half - i) % nrow
                xs = xin if sh == 0 else pltpu.roll(xin, sh, 0)
                y = y + xs[pad:pad + rb, :] * cw[i:i + 1, :]
            y = _silu(y)
            for hl in range(DN_HG):
                seg = y[:, hl * hd:(hl + 1) * hd]
                if norm:
                    seg = seg * (lax.rsqrt(jnp.sum(seg * seg, -1, keepdims=True) + RMS_EPS) * mul)
                dst[hl, r0:r0 + rb, :] = seg.astype(BF16)

    ri = lax.broadcasted_iota(jnp.int32, (CHUNK, CHUNK), 0)
    ci = lax.broadcasted_iota(jnp.int32, (CHUNK, CHUNK), 1)
    eye = (ri == ci).astype(F32)
    tril = (ri >= ci).astype(BF16)
    triu = (ri <= ci).astype(BF16)
    lower_incl, lower_strict = ri >= ci, ri > ci
    upper_incl, upper_strict = ri <= ci, ri < ci
    level_masks = []
    s = 1
    while s < CHUNK:
        sh = s.bit_length() - 1
        level_masks.append((((ri >> (sh + 1)) == (ci >> (sh + 1))) & ((ri >> sh) != (ci >> sh))).astype(F32))
        s *= 2

    dirs = ((lower_incl, lower_strict, CHUNK - 1), (upper_incl, upper_strict, 0))

    def prep(it, carry):
        probs = []
        for c in range(DN_PREP_UNROLL):
            n = it * DN_PREP_UNROLL + c
            r0 = pl.multiple_of(n * CHUNK, CHUNK)
            gcol = gn_s[pl.ds(r0, CHUNK), :]
            grow = gt_s[n]
            cs_col = (_dot_exact_lhs(tril, gcol), _dot_exact_lhs(triu, gcol))
            cs_row = (_dot_exact_rhs(grow, triu), _dot_exact_rhs(grow, tril))
            for hl in range(DN_HG):
                q = qs[hl, pl.ds(r0, CHUNK), :]
                k = ks[hl, pl.ds(r0, CHUNK), :]
                v = vs[hl, pl.ds(r0, CHUNK), :]
                pr = _dot_nt(jnp.concatenate([q, k], 0), k)
                for d in range(2):
                    col = 2 * DN_HG + d * DN_HG + hl
                    probs.append(dict(n=n, r0=r0, hl=hl, d=d, q=q, k=k, v=v, pr=pr,
                                      bcol=gcol[:, d * DN_HG + hl:d * DN_HG + hl + 1],
                                      ccol=cs_col[d][:, col:col + 1], crow=cs_row[d][col:col + 1, :]))
        for p in probs:
            incl, strict, last = dirs[p["d"]]
            kf, vf, qf = p["k"].astype(F32), p["v"].astype(F32), p["q"].astype(F32)
            qk, kk = p["pr"][:CHUNK], p["pr"][CHUNK:]
            bcol, ccol = p["bcol"], p["ccol"]
            dec = jnp.exp(jnp.where(incl, ccol - p["crow"], -jnp.inf))
            p["nmat"] = jnp.where(strict, bcol * kk * dec, 0.0)
            p["intra"] = (qk * dec).astype(BF16)
            e = jnp.exp(ccol)
            p["rhs"] = jnp.concatenate([vf * bcol, kf * (bcol * e)], 1).astype(BF16)
            glast = ccol[last:last + 1, :]
            p["qdec"] = (qf * e).astype(BF16)
            p["kdect"] = (kf * jnp.exp(glast - ccol)).T.astype(BF16)
            p["eg"] = jnp.broadcast_to(jnp.exp(glast), (8, LANES))
            p["t"] = eye - p["nmat"] * level_masks[0]
        for m in level_masks[1:]:
            xs = [_dot((p["nmat"] * m).astype(BF16), p["t"].astype(BF16)) for p in probs]
            ys = [_dot(p["t"].astype(BF16), x.astype(BF16)) for p, x in zip(probs, xs)]
            for p, y in zip(probs, ys):
                p["t"] = p["t"] - y
        sols = [_dot(p["t"].astype(BF16), p["rhs"]) for p in probs]
        for p, sol in zip(probs, sols):
            d, hl, n = p["d"], p["hl"], p["n"]
            u_s[d, hl, pl.ds(p["r0"], CHUNK), :] = sol[:, :hd]
            wq_s[d, hl, n, 0:CHUNK, :] = sol[:, hd:].astype(BF16)
            wq_s[d, hl, n, CHUNK:2 * CHUNK, :] = p["qdec"]
            ik_s[d, hl, n, 0:CHUNK, :] = p["intra"]
            ik_s[d, hl, n, CHUNK:2 * CHUNK, :] = p["kdect"]
            eg_s[d, hl, n] = p["eg"]
        return carry

    lax.fori_loop(0, nchunk // DN_PREP_UNROLL, prep, 0)

    st_s[...] = jnp.zeros_like(st_s)

    def scan(i, carry):
        probs = []
        for d in range(2):
            n = i if d == 0 else nchunk - 1 - i
            r0 = pl.multiple_of(n * CHUNK, CHUNK)
            for hl in range(DN_HG):
                probs.append((d, hl, n, r0))
        sts = [st_s[d, hl] for d, hl, n, r0 in probs]
        ts = [_dot(wq_s[d, hl, n], st.astype(BF16))
              for (d, hl, n, r0), st in zip(probs, sts)]
        vnews = [u_s[d, hl, pl.ds(r0, CHUNK), :] - t[:CHUNK] for (d, hl, n, r0), t in zip(probs, ts)]
        t2s = [_dot(ik_s[d, hl, n], vn.astype(BF16))
               for (d, hl, n, r0), vn in zip(probs, vnews)]
        for (d, hl, n, r0), st, t, t2 in zip(probs, sts, ts, t2s):
            u_s[d, hl, pl.ds(r0, CHUNK), :] = t[CHUNK:] + t2[:CHUNK]
            st_s[d, hl] = st * eg_s[d, hl, n][0:1, 0:hd] + t2[CHUNK:]
        return carry

    lax.fori_loop(0, nchunk, scan, 0)

    nw = nw_ref[...]
    for r0 in range(0, seq, rb):
        for hl in range(DN_HG):
            o = u_s[0, hl, r0:r0 + rb, :] + u_s[1, hl, r0:r0 + rb, :]
            o = o * lax.rsqrt(jnp.mean(o * o, -1, keepdims=True) + RMS_EPS) * nw
            zz = z_ref[r0:r0 + rb, hl * hd:(hl + 1) * hd].astype(F32)
            o_ref[r0:r0 + rb, hl * hd:(hl + 1) * hd] = (o * _silu(zz)).astype(BF16)


def _dn_call(h_all, g_nat, g_t, cw, prow, pcol, norm_w, batch, seq):
    n = batch * seq
    ng = DN_HEADS // DN_HG
    gw = 4 * DN_HG
    nchunk = seq // CHUNK
    cb = DN_GW // LANES

    def col(c0):
        return lambda b, g: (b, c0 // DN_GW + g)

    return pl.pallas_call(
        functools.partial(_dn_kernel, seq=seq),
        grid=(batch, ng),
        in_specs=[
            pl.BlockSpec((seq, DN_GW), col(COL_QDN)),
            pl.BlockSpec((seq, DN_GW), col(COL_KDN)),
            pl.BlockSpec((seq, DN_GW), col(COL_VDN)),
            pl.BlockSpec((seq, DN_GW), col(COL_ZDN)),
            pl.BlockSpec((1, seq, gw), lambda b, g: (g, b, 0)),
            pl.BlockSpec((1, gw, seq), lambda b, g: (g, 0, b)),
            pl.BlockSpec((DN_CONV, DN_GW), lambda b, g: (0, g)),
            pl.BlockSpec((DN_CONV, DN_GW), lambda b, g: (0, DN_WIDTH // DN_GW + g)),
            pl.BlockSpec((DN_CONV, DN_GW), lambda b, g: (0, 2 * DN_WIDTH // DN_GW + g)),
            pl.BlockSpec((1, 2, gw), lambda b, g: (g, 0, 0)),
            pl.BlockSpec((1, gw, 2), lambda b, g: (g, 0, 0)),
            pl.BlockSpec((1, DN_HEAD_DIM), lambda b, g: (0, 0)),
        ],
        out_specs=pl.BlockSpec((seq, DN_GW), lambda b, g: (b, g)),
        out_shape=jax.ShapeDtypeStruct((n, DN_WIDTH), BF16),
        scratch_shapes=[
            pltpu.VMEM((DN_HG, seq, DN_HEAD_DIM), BF16),
            pltpu.VMEM((DN_HG, seq, DN_HEAD_DIM), BF16),
            pltpu.VMEM((DN_HG, seq, DN_HEAD_DIM), BF16),
            pltpu.VMEM((seq, gw), F32),
            pltpu.VMEM((nchunk, gw, CHUNK), F32),
            pltpu.VMEM((2, DN_HG, nchunk, 2 * CHUNK, DN_HEAD_DIM), BF16),
            pltpu.VMEM((2, DN_HG, nchunk, 2 * CHUNK, DN_HEAD_DIM), BF16),
            pltpu.VMEM((2, DN_HG, seq, DN_HEAD_DIM), F32),
            pltpu.VMEM((2, DN_HG, nchunk, 8, LANES), F32),
            pltpu.VMEM((2, DN_HG, CHUNK, DN_HEAD_DIM), F32),
        ],
        compiler_params=pltpu.CompilerParams(
            dimension_semantics=("arbitrary", "arbitrary"), vmem_limit_bytes=VMEM_LIMIT_BYTES),
        name="deltanet",
    )(h_all, h_all, h_all, h_all, g_nat, g_t, cw, cw, cw, prow, pcol, norm_w)


def _merge_kernel(x_ref, yna_ref, ydn_ref, gna_ref, gdn_ref, wpn_ref, wpd_ref, wo_ref,
                  g1_ref, b1_ref, wr_ref, br_ref, x1_ref, x1b_ref, comb_ref):
    a = _dot(yna_ref[...], wpn_ref[...])
    b = _dot(ydn_ref[...], wpd_ref[...])
    merged = _sigmoid(gna_ref[...].astype(F32)) * a + _sigmoid(gdn_ref[...].astype(F32)) * b
    mix = _dot(merged.astype(BF16), wo_ref[...])
    x1 = _layer_norm(DEEPNORM_ALPHA * x_ref[...] + mix, g1_ref[...], b1_ref[...])
    x1_ref[...] = x1
    x1b_ref[...] = x1.astype(BF16)

    logits = _dot_f32(x1, wr_ref[...]) + br_ref[...]
    lane = lax.broadcasted_iota(jnp.int32, logits.shape, 1)
    lanef = lane.astype(F32)
    big = float(ROUTER_LANES)
    gmask = lane < N_GROUPS
    gl = jnp.where(gmask, logits, -jnp.inf)
    gmax = jnp.max(gl, -1, keepdims=True)
    gidx = jnp.min(jnp.where(gl == gmax, lanef, big), -1, keepdims=True)
    pg = 1.0 / jnp.sum(jnp.where(gmask, jnp.exp(gl - gmax), 0.0), -1, keepdims=True)
    egrp = jnp.floor((lanef - N_GROUPS) * (1.0 / EXPERTS_PER_GROUP))
    emask = (lane >= N_GROUPS) & (lane < N_GROUPS + N_EXPERTS) & (egrp == gidx)
    el = jnp.where(emask, logits, -jnp.inf)
    m1 = jnp.max(el, -1, keepdims=True)
    i1 = jnp.min(jnp.where(el == m1, lanef, big), -1, keepdims=True)
    el2 = jnp.where(lanef == i1, -jnp.inf, el)
    m2 = jnp.max(el2, -1, keepdims=True)
    i2 = jnp.min(jnp.where(el2 == m2, lanef, big), -1, keepdims=True)
    t = jnp.exp(m2 - m1)
    p1 = pg / (1.0 + t)
    p2 = pg * t / (1.0 + t)
    comb_ref[...] = jnp.where(lanef == i1, p1, 0.0) + jnp.where(lanef == i2, p2, 0.0)


def _merge_call(x2, y_na, y_dn, h_all, wpn, wpd, wo, g1, b1, wr, br, tm=512):
    n = x2.shape[0]
    const = lambda i: (0, 0)
    return pl.pallas_call(
        _merge_kernel,
        grid=(n // tm,),
        in_specs=[
            pl.BlockSpec((tm, D_MODEL), lambda i: (i, 0)),
            pl.BlockSpec((tm, NA_WIDTH), lambda i: (i, 0)),
            pl.BlockSpec((tm, DN_WIDTH), lambda i: (i, 0)),
            pl.BlockSpec((tm, D_MODEL), lambda i: (i, COL_GNA // D_MODEL)),
            pl.BlockSpec((tm, D_MODEL), lambda i: (i, COL_GDN // D_MODEL)),
            pl.BlockSpec((NA_WIDTH, D_MODEL), const),
            pl.BlockSpec((DN_WIDTH, D_MODEL), const),
            pl.BlockSpec((D_MODEL, D_MODEL), const),
            pl.BlockSpec((1, D_MODEL), const),
            pl.BlockSpec((1, D_MODEL), const),
            pl.BlockSpec((D_MODEL, ROUTER_LANES), const),
            pl.BlockSpec((1, ROUTER_LANES), const),
        ],
        out_specs=[
            pl.BlockSpec((tm, D_MODEL), lambda i: (i, 0)),
            pl.BlockSpec((tm, D_MODEL), lambda i: (i, 0)),
            pl.BlockSpec((tm, ROUTER_LANES), lambda i: (i, 0)),
        ],
        out_shape=[
            jax.ShapeDtypeStruct((n, D_MODEL), F32),
            jax.ShapeDtypeStruct((n, D_MODEL), BF16),
            jax.ShapeDtypeStruct((n, ROUTER_LANES), F32),
        ],
        compiler_params=pltpu.CompilerParams(
            dimension_semantics=("arbitrary",), vmem_limit_bytes=VMEM_LIMIT_BYTES),
        name="merge_ln1_router",
    )(x2, y_na, y_dn, h_all, h_all, wpn, wpd, wo, g1, b1, wr, br)


def _moe_kernel(x1_ref, x1b_ref, comb_ref, wgu_ref, wd_ref, g2_ref, b2_ref, o_ref, acc_ref):
    e = pl.program_id(1)

    @pl.when(e == 0)
    def _():
        acc_ref[...] = jnp.zeros_like(acc_ref)

    comb = comb_ref[...]
    lane = lax.broadcasted_iota(jnp.int32, comb.shape, 1)
    ce = jnp.sum(jnp.where(lane == e + N_GROUPS, comb, 0.0), -1, keepdims=True)
    hgu = _dot(x1b_ref[...], wgu_ref[0])
    hid = _silu(hgu[:, :D_EXPERT]) * hgu[:, D_EXPERT:] * ce
    acc_ref[...] += _dot(hid.astype(BF16), wd_ref[0])

    @pl.when(e == N_EXPERTS - 1)
    def _():
        o_ref[...] = _layer_norm(DEEPNORM_ALPHA * x1_ref[...] + acc_ref[...], g2_ref[...], b2_ref[...])


def _moe_call(x1, x1b, comb, wgu, wd, g2, b2, tm=1024):
    n = x1.shape[0]
    return pl.pallas_call(
        _moe_kernel,
        grid=(n // tm, N_EXPERTS),
        in_specs=[
            pl.BlockSpec((tm, D_MODEL), lambda i, e: (i, 0)),
            pl.BlockSpec((tm, D_MODEL), lambda i, e: (i, 0)),
            pl.BlockSpec((tm, ROUTER_LANES), lambda i, e: (i, 0)),
            pl.BlockSpec((1, D_MODEL, 2 * D_EXPERT), lambda i, e: (e, 0, 0)),
            pl.BlockSpec((1, D_EXPERT, D_MODEL), lambda i, e: (e, 0, 0)),
            pl.BlockSpec((1, D_MODEL), lambda i, e: (0, 0)),
            pl.BlockSpec((1, D_MODEL), lambda i, e: (0, 0)),
        ],
        out_specs=pl.BlockSpec((tm, D_MODEL), lambda i, e: (i, 0)),
        out_shape=jax.ShapeDtypeStruct((n, D_MODEL), F32),
        scratch_shapes=[pltpu.VMEM((tm, D_MODEL), F32)],
        compiler_params=pltpu.CompilerParams(
            dimension_semantics=("arbitrary", "arbitrary"), vmem_limit_bytes=VMEM_LIMIT_BYTES),
        name="moe_ln2",
    )(x1, x1b, comb, wgu, wd, g2, b2)


def _layer(x2, batch, seq, w_in, na_rpb, dn_conv_w, a_log_f, a_log_b, dt_bias_f, dt_bias_b, dn_norm_w,
           w_proj_na, w_proj_dn, w_out, ln1_g, ln1_b, w_rg, b_rg, w_re, b_re, w_gu, w_dn, ln2_g, ln2_b):
    n_act = 3 * NA_WIDTH + 4 * DN_WIDTH
    n_small = 4 * DN_HEADS
    w_main = jnp.concatenate([w_in[:, n_act + n_small:], w_in[:, :n_act]], 1).astype(BF16)
    ws = w_in[:, n_act:n_act + n_small].reshape(D_MODEL, 4, DN_HEADS // DN_HG, DN_HG)
    w_g = ws.transpose(2, 0, 1, 3).reshape(DN_HEADS // DN_HG, D_MODEL, 4 * DN_HG).astype(BF16)
    w_gt = w_g.transpose(0, 2, 1)

    def per_group(f, b):
        z = jnp.zeros((DN_HEADS // DN_HG, 2 * DN_HG), F32)
        return jnp.concatenate([z, f.reshape(-1, DN_HG), b.reshape(-1, DN_HG)], 1)

    pa, pd = per_group(a_log_f, a_log_b), per_group(dt_bias_f, dt_bias_b)
    prow = jnp.stack([pa, pd], 1)
    pcol = jnp.stack([pa, pd], 2)

    h_all, g_nat, g_t = _proj_call(x2, w_main, w_g, w_gt)
    bias = _na_bias_tables(na_rpb, seq // GRID_W)
    y_na = _na_call(h_all, bias, batch, seq)
    y_dn = _dn_call(h_all, g_nat, g_t, dn_conv_w, prow, pcol, dn_norm_w.reshape(1, DN_HEAD_DIM), batch, seq)

    wr = jnp.zeros((D_MODEL, ROUTER_LANES), F32)
    wr = wr.at[:, :N_GROUPS].set(w_rg).at[:, N_GROUPS:N_GROUPS + N_EXPERTS].set(w_re)
    br = jnp.zeros((1, ROUTER_LANES), F32)
    br = br.at[0, :N_GROUPS].set(b_rg).at[0, N_GROUPS:N_GROUPS + N_EXPERTS].set(b_re)
    x1, x1b, comb = _merge_call(x2, y_na, y_dn, h_all, w_proj_na.astype(BF16), w_proj_dn.astype(BF16),
                                w_out.astype(BF16), ln1_g.reshape(1, -1), ln1_b.reshape(1, -1), wr, br)
    return _moe_call(x1, x1b, comb, w_gu.astype(BF16), w_dn.astype(BF16),
                     ln2_g.reshape(1, -1), ln2_b.reshape(1, -1))


def kernel(x, w_in, na_rpb, dn_conv_w, dn_a_log_f, dn_a_log_b, dn_dt_bias_f, dn_dt_bias_b, dn_norm_w, w_proj_na, w_proj_dn, w_out, ln1_g, ln1_b, w_router_group, b_router_group, w_router_expert, b_router_expert, w_expert_gate_up, w_expert_down, ln2_g, ln2_b):
    batch, seq, d = x.shape
    x2 = x.reshape(batch * seq, d)
    for l in range(w_in.shape[0]):
        x2 = _layer(x2, batch, seq, w_in[l], na_rpb[l], dn_conv_w[l], dn_a_log_f[l], dn_a_log_b[l],
                    dn_dt_bias_f[l], dn_dt_bias_b[l], dn_norm_w[l], w_proj_na[l], w_proj_dn[l], w_out[l],
                    ln1_g[l], ln1_b[l], w_router_group[l], b_router_group[l], w_router_expert[l],
                    b_router_expert[l], w_expert_gate_up[l], w_expert_down[l], ln2_g[l], ln2_b[l])
    return x2.reshape(batch, seq, d)
```

```python
import functools

import numpy as np
import jax
import jax.numpy as jnp
from jax import lax
from jax.experimental import pallas as pl
from jax.experimental.pallas import tpu as pltpu

F32 = jnp.float32
BF16 = jnp.bfloat16

D_MODEL = 1024
GRID_W = 64
NA_HEADS = 8
NA_HEAD_DIM = 64
NA_WIN_ROWS = 8
NA_WIN_COLS = 16
NA_WIDTH = NA_HEADS * NA_HEAD_DIM
DN_HEADS = 8
DN_HEAD_DIM = 64
DN_WIDTH = DN_HEADS * DN_HEAD_DIM
DN_CONV = 5
CHUNK = 64
N_GROUPS = 4
EXPERTS_PER_GROUP = 8
N_EXPERTS = N_GROUPS * EXPERTS_PER_GROUP
D_EXPERT = 256
TOP_K = 2
DEPTH = 1
DEEPNORM_ALPHA = (2.0 * DEPTH) ** 0.25
LN_EPS = 1e-5
RMS_EPS = 1e-6

LANES = 128
VMEM_LIMIT_BYTES = 56 * 1024 * 1024

COL_GNA, COL_GDN = 0, 1024
COL_QNA, COL_KNA, COL_VNA = 2048, 2560, 3072
COL_QDN, COL_KDN, COL_VDN, COL_ZDN = 3584, 4096, 4608, 5120
H_COLS = 5632
PROJ_CHUNK = 512

NA_QROWS = 4
NA_KROWS = 12
NA_TQ = NA_QROWS * GRID_W
NA_TK = NA_KROWS * GRID_W

DN_HG = 4
DN_GW = DN_HG * DN_HEAD_DIM
DN_PREP_UNROLL = 2
ROUTER_LANES = 128
NEG_BIG = -1e30
TOK_SUB = D_MODEL // LANES
MOE_TM = 256


def _sigmoid(x):
    return 1.0 / (1.0 + jnp.exp(-x))


def _silu(x):
    return x * _sigmoid(x)


def _softplus(x):
    return jnp.maximum(x, 0.0) + jnp.log(1.0 + jnp.exp(-jnp.abs(x)))


def _split3(x):
    x1 = x.astype(BF16)
    r1 = x - x1.astype(F32)
    x2 = r1.astype(BF16)
    r2 = r1 - x2.astype(F32)
    return x1, x2, r2.astype(BF16)


def _dot(a, b):
    return jnp.dot(a, b, preferred_element_type=F32)


def _dot_nt(a, b):
    return lax.dot_general(a, b, (((1,), (1,)), ((), ())), preferred_element_type=F32)


def _dot_exact_lhs(a_bf16_exact, x):
    x1, x2, x3 = _split3(x)
    return _dot(a_bf16_exact, x1) + _dot(a_bf16_exact, x2) + _dot(a_bf16_exact, x3)


def _dot_exact_rhs(x, a_bf16_exact):
    x1, x2, x3 = _split3(x)
    return _dot(x1, a_bf16_exact) + _dot(x2, a_bf16_exact) + _dot(x3, a_bf16_exact)


def _dot_f32(a, b):
    a1, a2, a3 = _split3(a)
    b1, b2, b3 = _split3(b)
    return (_dot(a1, b1) + (_dot(a1, b2) + _dot(a2, b1))
            + (_dot(a1, b3) + _dot(a3, b1) + _dot(a2, b2)))


def _layer_norm(r, g, b):
    mu = jnp.mean(r, -1, keepdims=True)
    d = r - mu
    var = jnp.mean(d * d, -1, keepdims=True)
    return d * lax.rsqrt(var + LN_EPS) * g + b


def _proj_kernel(x_ref, w_ref, wg_ref, wgt_ref, h_ref, g_ref, gt_ref):
    xb = x_ref[...].astype(BF16)
    for c in range(H_COLS // PROJ_CHUNK):
        cs = slice(c * PROJ_CHUNK, (c + 1) * PROJ_CHUNK)
        h_ref[:, cs] = _dot(xb, w_ref[:, cs]).astype(BF16)
    for hg in range(DN_HEADS // DN_HG):
        g_ref[hg] = _dot(xb, wg_ref[hg])
        gt_ref[hg] = _dot_nt(wgt_ref[hg], xb)


def _proj_call(x2, w_main, w_g, w_gt, tm=512):
    n = x2.shape[0]
    ng = DN_HEADS // DN_HG
    gw = 4 * DN_HG
    return pl.pallas_call(
        _proj_kernel,
        grid=(n // tm,),
        in_specs=[
            pl.BlockSpec((tm, D_MODEL), lambda i: (i, 0)),
            pl.BlockSpec((D_MODEL, H_COLS), lambda i: (0, 0)),
            pl.BlockSpec((ng, D_MODEL, gw), lambda i: (0, 0, 0)),
            pl.BlockSpec((ng, gw, D_MODEL), lambda i: (0, 0, 0)),
        ],
        out_specs=[
            pl.BlockSpec((tm, H_COLS), lambda i: (i, 0)),
            pl.BlockSpec((ng, tm, gw), lambda i: (0, i, 0)),
            pl.BlockSpec((ng, gw, tm), lambda i: (0, 0, i)),
        ],
        out_shape=[
            jax.ShapeDtypeStruct((n, H_COLS), BF16),
            jax.ShapeDtypeStruct((ng, n, gw), F32),
            jax.ShapeDtypeStruct((ng, gw, n), F32),
        ],
        compiler_params=pltpu.CompilerParams(
            dimension_semantics=("arbitrary",), vmem_limit_bytes=VMEM_LIMIT_BYTES),
        name="proj",
    )(x2, w_main, w_g, w_gt)


def _na_bias_tables(rpb, rows):
    kr_win = min(NA_WIN_ROWS, rows)
    c = np.arange(GRID_W)
    dc = np.clip(c[None, :] - c[:, None], -(NA_WIN_COLS - 1), NA_WIN_COLS - 1) + (NA_WIN_COLS - 1)
    onehot = (dc[None] == np.arange(2 * NA_WIN_COLS - 1)[:, None, None]).astype(np.float32)
    tz = jnp.einsum('hrd,dqk->hrqk', rpb, onehot, precision=lax.Precision.HIGHEST)
    col_start = np.clip(c - NA_WIN_COLS // 2, 0, GRID_W - NA_WIN_COLS)
    col_ok = (c[None, :] >= col_start[:, None]) & (c[None, :] < col_start[:, None] + NA_WIN_COLS)
    tables = []
    for r0 in (0, 2 * NA_QROWS, rows - NA_QROWS):
        kw0 = int(np.clip(r0 - NA_WIN_ROWS // 2, 0, rows - NA_KROWS))
        qr = r0 + np.arange(NA_QROWS)
        kr = kw0 + np.arange(NA_KROWS)
        row_start = np.clip(qr - kr_win // 2, 0, rows - kr_win)
        row_ok = (kr[None, :] >= row_start[:, None]) & (kr[None, :] < row_start[:, None] + kr_win)
        dr = np.clip(kr[None, :] - qr[:, None] + (NA_WIN_ROWS - 1), 0, 2 * NA_WIN_ROWS - 2)
        blk = jnp.stack([jnp.stack([tz[:, dr[i, j]] for j in range(NA_KROWS)], 1)
                         for i in range(NA_QROWS)], 1)
        valid = row_ok[:, :, None, None] & col_ok[None, None]
        b = jnp.where(valid[None], blk, NEG_BIG).transpose(0, 1, 3, 2, 4)
        tables.append(b.reshape(rpb.shape[0], NA_TQ, NA_TK))
    return jnp.stack(tables, 0).astype(F32)


def _na_kernel(q_ref, k_ref, v_ref, bias_ref, o_ref, *, rows):
    qt = pl.program_id(2)
    kw0 = jnp.clip(qt * NA_QROWS - NA_WIN_ROWS // 2, 0, rows - NA_KROWS) * GRID_W
    kw0 = pl.multiple_of(kw0, GRID_W)
    scale = NA_HEAD_DIM ** -0.5
    for hh in range(LANES // NA_HEAD_DIM):
        ls = slice(hh * NA_HEAD_DIM, (hh + 1) * NA_HEAD_DIM)
        q = q_ref[:, ls]
        k = k_ref[pl.ds(kw0, NA_TK), ls]
        v = v_ref[pl.ds(kw0, NA_TK), ls]
        s = _dot_nt(q, k) * scale + bias_ref[0, hh]
        m = jnp.max(s, -1, keepdims=True)
        p = jnp.exp(s - m)
        l = jnp.sum(p, -1, keepdims=True)
        o = _dot(p.astype(BF16), v) / l
        o_ref[:, ls] = o.astype(BF16)


def _na_call(h_all, bias, batch, seq):
    rows = seq // GRID_W
    nqt = rows // NA_QROWS
    hp = NA_WIDTH // LANES
    n = batch * seq

    def bias_idx(b, p, t):
        return (jnp.where(t == 0, 0, jnp.where(t == nqt - 1, 2, 1)), p, 0, 0)

    return pl.pallas_call(
        functools.partial(_na_kernel, rows=rows),
        grid=(batch, hp, nqt),
        in_specs=[
            pl.BlockSpec((NA_TQ, LANES), lambda b, p, t: (b * nqt + t, COL_QNA // LANES + p)),
            pl.BlockSpec((seq, LANES), lambda b, p, t: (b, COL_KNA // LANES + p)),
            pl.BlockSpec((seq, LANES), lambda b, p, t: (b, COL_VNA // LANES + p)),
            pl.BlockSpec((1, LANES // NA_HEAD_DIM, NA_TQ, NA_TK), bias_idx),
        ],
        out_specs=pl.BlockSpec((NA_TQ, LANES), lambda b, p, t: (b * nqt + t, p)),
        out_shape=jax.ShapeDtypeStruct((n, NA_WIDTH), BF16),
        compiler_params=pltpu.CompilerParams(
            dimension_semantics=("arbitrary", "arbitrary", "arbitrary"),
            vmem_limit_bytes=VMEM_LIMIT_BYTES),
        name="natten",
    )(h_all, h_all, h_all, bias)


def _dn_kernel(q_ref, k_ref, v_ref, z_ref, g_ref, gt_ref, cwq_ref, cwk_ref, cwv_ref,
               prow_ref, pcol_ref, nw_ref, o_ref,
               qs, ks, vs, gn_s, gt_s, wq_s, ik_s, u_s, eg_s, st_s, *, seq):
    nchunk = seq // CHUNK
    hd = DN_HEAD_DIM
    rb = 256

    lane16 = lax.broadcasted_iota(jnp.int32, (1, 4 * DN_HG), 1)
    graw = g_ref[0]
    gl = -jnp.exp(prow_ref[0, 0:1, :]) * _softplus(graw + prow_ref[0, 1:2, :])
    gn_s[...] = jnp.where(lane16 < 2 * DN_HG, _sigmoid(graw), gl)
    sub16 = lax.broadcasted_iota(jnp.int32, (4 * DN_HG, 1), 0)
    grawt = gt_ref[0]
    glt = -jnp.exp(pcol_ref[0, :, 0:1]) * _softplus(grawt + pcol_ref[0, :, 1:2])
    gtt = jnp.where(sub16 < 2 * DN_HG, _sigmoid(grawt), glt)
    for n in range(nchunk):
        gt_s[n] = gtt[:, n * CHUNK:(n + 1) * CHUNK]

    pad = 16
    half = DN_CONV // 2
    for src, cw_ref, dst, norm, mul in ((q_ref, cwq_ref, qs, True, hd ** -0.5),
                                        (k_ref, cwk_ref, ks, True, 1.0),
                                        (v_ref, cwv_ref, vs, False, 1.0)):
        cw = cw_ref[...]
        for r0 in range(0, seq, rb):
            lo, hi = r0 - pad, r0 + rb + pad
            parts = []
            if lo < 0:
                parts.append(jnp.zeros((pad, DN_GW), F32))
            parts.append(src[max(lo, 0):min(hi, seq), :].astype(F32))
            if hi > seq:
                parts.append(jnp.zeros((pad, DN_GW), F32))
            xin = jnp.concatenate(parts, 0) if len(parts) > 1 else parts[0]
            nrow = rb + 2 * pad
            y = jnp.zeros((rb, DN_GW), F32)
            for i in range(DN_CONV):
                sh = (half - i) % nrow
                xs = xin if sh == 0 else pltpu.roll(xin, sh, 0)
                y = y + xs[pad:pad + rb, :] * cw[i:i + 1, :]
            y = _silu(y)
            for hl in range(DN_HG):
                seg = y[:, hl * hd:(hl + 1) * hd]
                if norm:
                    seg = seg * (lax.rsqrt(jnp.sum(seg * seg, -1, keepdims=True) + RMS_EPS) * mul)
                dst[hl, r0:r0 + rb, :] = seg.astype(BF16)

    ri = lax.broadcasted_iota(jnp.int32, (CHUNK, CHUNK), 0)
    ci = lax.broadcasted_iota(jnp.int32, (CHUNK, CHUNK), 1)
    eye = (ri == ci).astype(F32)
    tril = (ri >= ci).astype(BF16)
    triu = (ri <= ci).astype(BF16)
    lower_incl, lower_strict = ri >= ci, ri > ci
    upper_incl, upper_strict = ri <= ci, ri < ci
    level_masks = []
    s = 1
    while s < CHUNK:
        sh = s.bit_length() - 1
        level_masks.append((((ri >> (sh + 1)) == (ci >> (sh + 1))) & ((ri >> sh) != (ci >> sh))).astype(F32))
        s *= 2

    dirs = ((lower_incl, lower_strict, CHUNK - 1), (upper_incl, upper_strict, 0))

    def prep(it, carry):
        probs = []
        for c in range(DN_PREP_UNROLL):
            n = it * DN_PREP_UNROLL + c
            r0 = pl.multiple_of(n * CHUNK, CHUNK)
            gcol = gn_s[pl.ds(r0, CHUNK), :]
            grow = gt_s[n]
            cs_col = (_dot_exact_lhs(tril, gcol), _dot_exact_lhs(triu, gcol))
            cs_row = (_dot_exact_rhs(grow, triu), _dot_exact_rhs(grow, tril))
            for hl in range(DN_HG):
                q = qs[hl, pl.ds(r0, CHUNK), :]
                k = ks[hl, pl.ds(r0, CHUNK), :]
                v = vs[hl, pl.ds(r0, CHUNK), :]
                pr = _dot_nt(jnp.concatenate([q, k], 0), k)
                for d in range(2):
                    col = 2 * DN_HG + d * DN_HG + hl
                    probs.append(dict(n=n, r0=r0, hl=hl, d=d, q=q, k=k, v=v, pr=pr,
                                      bcol=gcol[:, d * DN_HG + hl:d * DN_HG + hl + 1],
                                      ccol=cs_col[d][:, col:col + 1], crow=cs_row[d][col:col + 1, :]))
        for p in probs:
            incl, strict, last = dirs[p["d"]]
            kf, vf, qf = p["k"].astype(F32), p["v"].astype(F32), p["q"].astype(F32)
            qk, kk = p["pr"][:CHUNK], p["pr"][CHUNK:]
            bcol, ccol = p["bcol"], p["ccol"]
            dec = jnp.exp(jnp.where(incl, ccol - p["crow"], -jnp.inf))
            p["nmat"] = jnp.where(strict, bcol * kk * dec, 0.0)
            p["intra"] = (qk * dec).astype(BF16)
            e = jnp.exp(ccol)
            p["rhs"] = jnp.concatenate([vf * bcol, kf * (bcol * e)], 1).astype(BF16)
            glast = ccol[last:last + 1, :]
            p["qdec"] = (qf * e).astype(BF16)
            p["kdect"] = (kf * jnp.exp(glast - ccol)).T.astype(BF16)
            p["eg"] = jnp.broadcast_to(jnp.exp(glast), (8, LANES))
            p["t"] = eye - p["nmat"] * level_masks[0]
        for m in level_masks[1:]:
            xs = [_dot((p["nmat"] * m).astype(BF16), p["t"].astype(BF16)) for p in probs]
            ys = [_dot(p["t"].astype(BF16), x.astype(BF16)) for p, x in zip(probs, xs)]
            for p, y in zip(probs, ys):
                p["t"] = p["t"] - y
        sols = [_dot(p["t"].astype(BF16), p["rhs"]) for p in probs]
        for p, sol in zip(probs, sols):
            d, hl, n = p["d"], p["hl"], p["n"]
            u_s[d, hl, pl.ds(p["r0"], CHUNK), :] = sol[:, :hd]
            wq_s[d, hl, n, 0:CHUNK, :] = sol[:, hd:].astype(BF16)
            wq_s[d, hl, n, CHUNK:2 * CHUNK, :] = p["qdec"]
            ik_s[d, hl, n, 0:CHUNK, :] = p["intra"]
            ik_s[d, hl, n, CHUNK:2 * CHUNK, :] = p["kdect"]
            eg_s[d, hl, n] = p["eg"]
        return carry

    lax.fori_loop(0, nchunk // DN_PREP_UNROLL, prep, 0)

    st_s[...] = jnp.zeros_like(st_s)

    def scan(i, carry):
        probs = []
        for d in range(2):
            n = i if d == 0 else nchunk - 1 - i
            r0 = pl.multiple_of(n * CHUNK, CHUNK)
            for hl in range(DN_HG):
                probs.append((d, hl, n, r0))
        sts = [st_s[d, hl] for d, hl, n, r0 in probs]
        ts = [_dot(wq_s[d, hl, n], st.astype(BF16))
              for (d, hl, n, r0), st in zip(probs, sts)]
        vnews = [u_s[d, hl, pl.ds(r0, CHUNK), :] - t[:CHUNK] for (d, hl, n, r0), t in zip(probs, ts)]
        t2s = [_dot(ik_s[d, hl, n], vn.astype(BF16))
               for (d, hl, n, r0), vn in zip(probs, vnews)]
        for (d, hl, n, r0), st, t, t2 in zip(probs, sts, ts, t2s):
            u_s[d, hl, pl.ds(r0, CHUNK), :] = t[CHUNK:] + t2[:CHUNK]
            st_s[d, hl] = st * eg_s[d, hl, n][0:1, 0:hd] + t2[CHUNK:]
        return carry

    lax.fori_loop(0, nchunk, scan, 0)

    nw = nw_ref[...]
    for r0 in range(0, seq, rb):
        for hl in range(DN_HG):
            o = u_s[0, hl, r0:r0 + rb, :] + u_s[1, hl, r0:r0 + rb, :]
            o = o * lax.rsqrt(jnp.mean(o * o, -1, keepdims=True) + RMS_EPS) * nw
            zz = z_ref[r0:r0 + rb, hl * hd:(hl + 1) * hd].astype(F32)
            o_ref[r0:r0 + rb, hl * hd:(hl + 1) * hd] = (o * _silu(zz)).astype(BF16)


def _dn_call(h_all, g_nat, g_t, cw, prow, pcol, norm_w, batch, seq):
    n = batch * seq
    ng = DN_HEADS // DN_HG
    gw = 4 * DN_HG
    nchunk = seq // CHUNK
    cb = DN_GW // LANES

    def col(c0):
        return lambda b, g: (b, c0 // DN_GW + g)

    return pl.pallas_call(
        functools.partial(_dn_kernel, seq=seq),
        grid=(batch, ng),
        in_specs=[
            pl.BlockSpec((seq, DN_GW), col(COL_QDN)),
            pl.BlockSpec((seq, DN_GW), col(COL_KDN)),
            pl.BlockSpec((seq, DN_GW), col(COL_VDN)),
            pl.BlockSpec((seq, DN_GW), col(COL_ZDN)),
            pl.BlockSpec((1, seq, gw), lambda b, g: (g, b, 0)),
            pl.BlockSpec((1, gw, seq), lambda b, g: (g, 0, b)),
            pl.BlockSpec((DN_CONV, DN_GW), lambda b, g: (0, g)),
            pl.BlockSpec((DN_CONV, DN_GW), lambda b, g: (0, DN_WIDTH // DN_GW + g)),
            pl.BlockSpec((DN_CONV, DN_GW), lambda b, g: (0, 2 * DN_WIDTH // DN_GW + g)),
            pl.BlockSpec((1, 2, gw), lambda b, g: (g, 0, 0)),
            pl.BlockSpec((1, gw, 2), lambda b, g: (g, 0, 0)),
            pl.BlockSpec((1, DN_HEAD_DIM), lambda b, g: (0, 0)),
        ],
        out_specs=pl.BlockSpec((seq, DN_GW), lambda b, g: (b, g)),
        out_shape=jax.ShapeDtypeStruct((n, DN_WIDTH), BF16),
        scratch_shapes=[
            pltpu.VMEM((DN_HG, seq, DN_HEAD_DIM), BF16),
            pltpu.VMEM((DN_HG, seq, DN_HEAD_DIM), BF16),
            pltpu.VMEM((DN_HG, seq, DN_HEAD_DIM), BF16),
            pltpu.VMEM((seq, gw), F32),
            pltpu.VMEM((nchunk, gw, CHUNK), F32),
            pltpu.VMEM((2, DN_HG, nchunk, 2 * CHUNK, DN_HEAD_DIM), BF16),
            pltpu.VMEM((2, DN_HG, nchunk, 2 * CHUNK, DN_HEAD_DIM), BF16),
            pltpu.VMEM((2, DN_HG, seq, DN_HEAD_DIM), F32),
            pltpu.VMEM((2, DN_HG, nchunk, 8, LANES), F32),
            pltpu.VMEM((2, DN_HG, CHUNK, DN_HEAD_DIM), F32),
        ],
        compiler_params=pltpu.CompilerParams(
            dimension_semantics=("arbitrary", "arbitrary"), vmem_limit_bytes=VMEM_LIMIT_BYTES),
        name="deltanet",
    )(h_all, h_all, h_all, h_all, g_nat, g_t, cw, cw, cw, prow, pcol, norm_w)


def _merge_kernel(x_ref, yna_ref, ydn_ref, gna_ref, gdn_ref, wpn_ref, wpd_ref, wo_ref,
                  g1_ref, b1_ref, wr_ref, br_ref, x1t_ref, route_ref, cnt_ref, carry_ref):
    a = _dot(yna_ref[...], wpn_ref[...])
    b = _dot(ydn_ref[...], wpd_ref[...])
    merged = _sigmoid(gna_ref[...].astype(F32)) * a + _sigmoid(gdn_ref[...].astype(F32)) * b
    mix = _dot(merged.astype(BF16), wo_ref[...])
    x1 = _layer_norm(DEEPNORM_ALPHA * x_ref[...] + mix, g1_ref[...], b1_ref[...])
    for kk in range(TOK_SUB):
        x1t_ref[:, kk, :] = x1[:, kk * LANES:(kk + 1) * LANES]

    logits = _dot_f32(x1, wr_ref[...]) + br_ref[...]
    tm = logits.shape[0]
    lane = lax.broadcasted_iota(jnp.int32, logits.shape, 1)
    lanef = lane.astype(F32)
    big = float(ROUTER_LANES)
    gmask = lane < N_GROUPS
    gl = jnp.where(gmask, logits, -jnp.inf)
    gmax = jnp.max(gl, -1, keepdims=True)
    gidx = jnp.min(jnp.where(gl == gmax, lanef, big), -1, keepdims=True)
    pg = 1.0 / jnp.sum(jnp.where(gmask, jnp.exp(gl - gmax), 0.0), -1, keepdims=True)
    egrp = jnp.floor((lanef - N_GROUPS) * (1.0 / EXPERTS_PER_GROUP))
    emask = (lane >= N_GROUPS) & (lane < N_GROUPS + N_EXPERTS) & (egrp == gidx)
    el = jnp.where(emask, logits, -jnp.inf)
    m1 = jnp.max(el, -1, keepdims=True)
    i1 = jnp.min(jnp.where(el == m1, lanef, big), -1, keepdims=True)
    el2 = jnp.where(lanef == i1, -jnp.inf, el)
    m2 = jnp.max(el2, -1, keepdims=True)
    i2 = jnp.min(jnp.where(el2 == m2, lanef, big), -1, keepdims=True)
    t = jnp.exp(m2 - m1)
    p1 = pg / (1.0 + t)
    p2 = pg * t / (1.0 + t)

    @pl.when(pl.program_id(0) == 0)
    def _():
        carry_ref[...] = jnp.zeros_like(carry_ref)

    oh1 = (lanef == i1).astype(F32)
    oh2 = (lanef == i2).astype(F32)
    oh = oh1 + oh2
    ri = lax.broadcasted_iota(jnp.int32, (tm, tm), 0)
    ci = lax.broadcasted_iota(jnp.int32, (tm, tm), 1)
    before = _dot((ri > ci).astype(BF16), oh.astype(BF16)) + carry_ref[...]
    rank1 = jnp.sum(before * oh1, -1, keepdims=True)
    rank2 = jnp.sum(before * oh2, -1, keepdims=True)
    carry = carry_ref[...] + jnp.sum(oh, 0, keepdims=True)
    carry_ref[...] = carry
    cnt_ref[...] = jnp.broadcast_to(carry, cnt_ref.shape)
    route = jnp.zeros_like(logits)
    for j, col in enumerate((i1 - N_GROUPS, i2 - N_GROUPS, p1, p2, rank1, rank2)):
        route = jnp.where(lane == j, col, route)
    route_ref[...] = route


def _merge_call(x2, y_na, y_dn, h_all, wpn, wpd, wo, g1, b1, wr, br, tm=512):
    n = x2.shape[0]
    const = lambda i: (0, 0)
    return pl.pallas_call(
        _merge_kernel,
        grid=(n // tm,),
        in_specs=[
            pl.BlockSpec((tm, D_MODEL), lambda i: (i, 0)),
            pl.BlockSpec((tm, NA_WIDTH), lambda i: (i, 0)),
            pl.BlockSpec((tm, DN_WIDTH), lambda i: (i, 0)),
            pl.BlockSpec((tm, D_MODEL), lambda i: (i, COL_GNA // D_MODEL)),
            pl.BlockSpec((tm, D_MODEL), lambda i: (i, COL_GDN // D_MODEL)),
            pl.BlockSpec((NA_WIDTH, D_MODEL), const),
            pl.BlockSpec((DN_WIDTH, D_MODEL), const),
            pl.BlockSpec((D_MODEL, D_MODEL), const),
            pl.BlockSpec((1, D_MODEL), const),
            pl.BlockSpec((1, D_MODEL), const),
            pl.BlockSpec((D_MODEL, ROUTER_LANES), const),
            pl.BlockSpec((1, ROUTER_LANES), const),
        ],
        out_specs=[
            pl.BlockSpec((tm, TOK_SUB, LANES), lambda i: (i, 0, 0)),
            pl.BlockSpec((tm, ROUTER_LANES), lambda i: (i, 0)),
            pl.BlockSpec((8, ROUTER_LANES), const),
        ],
        out_shape=[
            jax.ShapeDtypeStruct((n, TOK_SUB, LANES), F32),
            jax.ShapeDtypeStruct((n, ROUTER_LANES), F32),
            jax.ShapeDtypeStruct((8, ROUTER_LANES), F32),
        ],
        scratch_shapes=[pltpu.VMEM((1, ROUTER_LANES), F32)],
        compiler_params=pltpu.CompilerParams(
            dimension_semantics=("arbitrary",), vmem_limit_bytes=VMEM_LIMIT_BYTES),
        name="merge_ln1_router",
    )(x2, y_na, y_dn, h_all, h_all, wpn, wpd, wo, g1, b1, wr, br)


def _moe_tiles(n):
    return (TOP_K * n) // MOE_TM + N_EXPERTS


def _plan_kernel(route_ref, cnt_ref, pos_ref, meta_ref):
    cnt_col = cnt_ref[...].T[:, 0:1]
    sub = lax.broadcasted_iota(jnp.int32, (ROUTER_LANES, 1), 0)
    is_e = (sub >= N_GROUPS) & (sub < N_GROUPS + N_EXPERTS)
    padded = jnp.where(is_e, jnp.ceil(cnt_col * (1.0 / MOE_TM)) * MOE_TM, 0.0)
    ri = lax.broadcasted_iota(jnp.int32, (ROUTER_LANES, ROUTER_LANES), 0)
    ci = lax.broadcasted_iota(jnp.int32, (ROUTER_LANES, ROUTER_LANES), 1)
    pb = jnp.broadcast_to(padded, (ROUTER_LANES, ROUTER_LANES))
    off = _dot_exact_lhs((ri > ci).astype(BF16), pb)[:, 0:1]
    end = off + padded

    rt = route_ref[...].T
    tm = rt.shape[1]
    subt = lax.broadcasted_iota(jnp.int32, (ROUTER_LANES, tm), 0).astype(F32) - N_GROUPS
    rows = []
    for k in range(TOP_K):
        e_row, r_row = rt[k:k + 1, :], rt[4 + k:5 + k, :]
        start = jnp.sum(jnp.where(subt == e_row, off, 0.0), 0, keepdims=True)
        rows.append(start + r_row)
    pos = jnp.concatenate(rows + [jnp.zeros((8 - TOP_K, tm), F32)], 0)
    pos_ref[...] = pos.astype(jnp.int32)

    tl = lax.broadcasted_iota(jnp.int32, (ROUTER_LANES, meta_ref.shape[1]), 1).astype(F32) * MOE_TM
    tile_e = jnp.sum(jnp.where(is_e & (end <= tl), 1.0, 0.0), 0, keepdims=True)
    ntile = jnp.max(end, 0, keepdims=True) * (1.0 / MOE_TM)
    row = lax.broadcasted_iota(jnp.int32, meta_ref.shape, 0)
    meta = jnp.where(row == 0, jnp.minimum(tile_e, N_EXPERTS - 1.0), jnp.where(row == 1, ntile, 0.0))
    meta_ref[...] = meta.astype(jnp.int32)


def _plan_call(route, counts, tm=512):
    n = route.shape[0]
    mt = -(-_moe_tiles(n) // LANES) * LANES
    return pl.pallas_call(
        _plan_kernel,
        grid=(n // tm,),
        in_specs=[pl.BlockSpec((tm, ROUTER_LANES), lambda i: (i, 0)),
                  pl.BlockSpec((8, ROUTER_LANES), lambda i: (0, 0))],
        out_specs=[pl.BlockSpec((8, tm), lambda i: (0, i)),
                   pl.BlockSpec((8, mt), lambda i: (0, 0))],
        out_shape=[jax.ShapeDtypeStruct((8, n), jnp.int32),
                   jax.ShapeDtypeStruct((8, mt), jnp.int32)],
        compiler_params=pltpu.CompilerParams(dimension_semantics=("arbitrary",)),
        name="moe_plan",
    )(route, counts)


def _row_copy(src, dst, src_row, dst_row, sem):
    return pltpu.make_async_copy(src.at[src_row], dst.at[dst_row], sem)


def _dispatch_kernel(pos_ref, x1t_hbm, xs_in_hbm, xs_hbm, sem, *, n, tb):
    del xs_in_hbm
    base = pl.program_id(0) * tb

    def issue(t, c):
        for k in range(TOP_K):
            _row_copy(x1t_hbm, xs_hbm, base + t, pos_ref[k * n + base + t], sem).start()
        return c

    lax.fori_loop(0, tb, issue, 0, unroll=8)

    def drain(t, c):
        for k in range(TOP_K):
            _row_copy(x1t_hbm, xs_hbm, 0, 0, sem).wait()
        return c

    lax.fori_loop(0, tb, drain, 0, unroll=8)


def _dispatch_call(pos_flat, x1t, tb=1024):
    n = x1t.shape[0]
    rows = _moe_tiles(n) * MOE_TM
    xs0 = jnp.zeros((rows, TOK_SUB, LANES), F32)
    return pl.pallas_call(
        functools.partial(_dispatch_kernel, n=n, tb=tb),
        grid_spec=pltpu.PrefetchScalarGridSpec(
            num_scalar_prefetch=1,
            grid=(n // tb,),
            in_specs=[pl.BlockSpec(memory_space=pl.ANY), pl.BlockSpec(memory_space=pl.ANY)],
            out_specs=pl.BlockSpec(memory_space=pl.ANY),
            scratch_shapes=[pltpu.SemaphoreType.DMA],
        ),
        out_shape=jax.ShapeDtypeStruct((rows, TOK_SUB, LANES), F32),
        input_output_aliases={2: 0},
        compiler_params=pltpu.CompilerParams(dimension_semantics=("arbitrary",), has_side_effects=True),
        name="moe_dispatch",
    )(pos_flat, x1t, xs0)


def _experts_kernel(te_ref, nt_ref, xs_ref, wgu_ref, wd_ref, ys_ref):
    @pl.when(pl.program_id(0) < nt_ref[0])
    def _():
        hgu = jnp.zeros((MOE_TM, 2 * D_EXPERT), F32)
        for kk in range(TOK_SUB):
            hgu = hgu + _dot(xs_ref[:, kk, :].astype(BF16), wgu_ref[0, kk * LANES:(kk + 1) * LANES, :])
        hid = _silu(hgu[:, :D_EXPERT]) * hgu[:, D_EXPERT:]
        y = _dot(hid.astype(BF16), wd_ref[0])
        for kk in range(TOK_SUB):
            ys_ref[:, kk, :] = y[:, kk * LANES:(kk + 1) * LANES]


def _experts_call(tile_e, ntile, xs, wgu, wd):
    nt_max = xs.shape[0] // MOE_TM

    def tile(j, te, nt):
        return (jnp.minimum(j, nt[0] - 1), 0, 0)

    def expert(j, te, nt):
        return (te[jnp.minimum(j, nt[0] - 1)], 0, 0)

    return pl.pallas_call(
        _experts_kernel,
        grid_spec=pltpu.PrefetchScalarGridSpec(
            num_scalar_prefetch=2,
            grid=(nt_max,),
            in_specs=[pl.BlockSpec((MOE_TM, TOK_SUB, LANES), tile),
                      pl.BlockSpec((1, D_MODEL, 2 * D_EXPERT), expert),
                      pl.BlockSpec((1, D_EXPERT, D_MODEL), expert)],
            out_specs=pl.BlockSpec((MOE_TM, TOK_SUB, LANES), tile),
        ),
        out_shape=jax.ShapeDtypeStruct(xs.shape, F32),
        input_output_aliases={2: 0},
        compiler_params=pltpu.CompilerParams(
            dimension_semantics=("arbitrary",), vmem_limit_bytes=VMEM_LIMIT_BYTES),
        name="moe_experts",
    )(tile_e, ntile, xs, wgu, wd)


def _combine_kernel(pos_ref, x1t_ref, route_ref, g2_ref, b2_ref, ys_hbm, o_ref, gbuf, sem, *, n, tc):
    base = pl.program_id(0) * tc

    def issue(t, c):
        for k in range(TOP_K):
            _row_copy(ys_hbm, gbuf.at[k], pos_ref[k * n + base + t], t, sem).start()
        return c

    lax.fori_loop(0, tc, issue, 0, unroll=8)

    def drain(t, c):
        for k in range(TOP_K):
            _row_copy(ys_hbm, gbuf.at[k], 0, 0, sem).wait()
        return c

    lax.fori_loop(0, tc, drain, 0, unroll=8)

    route = route_ref[...]
    p1, p2 = route[:, 2:3], route[:, 3:4]
    parts = []
    for kk in range(TOK_SUB):
        ffn = p1 * gbuf[0, :, kk, :] + p2 * gbuf[1, :, kk, :]
        parts.append(DEEPNORM_ALPHA * x1t_ref[:, kk, :] + ffn)
    o_ref[...] = _layer_norm(jnp.concatenate(parts, -1), g2_ref[...], b2_ref[...])


def _combine_call(pos_flat, x1t, route, ys, g2, b2, tc=512):
    n = x1t.shape[0]
    return pl.pallas_call(
        functools.partial(_combine_kernel, n=n, tc=tc),
        grid_spec=pltpu.PrefetchScalarGridSpec(
            num_scalar_prefetch=1,
            grid=(n // tc,),
            in_specs=[pl.BlockSpec((tc, TOK_SUB, LANES), lambda i, p: (i, 0, 0)),
                      pl.BlockSpec((tc, ROUTER_LANES), lambda i, p: (i, 0)),
                      pl.BlockSpec((1, D_MODEL), lambda i, p: (0, 0)),
                      pl.BlockSpec((1, D_MODEL), lambda i, p: (0, 0)),
                      pl.BlockSpec(memory_space=pl.ANY)],
            out_specs=pl.BlockSpec((tc, D_MODEL), lambda i, p: (i, 0)),
            scratch_shapes=[pltpu.VMEM((TOP_K, tc, TOK_SUB, LANES), F32), pltpu.SemaphoreType.DMA],
        ),
        out_shape=jax.ShapeDtypeStruct((n, D_MODEL), F32),
        compiler_params=pltpu.CompilerParams(
            dimension_semantics=("arbitrary",), vmem_limit_bytes=VMEM_LIMIT_BYTES),
        name="moe_combine_ln2",
    )(pos_flat, x1t, route, g2, b2, ys)


def _layer(x2, batch, seq, w_in, na_rpb, dn_conv_w, a_log_f, a_log_b, dt_bias_f, dt_bias_b, dn_norm_w,
           w_proj_na, w_proj_dn, w_out, ln1_g, ln1_b, w_rg, b_rg, w_re, b_re, w_gu, w_dn, ln2_g, ln2_b):
    n_act = 3 * NA_WIDTH + 4 * DN_WIDTH
    n_small = 4 * DN_HEADS
    w_main = jnp.concatenate([w_in[:, n_act + n_small:], w_in[:, :n_act]], 1).astype(BF16)
    ws = w_in[:, n_act:n_act + n_small].reshape(D_MODEL, 4, DN_HEADS // DN_HG, DN_HG)
    w_g = ws.transpose(2, 0, 1, 3).reshape(DN_HEADS // DN_HG, D_MODEL, 4 * DN_HG).astype(BF16)
    w_gt = w_g.transpose(0, 2, 1)

    def per_group(f, b):
        z = jnp.zeros((DN_HEADS // DN_HG, 2 * DN_HG), F32)
        return jnp.concatenate([z, f.reshape(-1, DN_HG), b.reshape(-1, DN_HG)], 1)

    pa, pd = per_group(a_log_f, a_log_b), per_group(dt_bias_f, dt_bias_b)
    prow = jnp.stack([pa, pd], 1)
    pcol = jnp.stack([pa, pd], 2)

    h_all, g_nat, g_t = _proj_call(x2, w_main, w_g, w_gt)
    bias = _na_bias_tables(na_rpb, seq // GRID_W)
    y_na = _na_call(h_all, bias, batch, seq)
    y_dn = _dn_call(h_all, g_nat, g_t, dn_conv_w, prow, pcol, dn_norm_w.reshape(1, DN_HEAD_DIM), batch, seq)

    wr = jnp.zeros((D_MODEL, ROUTER_LANES), F32)
    wr = wr.at[:, :N_GROUPS].set(w_rg).at[:, N_GROUPS:N_GROUPS + N_EXPERTS].set(w_re)
    br = jnp.zeros((1, ROUTER_LANES), F32)
    br = br.at[0, :N_GROUPS].set(b_rg).at[0, N_GROUPS:N_GROUPS + N_EXPERTS].set(b_re)
    x1t, route, counts = _merge_call(x2, y_na, y_dn, h_all, w_proj_na.astype(BF16), w_proj_dn.astype(BF16),
                                     w_out.astype(BF16), ln1_g.reshape(1, -1), ln1_b.reshape(1, -1), wr, br)
    pos, meta = _plan_call(route, counts)
    pos_flat = pos[:TOP_K].reshape(-1)
    xs = _dispatch_call(pos_flat, x1t)
    ys = _experts_call(meta[0, :_moe_tiles(x2.shape[0])], meta[1, :1], xs, w_gu.astype(BF16), w_dn.astype(BF16))
    return _combine_call(pos_flat, x1t, route, ys, ln2_g.reshape(1, -1), ln2_b.reshape(1, -1))


def kernel(x, w_in, na_rpb, dn_conv_w, dn_a_log_f, dn_a_log_b, dn_dt_bias_f, dn_dt_bias_b, dn_norm_w, w_proj_na, w_proj_dn, w_out, ln1_g, ln1_b, w_router_group, b_router_group, w_router_expert, b_router_expert, w_expert_gate_up, w_expert_down, ln2_g, ln2_b):
    batch, seq, d = x.shape
    x2 = x.reshape(batch * seq, d)
    for l in range(w_in.shape[0]):
        x2 = _layer(x2, batch, seq, w_in[l], na_rpb[l], dn_conv_w[l], dn_a_log_f[l], dn_a_log_b[l],
                    dn_dt_bias_f[l], dn_dt_bias_b[l], dn_norm_w[l], w_proj_na[l], w_proj_dn[l], w_out[l],
                    ln1_g[l], ln1_b[l], w_router_group[l], b_router_group[l], w_router_expert[l],
                    b_router_expert[l], w_expert_gate_up[l], w_expert_down[l], ln2_g[l], ln2_b[l])
    return x2.reshape(batch, seq, d)
```

```python
import functools

import numpy as np
import jax
import jax.numpy as jnp
from jax import lax
from jax.experimental import pallas as pl
from jax.experimental.pallas import tpu as pltpu

F32 = jnp.float32
BF16 = jnp.bfloat16

D_MODEL = 1024
GRID_W = 64
NA_HEADS = 8
NA_HEAD_DIM = 64
NA_WIN_ROWS = 8
NA_WIN_COLS = 16
NA_WIDTH = NA_HEADS * NA_HEAD_DIM
DN_HEADS = 8
DN_HEAD_DIM = 64
DN_WIDTH = DN_HEADS * DN_HEAD_DIM
DN_CONV = 5
CHUNK = 64
N_GROUPS = 4
EXPERTS_PER_GROUP = 8
N_EXPERTS = N_GROUPS * EXPERTS_PER_GROUP
D_EXPERT = 256
TOP_K = 2
DEPTH = 1
DEEPNORM_ALPHA = (2.0 * DEPTH) ** 0.25
LN_EPS = 1e-5
RMS_EPS = 1e-6

LANES = 128
VMEM_LIMIT_BYTES = 56 * 1024 * 1024

COL_GNA, COL_GDN = 0, 1024
COL_QNA, COL_KNA, COL_VNA = 2048, 2560, 3072
COL_QDN, COL_KDN, COL_VDN, COL_ZDN = 3584, 4096, 4608, 5120
H_COLS = 5632
PROJ_CHUNK = 512

NA_QROWS = 4
NA_KROWS = 12
NA_TQ = NA_QROWS * GRID_W
NA_TK = NA_KROWS * GRID_W

DN_HG = 4
DN_GW = DN_HG * DN_HEAD_DIM
DN_PREP_UNROLL = 2
ROUTER_LANES = 128
NEG_BIG = -1e30
TOK_SUB = D_MODEL // LANES
MOE_TM = 256


def _sigmoid(x):
    return 1.0 / (1.0 + jnp.exp(-x))


def _silu(x):
    return x * _sigmoid(x)


def _softplus(x):
    return jnp.maximum(x, 0.0) + jnp.log(1.0 + jnp.exp(-jnp.abs(x)))


def _split3(x):
    x1 = x.astype(BF16)
    r1 = x - x1.astype(F32)
    x2 = r1.astype(BF16)
    r2 = r1 - x2.astype(F32)
    return x1, x2, r2.astype(BF16)


def _dot(a, b):
    return jnp.dot(a, b, preferred_element_type=F32)


def _dot_nt(a, b):
    return lax.dot_general(a, b, (((1,), (1,)), ((), ())), preferred_element_type=F32)


def _dot_exact_lhs(a_bf16_exact, x):
    x1, x2, x3 = _split3(x)
    return _dot(a_bf16_exact, x1) + _dot(a_bf16_exact, x2) + _dot(a_bf16_exact, x3)


def _dot_exact_rhs(x, a_bf16_exact):
    x1, x2, x3 = _split3(x)
    return _dot(x1, a_bf16_exact) + _dot(x2, a_bf16_exact) + _dot(x3, a_bf16_exact)


def _dot_f32(a, b):
    a1, a2, a3 = _split3(a)
    b1, b2, b3 = _split3(b)
    return (_dot(a1, b1) + (_dot(a1, b2) + _dot(a2, b1))
            + (_dot(a1, b3) + _dot(a3, b1) + _dot(a2, b2)))


def _layer_norm(r, g, b):
    mu = jnp.mean(r, -1, keepdims=True)
    d = r - mu
    var = jnp.mean(d * d, -1, keepdims=True)
    return d * lax.rsqrt(var + LN_EPS) * g + b


def _proj_kernel(x_ref, w_ref, wg_ref, wgt_ref, h_ref, g_ref, gt_ref):
    xb = x_ref[...].astype(BF16)
    for c in range(H_COLS // PROJ_CHUNK):
        cs = slice(c * PROJ_CHUNK, (c + 1) * PROJ_CHUNK)
        h_ref[:, cs] = _dot(xb, w_ref[:, cs]).astype(BF16)
    for hg in range(DN_HEADS // DN_HG):
        g_ref[hg] = _dot(xb, wg_ref[hg])
        gt_ref[hg] = _dot_nt(wgt_ref[hg], xb)


def _proj_call(x2, w_main, w_g, w_gt, tm=512):
    n = x2.shape[0]
    ng = DN_HEADS // DN_HG
    gw = 4 * DN_HG
    return pl.pallas_call(
        _proj_kernel,
        grid=(n // tm,),
        in_specs=[
            pl.BlockSpec((tm, D_MODEL), lambda i: (i, 0)),
            pl.BlockSpec((D_MODEL, H_COLS), lambda i: (0, 0)),
            pl.BlockSpec((ng, D_MODEL, gw), lambda i: (0, 0, 0)),
            pl.BlockSpec((ng, gw, D_MODEL), lambda i: (0, 0, 0)),
        ],
        out_specs=[
            pl.BlockSpec((tm, H_COLS), lambda i: (i, 0)),
            pl.BlockSpec((ng, tm, gw), lambda i: (0, i, 0)),
            pl.BlockSpec((ng, gw, tm), lambda i: (0, 0, i)),
        ],
        out_shape=[
            jax.ShapeDtypeStruct((n, H_COLS), BF16),
            jax.ShapeDtypeStruct((ng, n, gw), F32),
            jax.ShapeDtypeStruct((ng, gw, n), F32),
        ],
        compiler_params=pltpu.CompilerParams(
            dimension_semantics=("arbitrary",), vmem_limit_bytes=VMEM_LIMIT_BYTES),
        name="proj",
    )(x2, w_main, w_g, w_gt)


def _na_bias_tables(rpb, rows):
    kr_win = min(NA_WIN_ROWS, rows)
    c = np.arange(GRID_W)
    dc = np.clip(c[None, :] - c[:, None], -(NA_WIN_COLS - 1), NA_WIN_COLS - 1) + (NA_WIN_COLS - 1)
    col_hot = (dc[None] == np.arange(2 * NA_WIN_COLS - 1)[:, None, None]).astype(np.float32)
    col_start = np.clip(c - NA_WIN_COLS // 2, 0, GRID_W - NA_WIN_COLS)
    col_ok = (c[None, :] >= col_start[:, None]) & (c[None, :] < col_start[:, None] + NA_WIN_COLS)
    row_hot, valid = [], []
    for r0 in (0, 2 * NA_QROWS, rows - NA_QROWS):
        kw0 = int(np.clip(r0 - NA_WIN_ROWS // 2, 0, rows - NA_KROWS))
        qr = r0 + np.arange(NA_QROWS)
        kr = kw0 + np.arange(NA_KROWS)
        row_start = np.clip(qr - kr_win // 2, 0, rows - kr_win)
        row_ok = (kr[None, :] >= row_start[:, None]) & (kr[None, :] < row_start[:, None] + kr_win)
        dr = kr[None, :] - qr[:, None] + (NA_WIN_ROWS - 1)
        row_hot.append(((dr[:, :, None] == np.arange(2 * NA_WIN_ROWS - 1)) & row_ok[:, :, None]).astype(np.float32))
        valid.append(row_ok[:, None, :, None] & col_ok[None, :, None, :])
    row_hot = np.stack(row_hot)
    valid = np.stack(valid).reshape(3, 1, NA_TQ, NA_TK)
    tab = jnp.einsum('vijr,hrd,dqk->vhiqjk', row_hot, rpb, col_hot, precision=lax.Precision.HIGHEST)
    return jnp.where(valid, tab.reshape(3, rpb.shape[0], NA_TQ, NA_TK), NEG_BIG).astype(F32)


def _na_kernel(q_ref, k_ref, v_ref, bias_ref, o_ref, *, rows):
    qt = pl.program_id(2)
    kw0 = jnp.clip(qt * NA_QROWS - NA_WIN_ROWS // 2, 0, rows - NA_KROWS) * GRID_W
    kw0 = pl.multiple_of(kw0, GRID_W)
    scale = NA_HEAD_DIM ** -0.5
    for hh in range(LANES // NA_HEAD_DIM):
        ls = slice(hh * NA_HEAD_DIM, (hh + 1) * NA_HEAD_DIM)
        q = q_ref[:, ls]
        k = k_ref[pl.ds(kw0, NA_TK), ls]
        v = v_ref[pl.ds(kw0, NA_TK), ls]
        s = _dot_nt(q, k) * scale + bias_ref[0, hh]
        m = jnp.max(s, -1, keepdims=True)
        p = jnp.exp(s - m)
        l = jnp.sum(p, -1, keepdims=True)
        o = _dot(p.astype(BF16), v) / l
        o_ref[:, ls] = o.astype(BF16)


def _na_call(h_all, bias, batch, seq):
    rows = seq // GRID_W
    nqt = rows // NA_QROWS
    hp = NA_WIDTH // LANES
    n = batch * seq

    def bias_idx(b, p, t):
        return (jnp.where(t == 0, 0, jnp.where(t == nqt - 1, 2, 1)), p, 0, 0)

    return pl.pallas_call(
        functools.partial(_na_kernel, rows=rows),
        grid=(batch, hp, nqt),
        in_specs=[
            pl.BlockSpec((NA_TQ, LANES), lambda b, p, t: (b * nqt + t, COL_QNA // LANES + p)),
            pl.BlockSpec((seq, LANES), lambda b, p, t: (b, COL_KNA // LANES + p)),
            pl.BlockSpec((seq, LANES), lambda b, p, t: (b, COL_VNA // LANES + p)),
            pl.BlockSpec((1, LANES // NA_HEAD_DIM, NA_TQ, NA_TK), bias_idx),
        ],
        out_specs=pl.BlockSpec((NA_TQ, LANES), lambda b, p, t: (b * nqt + t, p)),
        out_shape=jax.ShapeDtypeStruct((n, NA_WIDTH), BF16),
        compiler_params=pltpu.CompilerParams(
            dimension_semantics=("arbitrary", "arbitrary", "arbitrary"),
            vmem_limit_bytes=VMEM_LIMIT_BYTES),
        name="natten",
    )(h_all, h_all, h_all, bias)


def _dn_kernel(q_ref, k_ref, v_ref, z_ref, g_ref, gt_ref, cwq_ref, cwk_ref, cwv_ref,
               prow_ref, pcol_ref, nw_ref, o_ref,
               qs, ks, vs, gn_s, gt_s, wq_s, ik_s, u_s, eg_s, st_s, *, seq):
    nchunk = seq // CHUNK
    hd = DN_HEAD_DIM
    rb = 256

    lane16 = lax.broadcasted_iota(jnp.int32, (1, 4 * DN_HG), 1)
    graw = g_ref[0]
    gl = -jnp.exp(prow_ref[0, 0:1, :]) * _softplus(graw + prow_ref[0, 1:2, :])
    gn_s[...] = jnp.where(lane16 < 2 * DN_HG, _sigmoid(graw), gl)
    sub16 = lax.broadcasted_iota(jnp.int32, (4 * DN_HG, 1), 0)
    grawt = gt_ref[0]
    glt = -jnp.exp(pcol_ref[0, :, 0:1]) * _softplus(grawt + pcol_ref[0, :, 1:2])
    gtt = jnp.where(sub16 < 2 * DN_HG, _sigmoid(grawt), glt)
    for n in range(nchunk):
        gt_s[n] = gtt[:, n * CHUNK:(n + 1) * CHUNK]

    pad = 16
    half = DN_CONV // 2
    for src, cw_ref, dst, norm, mul in ((q_ref, cwq_ref, qs, True, hd ** -0.5),
                                        (k_ref, cwk_ref, ks, True, 1.0),
                                        (v_ref, cwv_ref, vs, False, 1.0)):
        cw = cw_ref[...]
        for r0 in range(0, seq, rb):
            lo, hi = r0 - pad, r0 + rb + pad
            parts = []
            if lo < 0:
                parts.append(jnp.zeros((pad, DN_GW), F32))
            parts.append(src[max(lo, 0):min(hi, seq), :].astype(F32))
            if hi > seq:
                parts.append(jnp.zeros((pad, DN_GW), F32))
            xin = jnp.concatenate(parts, 0) if len(parts) > 1 else parts[0]
            nrow = rb + 2 * pad
            y = jnp.zeros((rb, DN_GW), F32)
            for i in range(DN_CONV):
                sh = (half - i) % nrow
                xs = xin if sh == 0 else pltpu.roll(xin, sh, 0)
                y = y + xs[pad:pad + rb, :] * cw[i:i + 1, :]
            y = _silu(y)
            for hl in range(DN_HG):
                seg = y[:, hl * hd:(hl + 1) * hd]
                if norm:
                    seg = seg * (lax.rsqrt(jnp.sum(seg * seg, -1, keepdims=True) + RMS_EPS) * mul)
                dst[hl, r0:r0 + rb, :] = seg.astype(BF16)

    ri = lax.broadcasted_iota(jnp.int32, (CHUNK, CHUNK), 0)
    ci = lax.broadcasted_iota(jnp.int32, (CHUNK, CHUNK), 1)
    eye = (ri == ci).astype(F32)
    tril = (ri >= ci).astype(BF16)
    triu = (ri <= ci).astype(BF16)
    lower_incl, lower_strict = ri >= ci, ri > ci
    upper_incl, upper_strict = ri <= ci, ri < ci
    level_masks = []
    s = 1
    while s < CHUNK:
        sh = s.bit_length() - 1
        level_masks.append((((ri >> (sh + 1)) == (ci >> (sh + 1))) & ((ri >> sh) != (ci >> sh))).astype(F32))
        s *= 2

    dirs = ((lower_incl, lower_strict, CHUNK - 1), (upper_incl, upper_strict, 0))

    def prep(it, carry):
        probs = []
        for c in range(DN_PREP_UNROLL):
            n = it * DN_PREP_UNROLL + c
            r0 = pl.multiple_of(n * CHUNK, CHUNK)
            gcol = gn_s[pl.ds(r0, CHUNK), :]
            grow = gt_s[n]
            cs_col = (_dot_exact_lhs(tril, gcol), _dot_exact_lhs(triu, gcol))
            cs_row = (_dot_exact_rhs(grow, triu), _dot_exact_rhs(grow, tril))
            for hl in range(DN_HG):
                q = qs[hl, pl.ds(r0, CHUNK), :]
                k = ks[hl, pl.ds(r0, CHUNK), :]
                v = vs[hl, pl.ds(r0, CHUNK), :]
                pr = _dot_nt(jnp.concatenate([q, k], 0), k)
                for d in range(2):
                    col = 2 * DN_HG + d * DN_HG + hl
                    probs.append(dict(n=n, r0=r0, hl=hl, d=d, q=q, k=k, v=v, pr=pr,
                                      bcol=gcol[:, d * DN_HG + hl:d * DN_HG + hl + 1],
                                      ccol=cs_col[d][:, col:col + 1], crow=cs_row[d][col:col + 1, :]))
        for p in probs:
            incl, strict, last = dirs[p["d"]]
            kf, vf, qf = p["k"].astype(F32), p["v"].astype(F32), p["q"].astype(F32)
            qk, kk = p["pr"][:CHUNK], p["pr"][CHUNK:]
            bcol, ccol = p["bcol"], p["ccol"]
            dec = jnp.exp(jnp.where(incl, ccol - p["crow"], -jnp.inf))
            p["nmat"] = jnp.where(strict, bcol * kk * dec, 0.0)
            p["intra"] = (qk * dec).astype(BF16)
            e = jnp.exp(ccol)
            p["rhs"] = jnp.concatenate([vf * bcol, kf * (bcol * e)], 1).astype(BF16)
            glast = ccol[last:last + 1, :]
            p["qdec"] = (qf * e).astype(BF16)
            p["kdect"] = (kf * jnp.exp(glast - ccol)).T.astype(BF16)
            p["eg"] = jnp.broadcast_to(jnp.exp(glast), (8, LANES))
            p["t"] = eye - p["nmat"] * level_masks[0]
        for m in level_masks[1:]:
            xs = [_dot((p["nmat"] * m).astype(BF16), p["t"].astype(BF16)) for p in probs]
            ys = [_dot(p["t"].astype(BF16), x.astype(BF16)) for p, x in zip(probs, xs)]
            for p, y in zip(probs, ys):
                p["t"] = p["t"] - y
        sols = [_dot(p["t"].astype(BF16), p["rhs"]) for p in probs]
        for p, sol in zip(probs, sols):
            d, hl, n = p["d"], p["hl"], p["n"]
            u_s[d, hl, pl.ds(p["r0"], CHUNK), :] = sol[:, :hd]
            wq_s[d, hl, n, 0:CHUNK, :] = sol[:, hd:].astype(BF16)
            wq_s[d, hl, n, CHUNK:2 * CHUNK, :] = p["qdec"]
            ik_s[d, hl, n, 0:CHUNK, :] = p["intra"]
            ik_s[d, hl, n, CHUNK:2 * CHUNK, :] = p["kdect"]
            eg_s[d, hl, n] = p["eg"]
        return carry

    lax.fori_loop(0, nchunk // DN_PREP_UNROLL, prep, 0)

    st_s[...] = jnp.zeros_like(st_s)

    def scan(i, carry):
        probs = []
        for d in range(2):
            n = i if d == 0 else nchunk - 1 - i
            r0 = pl.multiple_of(n * CHUNK, CHUNK)
            for hl in range(DN_HG):
                probs.append((d, hl, n, r0))
        sts = [st_s[d, hl] for d, hl, n, r0 in probs]
        ts = [_dot(wq_s[d, hl, n], st.astype(BF16))
              for (d, hl, n, r0), st in zip(probs, sts)]
        vnews = [u_s[d, hl, pl.ds(r0, CHUNK), :] - t[:CHUNK] for (d, hl, n, r0), t in zip(probs, ts)]
        t2s = [_dot(ik_s[d, hl, n], vn.astype(BF16))
               for (d, hl, n, r0), vn in zip(probs, vnews)]
        for (d, hl, n, r0), st, t, t2 in zip(probs, sts, ts, t2s):
            u_s[d, hl, pl.ds(r0, CHUNK), :] = t[CHUNK:] + t2[:CHUNK]
            st_s[d, hl] = st * eg_s[d, hl, n][0:1, 0:hd] + t2[CHUNK:]
        return carry

    lax.fori_loop(0, nchunk, scan, 0)

    nw = nw_ref[...]
    for r0 in range(0, seq, rb):
        for hl in range(DN_HG):
            o = u_s[0, hl, r0:r0 + rb, :] + u_s[1, hl, r0:r0 + rb, :]
            o = o * lax.rsqrt(jnp.mean(o * o, -1, keepdims=True) + RMS_EPS) * nw
            zz = z_ref[r0:r0 + rb, hl * hd:(hl + 1) * hd].astype(F32)
            o_ref[r0:r0 + rb, hl * hd:(hl + 1) * hd] = (o * _silu(zz)).astype(BF16)


def _dn_call(h_all, g_nat, g_t, cw, prow, pcol, norm_w, batch, seq):
    n = batch * seq
    ng = DN_HEADS // DN_HG
    gw = 4 * DN_HG
    nchunk = seq // CHUNK
    cb = DN_GW // LANES

    def col(c0):
        return lambda b, g: (b, c0 // DN_GW + g)

    return pl.pallas_call(
        functools.partial(_dn_kernel, seq=seq),
        grid=(batch, ng),
        in_specs=[
            pl.BlockSpec((seq, DN_GW), col(COL_QDN)),
            pl.BlockSpec((seq, DN_GW), col(COL_KDN)),
            pl.BlockSpec((seq, DN_GW), col(COL_VDN)),
            pl.BlockSpec((seq, DN_GW), col(COL_ZDN)),
            pl.BlockSpec((1, seq, gw), lambda b, g: (g, b, 0)),
            pl.BlockSpec((1, gw, seq), lambda b, g: (g, 0, b)),
            pl.BlockSpec((DN_CONV, DN_GW), lambda b, g: (0, g)),
            pl.BlockSpec((DN_CONV, DN_GW), lambda b, g: (0, DN_WIDTH // DN_GW + g)),
            pl.BlockSpec((DN_CONV, DN_GW), lambda b, g: (0, 2 * DN_WIDTH // DN_GW + g)),
            pl.BlockSpec((1, 2, gw), lambda b, g: (g, 0, 0)),
            pl.BlockSpec((1, gw, 2), lambda b, g: (g, 0, 0)),
            pl.BlockSpec((1, DN_HEAD_DIM), lambda b, g: (0, 0)),
        ],
        out_specs=pl.BlockSpec((seq, DN_GW), lambda b, g: (b, g)),
        out_shape=jax.ShapeDtypeStruct((n, DN_WIDTH), BF16),
        scratch_shapes=[
            pltpu.VMEM((DN_HG, seq, DN_HEAD_DIM), BF16),
            pltpu.VMEM((DN_HG, seq, DN_HEAD_DIM), BF16),
            pltpu.VMEM((DN_HG, seq, DN_HEAD_DIM), BF16),
            pltpu.VMEM((seq, gw), F32),
            pltpu.VMEM((nchunk, gw, CHUNK), F32),
            pltpu.VMEM((2, DN_HG, nchunk, 2 * CHUNK, DN_HEAD_DIM), BF16),
            pltpu.VMEM((2, DN_HG, nchunk, 2 * CHUNK, DN_HEAD_DIM), BF16),
            pltpu.VMEM((2, DN_HG, seq, DN_HEAD_DIM), F32),
            pltpu.VMEM((2, DN_HG, nchunk, 8, LANES), F32),
            pltpu.VMEM((2, DN_HG, CHUNK, DN_HEAD_DIM), F32),
        ],
        compiler_params=pltpu.CompilerParams(
            dimension_semantics=("arbitrary", "arbitrary"), vmem_limit_bytes=VMEM_LIMIT_BYTES),
        name="deltanet",
    )(h_all, h_all, h_all, h_all, g_nat, g_t, cw, cw, cw, prow, pcol, norm_w)


def _merge_kernel(x_ref, yna_ref, ydn_ref, gna_ref, gdn_ref, wpn_ref, wpd_ref, wo_ref,
                  g1_ref, b1_ref, wr_ref, br_ref, x1t_ref, route_ref, cnt_ref, carry_ref):
    a = _dot(yna_ref[...], wpn_ref[...])
    b = _dot(ydn_ref[...], wpd_ref[...])
    merged = _sigmoid(gna_ref[...].astype(F32)) * a + _sigmoid(gdn_ref[...].astype(F32)) * b
    mix = _dot(merged.astype(BF16), wo_ref[...])
    x1 = _layer_norm(DEEPNORM_ALPHA * x_ref[...] + mix, g1_ref[...], b1_ref[...])
    for kk in range(TOK_SUB):
        x1t_ref[:, kk, :] = x1[:, kk * LANES:(kk + 1) * LANES]

    logits = _dot_f32(x1, wr_ref[...]) + br_ref[...]
    tm = logits.shape[0]
    lane = lax.broadcasted_iota(jnp.int32, logits.shape, 1)
    lanef = lane.astype(F32)
    big = float(ROUTER_LANES)
    gmask = lane < N_GROUPS
    gl = jnp.where(gmask, logits, -jnp.inf)
    gmax = jnp.max(gl, -1, keepdims=True)
    gidx = jnp.min(jnp.where(gl == gmax, lanef, big), -1, keepdims=True)
    pg = 1.0 / jnp.sum(jnp.where(gmask, jnp.exp(gl - gmax), 0.0), -1, keepdims=True)
    egrp = jnp.floor((lanef - N_GROUPS) * (1.0 / EXPERTS_PER_GROUP))
    emask = (lane >= N_GROUPS) & (lane < N_GROUPS + N_EXPERTS) & (egrp == gidx)
    el = jnp.where(emask, logits, -jnp.inf)
    m1 = jnp.max(el, -1, keepdims=True)
    i1 = jnp.min(jnp.where(el == m1, lanef, big), -1, keepdims=True)
    el2 = jnp.where(lanef == i1, -jnp.inf, el)
    m2 = jnp.max(el2, -1, keepdims=True)
    i2 = jnp.min(jnp.where(el2 == m2, lanef, big), -1, keepdims=True)
    t = jnp.exp(m2 - m1)
    p1 = pg / (1.0 + t)
    p2 = pg * t / (1.0 + t)

    @pl.when(pl.program_id(0) == 0)
    def _():
        carry_ref[...] = jnp.zeros_like(carry_ref)

    oh1 = (lanef == i1).astype(F32)
    oh2 = (lanef == i2).astype(F32)
    oh = oh1 + oh2
    ri = lax.broadcasted_iota(jnp.int32, (tm, tm), 0)
    ci = lax.broadcasted_iota(jnp.int32, (tm, tm), 1)
    before = _dot((ri > ci).astype(BF16), oh.astype(BF16)) + carry_ref[...]
    rank1 = jnp.sum(before * oh1, -1, keepdims=True)
    rank2 = jnp.sum(before * oh2, -1, keepdims=True)
    carry = carry_ref[...] + jnp.sum(oh, 0, keepdims=True)
    carry_ref[...] = carry
    cnt_ref[...] = jnp.broadcast_to(carry, cnt_ref.shape)
    route = jnp.zeros_like(logits)
    for j, col in enumerate((i1 - N_GROUPS, i2 - N_GROUPS, p1, p2, rank1, rank2)):
        route = jnp.where(lane == j, col, route)
    route_ref[...] = route


def _merge_call(x2, y_na, y_dn, h_all, wpn, wpd, wo, g1, b1, wr, br, tm=512):
    n = x2.shape[0]
    const = lambda i: (0, 0)
    return pl.pallas_call(
        _merge_kernel,
        grid=(n // tm,),
        in_specs=[
            pl.BlockSpec((tm, D_MODEL), lambda i: (i, 0)),
            pl.BlockSpec((tm, NA_WIDTH), lambda i: (i, 0)),
            pl.BlockSpec((tm, DN_WIDTH), lambda i: (i, 0)),
            pl.BlockSpec((tm, D_MODEL), lambda i: (i, COL_GNA // D_MODEL)),
            pl.BlockSpec((tm, D_MODEL), lambda i: (i, COL_GDN // D_MODEL)),
            pl.BlockSpec((NA_WIDTH, D_MODEL), const),
            pl.BlockSpec((DN_WIDTH, D_MODEL), const),
            pl.BlockSpec((D_MODEL, D_MODEL), const),
            pl.BlockSpec((1, D_MODEL), const),
            pl.BlockSpec((1, D_MODEL), const),
            pl.BlockSpec((D_MODEL, ROUTER_LANES), const),
            pl.BlockSpec((1, ROUTER_LANES), const),
        ],
        out_specs=[
            pl.BlockSpec((tm, TOK_SUB, LANES), lambda i: (i, 0, 0)),
            pl.BlockSpec((tm, ROUTER_LANES), lambda i: (i, 0)),
            pl.BlockSpec((8, ROUTER_LANES), const),
        ],
        out_shape=[
            jax.ShapeDtypeStruct((n, TOK_SUB, LANES), F32),
            jax.ShapeDtypeStruct((n, ROUTER_LANES), F32),
            jax.ShapeDtypeStruct((8, ROUTER_LANES), F32),
        ],
        scratch_shapes=[pltpu.VMEM((1, ROUTER_LANES), F32)],
        compiler_params=pltpu.CompilerParams(
            dimension_semantics=("arbitrary",), vmem_limit_bytes=VMEM_LIMIT_BYTES),
        name="merge_ln1_router",
    )(x2, y_na, y_dn, h_all, h_all, wpn, wpd, wo, g1, b1, wr, br)


def _moe_tiles(n):
    return (TOP_K * n) // MOE_TM + N_EXPERTS


def _plan_kernel(route_ref, cnt_ref, pos_ref, meta_ref):
    cnt_col = cnt_ref[...].T[:, 0:1]
    sub = lax.broadcasted_iota(jnp.int32, (ROUTER_LANES, 1), 0)
    is_e = (sub >= N_GROUPS) & (sub < N_GROUPS + N_EXPERTS)
    padded = jnp.where(is_e, jnp.ceil(cnt_col * (1.0 / MOE_TM)) * MOE_TM, 0.0)
    ri = lax.broadcasted_iota(jnp.int32, (ROUTER_LANES, ROUTER_LANES), 0)
    ci = lax.broadcasted_iota(jnp.int32, (ROUTER_LANES, ROUTER_LANES), 1)
    pb = jnp.broadcast_to(padded, (ROUTER_LANES, ROUTER_LANES))
    off = _dot_exact_lhs((ri > ci).astype(BF16), pb)[:, 0:1]
    end = off + padded

    rt = route_ref[...].T
    tm = rt.shape[1]
    subt = lax.broadcasted_iota(jnp.int32, (ROUTER_LANES, tm), 0).astype(F32) - N_GROUPS
    rows = []
    for k in range(TOP_K):
        e_row, r_row = rt[k:k + 1, :], rt[4 + k:5 + k, :]
        start = jnp.sum(jnp.where(subt == e_row, off, 0.0), 0, keepdims=True)
        rows.append(start + r_row)
    pos = jnp.concatenate(rows + [jnp.zeros((8 - TOP_K, tm), F32)], 0)
    pos_ref[...] = pos.astype(jnp.int32)

    tl = lax.broadcasted_iota(jnp.int32, (ROUTER_LANES, meta_ref.shape[1]), 1).astype(F32) * MOE_TM
    tile_e = jnp.sum(jnp.where(is_e & (end <= tl), 1.0, 0.0), 0, keepdims=True)
    ntile = jnp.max(end, 0, keepdims=True) * (1.0 / MOE_TM)
    row = lax.broadcasted_iota(jnp.int32, meta_ref.shape, 0)
    meta = jnp.where(row == 0, jnp.minimum(tile_e, N_EXPERTS - 1.0), jnp.where(row == 1, ntile, 0.0))
    meta_ref[...] = meta.astype(jnp.int32)


def _plan_call(route, counts, tm=512):
    n = route.shape[0]
    mt = -(-_moe_tiles(n) // LANES) * LANES
    return pl.pallas_call(
        _plan_kernel,
        grid=(n // tm,),
        in_specs=[pl.BlockSpec((tm, ROUTER_LANES), lambda i: (i, 0)),
                  pl.BlockSpec((8, ROUTER_LANES), lambda i: (0, 0))],
        out_specs=[pl.BlockSpec((8, tm), lambda i: (0, i)),
                   pl.BlockSpec((8, mt), lambda i: (0, 0))],
        out_shape=[jax.ShapeDtypeStruct((8, n), jnp.int32),
                   jax.ShapeDtypeStruct((8, mt), jnp.int32)],
        compiler_params=pltpu.CompilerParams(dimension_semantics=("arbitrary",)),
        name="moe_plan",
    )(route, counts)


def _row_copy(src, dst, src_row, dst_row, sem):
    return pltpu.make_async_copy(src.at[src_row], dst.at[dst_row], sem)


def _dispatch_kernel(pos_ref, x1t_ref, xs_in_hbm, xs_hbm, sem, *, n, tb):
    del xs_in_hbm
    base = pl.program_id(0) * tb

    def issue(t, c):
        for k in range(TOP_K):
            _row_copy(x1t_ref, xs_hbm, t, pos_ref[k * n + base + t], sem).start()
        return c

    lax.fori_loop(0, tb, issue, 0, unroll=8)

    def drain(t, c):
        for k in range(TOP_K):
            _row_copy(x1t_ref, xs_hbm, 0, 0, sem).wait()
        return c

    lax.fori_loop(0, tb, drain, 0, unroll=8)


def _dispatch_call(pos_flat, x1t, tb=512):
    n = x1t.shape[0]
    rows = _moe_tiles(n) * MOE_TM
    xs0 = jnp.zeros((rows, TOK_SUB, LANES), F32)
    return pl.pallas_call(
        functools.partial(_dispatch_kernel, n=n, tb=tb),
        grid_spec=pltpu.PrefetchScalarGridSpec(
            num_scalar_prefetch=1,
            grid=(n // tb,),
            in_specs=[pl.BlockSpec((tb, TOK_SUB, LANES), lambda i, p: (i, 0, 0)),
                      pl.BlockSpec(memory_space=pl.ANY)],
            out_specs=pl.BlockSpec(memory_space=pl.ANY),
            scratch_shapes=[pltpu.SemaphoreType.DMA],
        ),
        out_shape=jax.ShapeDtypeStruct((rows, TOK_SUB, LANES), F32),
        input_output_aliases={2: 0},
        compiler_params=pltpu.CompilerParams(dimension_semantics=("arbitrary",), has_side_effects=True),
        name="moe_dispatch",
    )(pos_flat, x1t, xs0)


def _experts_kernel(te_ref, nt_ref, xs_ref, wgu_ref, wd_ref, ys_ref):
    @pl.when(pl.program_id(0) < nt_ref[0])
    def _():
        x = jnp.concatenate([xs_ref[:, kk, :] for kk in range(TOK_SUB)], -1).astype(BF16)
        hgu = _dot(x, wgu_ref[0].astype(BF16))
        hid = _silu(hgu[:, :D_EXPERT]) * hgu[:, D_EXPERT:]
        y = _dot(hid.astype(BF16), wd_ref[0].astype(BF16))
        for kk in range(TOK_SUB):
            ys_ref[:, kk, :] = y[:, kk * LANES:(kk + 1) * LANES]


def _experts_call(tile_e, ntile, xs, wgu, wd):
    nt_max = xs.shape[0] // MOE_TM

    def tile(j, te, nt):
        return (jnp.minimum(j, nt[0] - 1), 0, 0)

    def expert(j, te, nt):
        return (te[jnp.minimum(j, nt[0] - 1)], 0, 0)

    return pl.pallas_call(
        _experts_kernel,
        grid_spec=pltpu.PrefetchScalarGridSpec(
            num_scalar_prefetch=2,
            grid=(nt_max,),
            in_specs=[pl.BlockSpec((MOE_TM, TOK_SUB, LANES), tile),
                      pl.BlockSpec((1, D_MODEL, 2 * D_EXPERT), expert),
                      pl.BlockSpec((1, D_EXPERT, D_MODEL), expert)],
            out_specs=pl.BlockSpec((MOE_TM, TOK_SUB, LANES), tile),
        ),
        out_shape=jax.ShapeDtypeStruct(xs.shape, F32),
        input_output_aliases={2: 0},
        compiler_params=pltpu.CompilerParams(
            dimension_semantics=("arbitrary",), vmem_limit_bytes=VMEM_LIMIT_BYTES),
        name="moe_experts",
    )(tile_e, ntile, xs, wgu, wd)


def _combine_kernel(pos_ref, x1t_ref, route_ref, g2_ref, b2_ref, ys_hbm, o_ref, gbuf, sem, *, n, tc):
    base = pl.program_id(0) * tc

    def issue(t, c):
        for k in range(TOP_K):
            _row_copy(ys_hbm, gbuf.at[k], pos_ref[k * n + base + t], t, sem).start()
        return c

    lax.fori_loop(0, tc, issue, 0, unroll=8)

    def drain(t, c):
        for k in range(TOP_K):
            _row_copy(ys_hbm, gbuf.at[k], 0, 0, sem).wait()
        return c

    lax.fori_loop(0, tc, drain, 0, unroll=8)

    route = route_ref[...]
    p1 = jnp.broadcast_to(route[:, 2:3], (tc, LANES))[:, None, :]
    p2 = jnp.broadcast_to(route[:, 3:4], (tc, LANES))[:, None, :]
    r = DEEPNORM_ALPHA * x1t_ref[...] + (p1 * gbuf[0] + p2 * gbuf[1])
    inv_d = 1.0 / D_MODEL
    mu = jnp.sum(jnp.sum(r, 2, keepdims=True), 1, keepdims=True) * inv_d
    d = r - mu
    var = jnp.sum(jnp.sum(d * d, 2, keepdims=True), 1, keepdims=True) * inv_d
    o_ref[...] = d * lax.rsqrt(var + LN_EPS) * g2_ref[...] + b2_ref[...]


def _combine_call(pos_flat, x1t, route, ys, g2, b2, tc=512):
    n = x1t.shape[0]
    return pl.pallas_call(
        functools.partial(_combine_kernel, n=n, tc=tc),
        grid_spec=pltpu.PrefetchScalarGridSpec(
            num_scalar_prefetch=1,
            grid=(n // tc,),
            in_specs=[pl.BlockSpec((tc, TOK_SUB, LANES), lambda i, p: (i, 0, 0)),
                      pl.BlockSpec((tc, ROUTER_LANES), lambda i, p: (i, 0)),
                      pl.BlockSpec((1, TOK_SUB, LANES), lambda i, p: (0, 0, 0)),
                      pl.BlockSpec((1, TOK_SUB, LANES), lambda i, p: (0, 0, 0)),
                      pl.BlockSpec(memory_space=pl.ANY)],
            out_specs=pl.BlockSpec((tc, TOK_SUB, LANES), lambda i, p: (i, 0, 0)),
            scratch_shapes=[pltpu.VMEM((TOP_K, tc, TOK_SUB, LANES), F32), pltpu.SemaphoreType.DMA],
        ),
        out_shape=jax.ShapeDtypeStruct((n, TOK_SUB, LANES), F32),
        compiler_params=pltpu.CompilerParams(
            dimension_semantics=("arbitrary",), vmem_limit_bytes=VMEM_LIMIT_BYTES),
        name="moe_combine_ln2",
    )(pos_flat, x1t, route, g2, b2, ys)


def _layer(x2, batch, seq, w_in, na_rpb, dn_conv_w, a_log_f, a_log_b, dt_bias_f, dt_bias_b, dn_norm_w,
           w_proj_na, w_proj_dn, w_out, ln1_g, ln1_b, w_rg, b_rg, w_re, b_re, w_gu, w_dn, ln2_g, ln2_b):
    n_act = 3 * NA_WIDTH + 4 * DN_WIDTH
    n_small = 4 * DN_HEADS
    w_main = jnp.concatenate([w_in[:, n_act + n_small:], w_in[:, :n_act]], 1).astype(BF16)
    ws = w_in[:, n_act:n_act + n_small].reshape(D_MODEL, 4, DN_HEADS // DN_HG, DN_HG)
    w_g = ws.transpose(2, 0, 1, 3).reshape(DN_HEADS // DN_HG, D_MODEL, 4 * DN_HG).astype(BF16)
    w_gt = w_g.transpose(0, 2, 1)

    def per_group(f, b):
        z = jnp.zeros((DN_HEADS // DN_HG, 2 * DN_HG), F32)
        return jnp.concatenate([z, f.reshape(-1, DN_HG), b.reshape(-1, DN_HG)], 1)

    pa, pd = per_group(a_log_f, a_log_b), per_group(dt_bias_f, dt_bias_b)
    prow = jnp.stack([pa, pd], 1)
    pcol = jnp.stack([pa, pd], 2)

    h_all, g_nat, g_t = _proj_call(x2, w_main, w_g, w_gt)
    bias = _na_bias_tables(na_rpb, seq // GRID_W)
    y_na = _na_call(h_all, bias, batch, seq)
    y_dn = _dn_call(h_all, g_nat, g_t, dn_conv_w, prow, pcol, dn_norm_w.reshape(1, DN_HEAD_DIM), batch, seq)

    wr = jnp.zeros((D_MODEL, ROUTER_LANES), F32)
    wr = wr.at[:, :N_GROUPS].set(w_rg).at[:, N_GROUPS:N_GROUPS + N_EXPERTS].set(w_re)
    br = jnp.zeros((1, ROUTER_LANES), F32)
    br = br.at[0, :N_GROUPS].set(b_rg).at[0, N_GROUPS:N_GROUPS + N_EXPERTS].set(b_re)
    x1t, route, counts = _merge_call(x2, y_na, y_dn, h_all, w_proj_na.astype(BF16), w_proj_dn.astype(BF16),
                                     w_out.astype(BF16), ln1_g.reshape(1, -1), ln1_b.reshape(1, -1), wr, br)
    pos, meta = _plan_call(route, counts)
    pos_flat = pos[:TOP_K].reshape(-1)
    xs = _dispatch_call(pos_flat, x1t)
    ys = _experts_call(meta[0, :_moe_tiles(x2.shape[0])], meta[1, :1], xs, w_gu, w_dn)
    out = _combine_call(pos_flat, x1t, route, ys, ln2_g.reshape(1, TOK_SUB, LANES), ln2_b.reshape(1, TOK_SUB, LANES))
    return out.reshape(x2.shape)


def kernel(x, w_in, na_rpb, dn_conv_w, dn_a_log_f, dn_a_log_b, dn_dt_bias_f, dn_dt_bias_b, dn_norm_w, w_proj_na, w_proj_dn, w_out, ln1_g, ln1_b, w_router_group, b_router_group, w_router_expert, b_router_expert, w_expert_gate_up, w_expert_down, ln2_g, ln2_b):
    batch, seq, d = x.shape
    x2 = x.reshape(batch * seq, d)
    for l in range(w_in.shape[0]):
        x2 = _layer(x2, batch, seq, w_in[l], na_rpb[l], dn_conv_w[l], dn_a_log_f[l], dn_a_log_b[l],
                    dn_dt_bias_f[l], dn_dt_bias_b[l], dn_norm_w[l], w_proj_na[l], w_proj_dn[l], w_out[l],
                    ln1_g[l], ln1_b[l], w_router_group[l], b_router_group[l], w_router_expert[l],
                    b_router_expert[l], w_expert_gate_up[l], w_expert_down[l], ln2_g[l], ln2_b[l])
    return x2.reshape(batch, seq, d)
```

```python
import functools

import numpy as np
import jax
import jax.numpy as jnp
from jax import lax
from jax.experimental import pallas as pl
from jax.experimental.pallas import tpu as pltpu

F32 = jnp.float32
BF16 = jnp.bfloat16

D_MODEL = 1024
GRID_W = 64
NA_HEADS = 8
NA_HEAD_DIM = 64
NA_WIN_ROWS = 8
NA_WIN_COLS = 16
NA_WIDTH = NA_HEADS * NA_HEAD_DIM
DN_HEADS = 8
DN_HEAD_DIM = 64
DN_WIDTH = DN_HEADS * DN_HEAD_DIM
DN_CONV = 5
CHUNK = 64
N_GROUPS = 4
EXPERTS_PER_GROUP = 8
N_EXPERTS = N_GROUPS * EXPERTS_PER_GROUP
D_EXPERT = 256
TOP_K = 2
DEPTH = 1
DEEPNORM_ALPHA = (2.0 * DEPTH) ** 0.25
LN_EPS = 1e-5
RMS_EPS = 1e-6

LANES = 128
VMEM_LIMIT_BYTES = 56 * 1024 * 1024

COL_GNA, COL_GDN = 0, 1024
COL_QNA, COL_KNA, COL_VNA = 2048, 2560, 3072
COL_QDN, COL_KDN, COL_VDN, COL_ZDN = 3584, 4096, 4608, 5120
H_COLS = 5632
PROJ_CHUNK = 512

NA_QROWS = 4
NA_KROWS = 12
NA_TQ = NA_QROWS * GRID_W
NA_TK = NA_KROWS * GRID_W

DN_HG = 4
DN_GW = DN_HG * DN_HEAD_DIM
DN_PREP_UNROLL = 2
ROUTER_LANES = 128
NEG_BIG = -1e30
TOK_SUB = D_MODEL // LANES
SUB = 8
MOE_TM = 256


def _sigmoid(x):
    return 1.0 / (1.0 + jnp.exp(-x))


def _silu(x):
    return x * _sigmoid(x)


def _softplus(x):
    return jnp.maximum(x, 0.0) + jnp.log(1.0 + jnp.exp(-jnp.abs(x)))


def _split3(x):
    x1 = x.astype(BF16)
    r1 = x - x1.astype(F32)
    x2 = r1.astype(BF16)
    r2 = r1 - x2.astype(F32)
    return x1, x2, r2.astype(BF16)


def _dot(a, b):
    return jnp.dot(a, b, preferred_element_type=F32)


def _dot_nt(a, b):
    return lax.dot_general(a, b, (((1,), (1,)), ((), ())), preferred_element_type=F32)


def _dot_exact_lhs(a_bf16_exact, x):
    x1, x2, x3 = _split3(x)
    return _dot(a_bf16_exact, x1) + _dot(a_bf16_exact, x2) + _dot(a_bf16_exact, x3)


def _dot_exact_rhs(x, a_bf16_exact):
    x1, x2, x3 = _split3(x)
    return _dot(x1, a_bf16_exact) + _dot(x2, a_bf16_exact) + _dot(x3, a_bf16_exact)


def _dot_f32(a, b):
    a1, a2, a3 = _split3(a)
    b1, b2, b3 = _split3(b)
    return (_dot(a1, b1) + (_dot(a1, b2) + _dot(a2, b1))
            + (_dot(a1, b3) + _dot(a3, b1) + _dot(a2, b2)))


def _layer_norm(r, g, b):
    mu = jnp.mean(r, -1, keepdims=True)
    d = r - mu
    var = jnp.mean(d * d, -1, keepdims=True)
    return d * lax.rsqrt(var + LN_EPS) * g + b


def _to_tiles(x):
    x3 = x.reshape(x.shape[0] // SUB, SUB, D_MODEL)
    return [x3[:, :, kk * LANES:(kk + 1) * LANES] for kk in range(TOK_SUB)]


def _from_tiles(ref, lead=()):
    slabs = [ref[lead + (slice(None), kk)] for kk in range(TOK_SUB)]
    x3 = jnp.concatenate(slabs, -1)
    return x3.reshape(x3.shape[0] * SUB, D_MODEL)


def _row_of(ref, row, lead=()):
    idx = lead + (lax.shift_right_logical(row, 3), slice(None), jnp.bitwise_and(row, SUB - 1), slice(None))
    return ref.at[idx]


def _proj_kernel(x_ref, w_ref, wg_ref, wgt_ref, h_ref, g_ref, gt_ref):
    xb = x_ref[...].astype(BF16)
    for c in range(H_COLS // PROJ_CHUNK):
        cs = slice(c * PROJ_CHUNK, (c + 1) * PROJ_CHUNK)
        h_ref[:, cs] = _dot(xb, w_ref[:, cs]).astype(BF16)
    for hg in range(DN_HEADS // DN_HG):
        g_ref[hg] = _dot(xb, wg_ref[hg])
        gt_ref[hg] = _dot_nt(wgt_ref[hg], xb)


def _proj_call(x2, w_main, w_g, w_gt, tm=512):
    n = x2.shape[0]
    ng = DN_HEADS // DN_HG
    gw = 4 * DN_HG
    return pl.pallas_call(
        _proj_kernel,
        grid=(n // tm,),
        in_specs=[
            pl.BlockSpec((tm, D_MODEL), lambda i: (i, 0)),
            pl.BlockSpec((D_MODEL, H_COLS), lambda i: (0, 0)),
            pl.BlockSpec((ng, D_MODEL, gw), lambda i: (0, 0, 0)),
            pl.BlockSpec((ng, gw, D_MODEL), lambda i: (0, 0, 0)),
        ],
        out_specs=[
            pl.BlockSpec((tm, H_COLS), lambda i: (i, 0)),
            pl.BlockSpec((ng, tm, gw), lambda i: (0, i, 0)),
            pl.BlockSpec((ng, gw, tm), lambda i: (0, 0, i)),
        ],
        out_shape=[
            jax.ShapeDtypeStruct((n, H_COLS), BF16),
            jax.ShapeDtypeStruct((ng, n, gw), F32),
            jax.ShapeDtypeStruct((ng, gw, n), F32),
        ],
        compiler_params=pltpu.CompilerParams(
            dimension_semantics=("arbitrary",), vmem_limit_bytes=VMEM_LIMIT_BYTES),
        name="proj",
    )(x2, w_main, w_g, w_gt)


def _na_bias_tables(rpb, rows):
    kr_win = min(NA_WIN_ROWS, rows)
    c = np.arange(GRID_W)
    dc = np.clip(c[None, :] - c[:, None], -(NA_WIN_COLS - 1), NA_WIN_COLS - 1) + (NA_WIN_COLS - 1)
    col_hot = (dc[None] == np.arange(2 * NA_WIN_COLS - 1)[:, None, None]).astype(np.float32)
    col_start = np.clip(c - NA_WIN_COLS // 2, 0, GRID_W - NA_WIN_COLS)
    col_ok = (c[None, :] >= col_start[:, None]) & (c[None, :] < col_start[:, None] + NA_WIN_COLS)
    row_hot, valid = [], []
    for r0 in (0, 2 * NA_QROWS, rows - NA_QROWS):
        kw0 = int(np.clip(r0 - NA_WIN_ROWS // 2, 0, rows - NA_KROWS))
        qr = r0 + np.arange(NA_QROWS)
        kr = kw0 + np.arange(NA_KROWS)
        row_start = np.clip(qr - kr_win // 2, 0, rows - kr_win)
        row_ok = (kr[None, :] >= row_start[:, None]) & (kr[None, :] < row_start[:, None] + kr_win)
        dr = kr[None, :] - qr[:, None] + (NA_WIN_ROWS - 1)
        row_hot.append(((dr[:, :, None] == np.arange(2 * NA_WIN_ROWS - 1)) & row_ok[:, :, None]).astype(np.float32))
        valid.append(row_ok[:, None, :, None] & col_ok[None, :, None, :])
    row_hot = np.stack(row_hot)
    valid = np.stack(valid).reshape(3, 1, NA_TQ, NA_TK)
    tab = jnp.einsum('vijr,hrd,dqk->vhiqjk', row_hot, rpb, col_hot, precision=lax.Precision.HIGHEST)
    return jnp.where(valid, tab.reshape(3, rpb.shape[0], NA_TQ, NA_TK), NEG_BIG).astype(F32)


def _na_kernel(q_ref, k_ref, v_ref, bias_ref, o_ref, *, rows):
    qt = pl.program_id(2)
    kw0 = jnp.clip(qt * NA_QROWS - NA_WIN_ROWS // 2, 0, rows - NA_KROWS) * GRID_W
    kw0 = pl.multiple_of(kw0, GRID_W)
    scale = NA_HEAD_DIM ** -0.5
    for hh in range(LANES // NA_HEAD_DIM):
        ls = slice(hh * NA_HEAD_DIM, (hh + 1) * NA_HEAD_DIM)
        q = q_ref[:, ls]
        k = k_ref[pl.ds(kw0, NA_TK), ls]
        v = v_ref[pl.ds(kw0, NA_TK), ls]
        s = _dot_nt(q, k) * scale + bias_ref[0, hh]
        m = jnp.max(s, -1, keepdims=True)
        p = jnp.exp(s - m)
        l = jnp.sum(p, -1, keepdims=True)
        o = _dot(p.astype(BF16), v) / l
        o_ref[:, ls] = o.astype(BF16)


def _na_call(h_all, bias, batch, seq):
    rows = seq // GRID_W
    nqt = rows // NA_QROWS
    hp = NA_WIDTH // LANES
    n = batch * seq

    def bias_idx(b, p, t):
        return (jnp.where(t == 0, 0, jnp.where(t == nqt - 1, 2, 1)), p, 0, 0)

    return pl.pallas_call(
        functools.partial(_na_kernel, rows=rows),
        grid=(batch, hp, nqt),
        in_specs=[
            pl.BlockSpec((NA_TQ, LANES), lambda b, p, t: (b * nqt + t, COL_QNA // LANES + p)),
            pl.BlockSpec((seq, LANES), lambda b, p, t: (b, COL_KNA // LANES + p)),
            pl.BlockSpec((seq, LANES), lambda b, p, t: (b, COL_VNA // LANES + p)),
            pl.BlockSpec((1, LANES // NA_HEAD_DIM, NA_TQ, NA_TK), bias_idx),
        ],
        out_specs=pl.BlockSpec((NA_TQ, LANES), lambda b, p, t: (b * nqt + t, p)),
        out_shape=jax.ShapeDtypeStruct((n, NA_WIDTH), BF16),
        compiler_params=pltpu.CompilerParams(
            dimension_semantics=("arbitrary", "arbitrary", "arbitrary"),
            vmem_limit_bytes=VMEM_LIMIT_BYTES),
        name="natten",
    )(h_all, h_all, h_all, bias)


def _dn_kernel(q_ref, k_ref, v_ref, z_ref, g_ref, gt_ref, cwq_ref, cwk_ref, cwv_ref,
               prow_ref, pcol_ref, nw_ref, o_ref,
               qs, ks, vs, gn_s, gt_s, wq_s, ik_s, u_s, eg_s, st_s, *, seq):
    nchunk = seq // CHUNK
    hd = DN_HEAD_DIM
    rb = 256

    lane16 = lax.broadcasted_iota(jnp.int32, (1, 4 * DN_HG), 1)
    graw = g_ref[0]
    gl = -jnp.exp(prow_ref[0, 0:1, :]) * _softplus(graw + prow_ref[0, 1:2, :])
    gn_s[...] = jnp.where(lane16 < 2 * DN_HG, _sigmoid(graw), gl)
    sub16 = lax.broadcasted_iota(jnp.int32, (4 * DN_HG, 1), 0)
    grawt = gt_ref[0]
    glt = -jnp.exp(pcol_ref[0, :, 0:1]) * _softplus(grawt + pcol_ref[0, :, 1:2])
    gtt = jnp.where(sub16 < 2 * DN_HG, _sigmoid(grawt), glt)
    for n in range(nchunk):
        gt_s[n] = gtt[:, n * CHUNK:(n + 1) * CHUNK]

    pad = 16
    half = DN_CONV // 2
    for src, cw_ref, dst, norm, mul in ((q_ref, cwq_ref, qs, True, hd ** -0.5),
                                        (k_ref, cwk_ref, ks, True, 1.0),
                                        (v_ref, cwv_ref, vs, False, 1.0)):
        cw = cw_ref[...]
        for r0 in range(0, seq, rb):
            lo, hi = r0 - pad, r0 + rb + pad
            parts = []
            if lo < 0:
                parts.append(jnp.zeros((pad, DN_GW), F32))
            parts.append(src[max(lo, 0):min(hi, seq), :].astype(F32))
            if hi > seq:
                parts.append(jnp.zeros((pad, DN_GW), F32))
            xin = jnp.concatenate(parts, 0) if len(parts) > 1 else parts[0]
            nrow = rb + 2 * pad
            y = jnp.zeros((rb, DN_GW), F32)
            for i in range(DN_CONV):
                sh = (half - i) % nrow
                xs = xin if sh == 0 else pltpu.roll(xin, sh, 0)
                y = y + xs[pad:pad + rb, :] * cw[i:i + 1, :]
            y = _silu(y)
            for hl in range(DN_HG):
                seg = y[:, hl * hd:(hl + 1) * hd]
                if norm:
                    seg = seg * (lax.rsqrt(jnp.sum(seg * seg, -1, keepdims=True) + RMS_EPS) * mul)
                dst[hl, r0:r0 + rb, :] = seg.astype(BF16)

    ri = lax.broadcasted_iota(jnp.int32, (CHUNK, CHUNK), 0)
    ci = lax.broadcasted_iota(jnp.int32, (CHUNK, CHUNK), 1)
    eye = (ri == ci).astype(F32)
    tril = (ri >= ci).astype(BF16)
    triu = (ri <= ci).astype(BF16)
    lower_incl, lower_strict = ri >= ci, ri > ci
    upper_incl, upper_strict = ri <= ci, ri < ci
    level_masks = []
    s = 1
    while s < CHUNK:
        sh = s.bit_length() - 1
        level_masks.append((((ri >> (sh + 1)) == (ci >> (sh + 1))) & ((ri >> sh) != (ci >> sh))).astype(F32))
        s *= 2

    dirs = ((lower_incl, lower_strict, CHUNK - 1), (upper_incl, upper_strict, 0))

    def prep(it, carry):
        probs = []
        for c in range(DN_PREP_UNROLL):
            n = it * DN_PREP_UNROLL + c
            r0 = pl.multiple_of(n * CHUNK, CHUNK)
            gcol = gn_s[pl.ds(r0, CHUNK), :]
            grow = gt_s[n]
            cs_col = (_dot_exact_lhs(tril, gcol), _dot_exact_lhs(triu, gcol))
            cs_row = (_dot_exact_rhs(grow, triu), _dot_exact_rhs(grow, tril))
            for hl in range(DN_HG):
                q = qs[hl, pl.ds(r0, CHUNK), :]
                k = ks[hl, pl.ds(r0, CHUNK), :]
                v = vs[hl, pl.ds(r0, CHUNK), :]
                pr = _dot_nt(jnp.concatenate([q, k], 0), k)
                for d in range(2):
                    col = 2 * DN_HG + d * DN_HG + hl
                    probs.append(dict(n=n, r0=r0, hl=hl, d=d, q=q, k=k, v=v, pr=pr,
                                      bcol=gcol[:, d * DN_HG + hl:d * DN_HG + hl + 1],
                                      ccol=cs_col[d][:, col:col + 1], crow=cs_row[d][col:col + 1, :]))
        for p in probs:
            incl, strict, last = dirs[p["d"]]
            kf, vf, qf = p["k"].astype(F32), p["v"].astype(F32), p["q"].astype(F32)
            qk, kk = p["pr"][:CHUNK], p["pr"][CHUNK:]
            bcol, ccol = p["bcol"], p["ccol"]
            dec = jnp.exp(jnp.where(incl, ccol - p["crow"], -jnp.inf))
            p["nmat"] = jnp.where(strict, bcol * kk * dec, 0.0)
            p["intra"] = (qk * dec).astype(BF16)
            e = jnp.exp(ccol)
            p["rhs"] = jnp.concatenate([vf * bcol, kf * (bcol * e)], 1).astype(BF16)
            glast = ccol[last:last + 1, :]
            p["qdec"] = (qf * e).astype(BF16)
            p["kdect"] = (kf * jnp.exp(glast - ccol)).T.astype(BF16)
            p["eg"] = jnp.broadcast_to(jnp.exp(glast), (8, LANES))
            p["t"] = eye - p["nmat"] * level_masks[0]
        for m in level_masks[1:]:
            xs = [_dot((p["nmat"] * m).astype(BF16), p["t"].astype(BF16)) for p in probs]
            ys = [_dot(p["t"].astype(BF16), x.astype(BF16)) for p, x in zip(probs, xs)]
            for p, y in zip(probs, ys):
                p["t"] = p["t"] - y
        sols = [_dot(p["t"].astype(BF16), p["rhs"]) for p in probs]
        for p, sol in zip(probs, sols):
            d, hl, n = p["d"], p["hl"], p["n"]
            u_s[d, hl, pl.ds(p["r0"], CHUNK), :] = sol[:, :hd]
            wq_s[d, hl, n, 0:CHUNK, :] = sol[:, hd:].astype(BF16)
            wq_s[d, hl, n, CHUNK:2 * CHUNK, :] = p["qdec"]
            ik_s[d, hl, n, 0:CHUNK, :] = p["intra"]
            ik_s[d, hl, n, CHUNK:2 * CHUNK, :] = p["kdect"]
            eg_s[d, hl, n] = p["eg"]
        return carry

    lax.fori_loop(0, nchunk // DN_PREP_UNROLL, prep, 0)

    st_s[...] = jnp.zeros_like(st_s)

    def scan(i, carry):
        probs = []
        for d in range(2):
            n = i if d == 0 else nchunk - 1 - i
            r0 = pl.multiple_of(n * CHUNK, CHUNK)
            for hl in range(DN_HG):
                probs.append((d, hl, n, r0))
        sts = [st_s[d, hl] for d, hl, n, r0 in probs]
        ts = [_dot(wq_s[d, hl, n], st.astype(BF16))
              for (d, hl, n, r0), st in zip(probs, sts)]
        vnews = [u_s[d, hl, pl.ds(r0, CHUNK), :] - t[:CHUNK] for (d, hl, n, r0), t in zip(probs, ts)]
        t2s = [_dot(ik_s[d, hl, n], vn.astype(BF16))
               for (d, hl, n, r0), vn in zip(probs, vnews)]
        for (d, hl, n, r0), st, t, t2 in zip(probs, sts, ts, t2s):
            u_s[d, hl, pl.ds(r0, CHUNK), :] = t[CHUNK:] + t2[:CHUNK]
            st_s[d, hl] = st * eg_s[d, hl, n][0:1, 0:hd] + t2[CHUNK:]
        return carry

    lax.fori_loop(0, nchunk, scan, 0)

    nw = nw_ref[...]
    for r0 in range(0, seq, rb):
        for hl in range(DN_HG):
            o = u_s[0, hl, r0:r0 + rb, :] + u_s[1, hl, r0:r0 + rb, :]
            o = o * lax.rsqrt(jnp.mean(o * o, -1, keepdims=True) + RMS_EPS) * nw
            zz = z_ref[r0:r0 + rb, hl * hd:(hl + 1) * hd].astype(F32)
            o_ref[r0:r0 + rb, hl * hd:(hl + 1) * hd] = (o * _silu(zz)).astype(BF16)


def _dn_call(h_all, g_nat, g_t, cw, prow, pcol, norm_w, batch, seq):
    n = batch * seq
    ng = DN_HEADS // DN_HG
    gw = 4 * DN_HG
    nchunk = seq // CHUNK
    cb = DN_GW // LANES

    def col(c0):
        return lambda b, g: (b, c0 // DN_GW + g)

    return pl.pallas_call(
        functools.partial(_dn_kernel, seq=seq),
        grid=(batch, ng),
        in_specs=[
            pl.BlockSpec((seq, DN_GW), col(COL_QDN)),
            pl.BlockSpec((seq, DN_GW), col(COL_KDN)),
            pl.BlockSpec((seq, DN_GW), col(COL_VDN)),
            pl.BlockSpec((seq, DN_GW), col(COL_ZDN)),
            pl.BlockSpec((1, seq, gw), lambda b, g: (g, b, 0)),
            pl.BlockSpec((1, gw, seq), lambda b, g: (g, 0, b)),
            pl.BlockSpec((DN_CONV, DN_GW), lambda b, g: (0, g)),
            pl.BlockSpec((DN_CONV, DN_GW), lambda b, g: (0, DN_WIDTH // DN_GW + g)),
            pl.BlockSpec((DN_CONV, DN_GW), lambda b, g: (0, 2 * DN_WIDTH // DN_GW + g)),
            pl.BlockSpec((1, 2, gw), lambda b, g: (g, 0, 0)),
            pl.BlockSpec((1, gw, 2), lambda b, g: (g, 0, 0)),
            pl.BlockSpec((1, DN_HEAD_DIM), lambda b, g: (0, 0)),
        ],
        out_specs=pl.BlockSpec((seq, DN_GW), lambda b, g: (b, g)),
        out_shape=jax.ShapeDtypeStruct((n, DN_WIDTH), BF16),
        scratch_shapes=[
            pltpu.VMEM((DN_HG, seq, DN_HEAD_DIM), BF16),
            pltpu.VMEM((DN_HG, seq, DN_HEAD_DIM), BF16),
            pltpu.VMEM((DN_HG, seq, DN_HEAD_DIM), BF16),
            pltpu.VMEM((seq, gw), F32),
            pltpu.VMEM((nchunk, gw, CHUNK), F32),
            pltpu.VMEM((2, DN_HG, nchunk, 2 * CHUNK, DN_HEAD_DIM), BF16),
            pltpu.VMEM((2, DN_HG, nchunk, 2 * CHUNK, DN_HEAD_DIM), BF16),
            pltpu.VMEM((2, DN_HG, seq, DN_HEAD_DIM), F32),
            pltpu.VMEM((2, DN_HG, nchunk, 8, LANES), F32),
            pltpu.VMEM((2, DN_HG, CHUNK, DN_HEAD_DIM), F32),
        ],
        compiler_params=pltpu.CompilerParams(
            dimension_semantics=("arbitrary", "arbitrary"), vmem_limit_bytes=VMEM_LIMIT_BYTES),
        name="deltanet",
    )(h_all, h_all, h_all, h_all, g_nat, g_t, cw, cw, cw, prow, pcol, norm_w)


def _merge_kernel(x_ref, yna_ref, ydn_ref, gna_ref, gdn_ref, wpn_ref, wpd_ref, wo_ref,
                  g1_ref, b1_ref, wr_ref, br_ref, x1t_ref, route_ref, cnt_ref, carry_ref):
    a = _dot(yna_ref[...], wpn_ref[...])
    b = _dot(ydn_ref[...], wpd_ref[...])
    merged = _sigmoid(gna_ref[...].astype(F32)) * a + _sigmoid(gdn_ref[...].astype(F32)) * b
    mix = _dot(merged.astype(BF16), wo_ref[...])
    x1 = _layer_norm(DEEPNORM_ALPHA * x_ref[...] + mix, g1_ref[...], b1_ref[...])
    for kk, slab in enumerate(_to_tiles(x1)):
        x1t_ref[:, kk] = slab

    logits = _dot_f32(x1, wr_ref[...]) + br_ref[...]
    tm = logits.shape[0]
    lane = lax.broadcasted_iota(jnp.int32, logits.shape, 1)
    lanef = lane.astype(F32)
    big = float(ROUTER_LANES)
    gmask = lane < N_GROUPS
    gl = jnp.where(gmask, logits, -jnp.inf)
    gmax = jnp.max(gl, -1, keepdims=True)
    gidx = jnp.min(jnp.where(gl == gmax, lanef, big), -1, keepdims=True)
    pg = 1.0 / jnp.sum(jnp.where(gmask, jnp.exp(gl - gmax), 0.0), -1, keepdims=True)
    egrp = jnp.floor((lanef - N_GROUPS) * (1.0 / EXPERTS_PER_GROUP))
    emask = (lane >= N_GROUPS) & (lane < N_GROUPS + N_EXPERTS) & (egrp == gidx)
    el = jnp.where(emask, logits, -jnp.inf)
    m1 = jnp.max(el, -1, keepdims=True)
    i1 = jnp.min(jnp.where(el == m1, lanef, big), -1, keepdims=True)
    el2 = jnp.where(lanef == i1, -jnp.inf, el)
    m2 = jnp.max(el2, -1, keepdims=True)
    i2 = jnp.min(jnp.where(el2 == m2, lanef, big), -1, keepdims=True)
    t = jnp.exp(m2 - m1)
    p1 = pg / (1.0 + t)
    p2 = pg * t / (1.0 + t)

    @pl.when(pl.program_id(0) == 0)
    def _():
        carry_ref[...] = jnp.zeros_like(carry_ref)

    oh1 = (lanef == i1).astype(F32)
    oh2 = (lanef == i2).astype(F32)
    oh = oh1 + oh2
    ri = lax.broadcasted_iota(jnp.int32, (tm, tm), 0)
    ci = lax.broadcasted_iota(jnp.int32, (tm, tm), 1)
    before = _dot((ri > ci).astype(BF16), oh.astype(BF16)) + carry_ref[...]
    rank1 = jnp.sum(before * oh1, -1, keepdims=True)
    rank2 = jnp.sum(before * oh2, -1, keepdims=True)
    carry = carry_ref[...] + jnp.sum(oh, 0, keepdims=True)
    carry_ref[...] = carry
    cnt_ref[...] = jnp.broadcast_to(carry, cnt_ref.shape)
    route = jnp.zeros_like(logits)
    for j, col in enumerate((i1 - N_GROUPS, i2 - N_GROUPS, p1, p2, rank1, rank2)):
        route = jnp.where(lane == j, col, route)
    route_ref[...] = route


def _merge_call(x2, y_na, y_dn, h_all, wpn, wpd, wo, g1, b1, wr, br, tm=512):
    n = x2.shape[0]
    const = lambda i: (0, 0)
    return pl.pallas_call(
        _merge_kernel,
        grid=(n // tm,),
        in_specs=[
            pl.BlockSpec((tm, D_MODEL), lambda i: (i, 0)),
            pl.BlockSpec((tm, NA_WIDTH), lambda i: (i, 0)),
            pl.BlockSpec((tm, DN_WIDTH), lambda i: (i, 0)),
            pl.BlockSpec((tm, D_MODEL), lambda i: (i, COL_GNA // D_MODEL)),
            pl.BlockSpec((tm, D_MODEL), lambda i: (i, COL_GDN // D_MODEL)),
            pl.BlockSpec((NA_WIDTH, D_MODEL), const),
            pl.BlockSpec((DN_WIDTH, D_MODEL), const),
            pl.BlockSpec((D_MODEL, D_MODEL), const),
            pl.BlockSpec((1, D_MODEL), const),
            pl.BlockSpec((1, D_MODEL), const),
            pl.BlockSpec((D_MODEL, ROUTER_LANES), const),
            pl.BlockSpec((1, ROUTER_LANES), const),
        ],
        out_specs=[
            pl.BlockSpec((tm // SUB, TOK_SUB, SUB, LANES), lambda i: (i, 0, 0, 0)),
            pl.BlockSpec((tm, ROUTER_LANES), lambda i: (i, 0)),
            pl.BlockSpec((8, ROUTER_LANES), const),
        ],
        out_shape=[
            jax.ShapeDtypeStruct((n // SUB, TOK_SUB, SUB, LANES), F32),
            jax.ShapeDtypeStruct((n, ROUTER_LANES), F32),
            jax.ShapeDtypeStruct((8, ROUTER_LANES), F32),
        ],
        scratch_shapes=[pltpu.VMEM((1, ROUTER_LANES), F32)],
        compiler_params=pltpu.CompilerParams(
            dimension_semantics=("arbitrary",), vmem_limit_bytes=VMEM_LIMIT_BYTES),
        name="merge_ln1_router",
    )(x2, y_na, y_dn, h_all, h_all, wpn, wpd, wo, g1, b1, wr, br)


def _moe_tiles(n):
    return (TOP_K * n) // MOE_TM + N_EXPERTS


def _plan_kernel(route_ref, cnt_ref, pos_ref, meta_ref):
    cnt_col = cnt_ref[...].T[:, 0:1]
    sub = lax.broadcasted_iota(jnp.int32, (ROUTER_LANES, 1), 0)
    is_e = (sub >= N_GROUPS) & (sub < N_GROUPS + N_EXPERTS)
    padded = jnp.where(is_e, jnp.ceil(cnt_col * (1.0 / MOE_TM)) * MOE_TM, 0.0)
    ri = lax.broadcasted_iota(jnp.int32, (ROUTER_LANES, ROUTER_LANES), 0)
    ci = lax.broadcasted_iota(jnp.int32, (ROUTER_LANES, ROUTER_LANES), 1)
    pb = jnp.broadcast_to(padded, (ROUTER_LANES, ROUTER_LANES))
    off = _dot_exact_lhs((ri > ci).astype(BF16), pb)[:, 0:1]
    end = off + padded

    rt = route_ref[...].T
    tm = rt.shape[1]
    subt = lax.broadcasted_iota(jnp.int32, (ROUTER_LANES, tm), 0).astype(F32) - N_GROUPS
    rows = []
    for k in range(TOP_K):
        e_row, r_row = rt[k:k + 1, :], rt[4 + k:5 + k, :]
        start = jnp.sum(jnp.where(subt == e_row, off, 0.0), 0, keepdims=True)
        rows.append(start + r_row)
    pos = jnp.concatenate(rows + [jnp.zeros((8 - TOP_K, tm), F32)], 0)
    pos_ref[...] = pos.astype(jnp.int32)

    tl = lax.broadcasted_iota(jnp.int32, (ROUTER_LANES, meta_ref.shape[1]), 1).astype(F32) * MOE_TM
    tile_e = jnp.sum(jnp.where(is_e & (end <= tl), 1.0, 0.0), 0, keepdims=True)
    ntile = jnp.max(end, 0, keepdims=True) * (1.0 / MOE_TM)
    row = lax.broadcasted_iota(jnp.int32, meta_ref.shape, 0)
    meta = jnp.where(row == 0, jnp.minimum(tile_e, N_EXPERTS - 1.0), jnp.where(row == 1, ntile, 0.0))
    meta_ref[...] = meta.astype(jnp.int32)


def _plan_call(route, counts, tm=512):
    n = route.shape[0]
    mt = -(-_moe_tiles(n) // LANES) * LANES
    return pl.pallas_call(
        _plan_kernel,
        grid=(n // tm,),
        in_specs=[pl.BlockSpec((tm, ROUTER_LANES), lambda i: (i, 0)),
                  pl.BlockSpec((8, ROUTER_LANES), lambda i: (0, 0))],
        out_specs=[pl.BlockSpec((8, tm), lambda i: (0, i)),
                   pl.BlockSpec((8, mt), lambda i: (0, 0))],
        out_shape=[jax.ShapeDtypeStruct((8, n), jnp.int32),
                   jax.ShapeDtypeStruct((8, mt), jnp.int32)],
        compiler_params=pltpu.CompilerParams(dimension_semantics=("arbitrary",)),
        name="moe_plan",
    )(route, counts)


def _row_copy(src, dst, src_row, dst_row, sem, dst_lead=()):
    return pltpu.make_async_copy(_row_of(src, src_row), _row_of(dst, dst_row, dst_lead), sem)


def _dispatch_kernel(pos_ref, x1t_ref, xs_in_hbm, xs_hbm, sem, *, n, tb):
    del xs_in_hbm
    base = pl.program_id(0) * tb

    def issue(t, c):
        for k in range(TOP_K):
            _row_copy(x1t_ref, xs_hbm, t, pos_ref[k * n + base + t], sem).start()
        return c

    lax.fori_loop(0, tb, issue, 0, unroll=8)

    def drain(t, c):
        for k in range(TOP_K):
            _row_copy(x1t_ref, xs_hbm, 0, 0, sem).wait()
        return c

    lax.fori_loop(0, tb, drain, 0, unroll=8)


def _dispatch_call(pos_flat, x1t, tb=512):
    n = x1t.shape[0] * SUB
    rows = _moe_tiles(n) * MOE_TM
    xs0 = jnp.zeros((rows // SUB, TOK_SUB, SUB, LANES), F32)
    return pl.pallas_call(
        functools.partial(_dispatch_kernel, n=n, tb=tb),
        grid_spec=pltpu.PrefetchScalarGridSpec(
            num_scalar_prefetch=1,
            grid=(n // tb,),
            in_specs=[pl.BlockSpec((tb // SUB, TOK_SUB, SUB, LANES), lambda i, p: (i, 0, 0, 0)),
                      pl.BlockSpec(memory_space=pl.ANY)],
            out_specs=pl.BlockSpec(memory_space=pl.ANY),
            scratch_shapes=[pltpu.SemaphoreType.DMA],
        ),
        out_shape=jax.ShapeDtypeStruct(xs0.shape, F32),
        input_output_aliases={2: 0},
        compiler_params=pltpu.CompilerParams(dimension_semantics=("arbitrary",), has_side_effects=True),
        name="moe_dispatch",
    )(pos_flat, x1t, xs0)


def _experts_kernel(te_ref, nt_ref, xs_ref, wgu_ref, wd_ref, ys_ref):
    @pl.when(pl.program_id(0) < nt_ref[0])
    def _():
        hgu = _dot(_from_tiles(xs_ref).astype(BF16), wgu_ref[0].astype(BF16))
        hid = _silu(hgu[:, :D_EXPERT]) * hgu[:, D_EXPERT:]
        y = _dot(hid.astype(BF16), wd_ref[0].astype(BF16))
        for kk, slab in enumerate(_to_tiles(y)):
            ys_ref[:, kk] = slab


def _experts_call(tile_e, ntile, xs, wgu, wd):
    nt_max = xs.shape[0] * SUB // MOE_TM

    def tile(j, te, nt):
        return (jnp.minimum(j, nt[0] - 1), 0, 0, 0)

    def expert(j, te, nt):
        return (te[jnp.minimum(j, nt[0] - 1)], 0, 0)

    return pl.pallas_call(
        _experts_kernel,
        grid_spec=pltpu.PrefetchScalarGridSpec(
            num_scalar_prefetch=2,
            grid=(nt_max,),
            in_specs=[pl.BlockSpec((MOE_TM // SUB, TOK_SUB, SUB, LANES), tile),
                      pl.BlockSpec((1, D_MODEL, 2 * D_EXPERT), expert),
                      pl.BlockSpec((1, D_EXPERT, D_MODEL), expert)],
            out_specs=pl.BlockSpec((MOE_TM // SUB, TOK_SUB, SUB, LANES), tile),
        ),
        out_shape=jax.ShapeDtypeStruct(xs.shape, F32),
        input_output_aliases={2: 0},
        compiler_params=pltpu.CompilerParams(
            dimension_semantics=("arbitrary",), vmem_limit_bytes=VMEM_LIMIT_BYTES),
        name="moe_experts",
    )(tile_e, ntile, xs, wgu, wd)


def _combine_kernel(pos_ref, x1t_ref, route_ref, g2_ref, b2_ref, ys_hbm, o_ref, gbuf, sem, *, n, tc):
    base = pl.program_id(0) * tc

    def issue(t, c):
        for k in range(TOP_K):
            _row_copy(ys_hbm, gbuf, pos_ref[k * n + base + t], t, sem, dst_lead=(k,)).start()
        return c

    lax.fori_loop(0, tc, issue, 0, unroll=8)

    def drain(t, c):
        for k in range(TOP_K):
            _row_copy(ys_hbm, gbuf, 0, 0, sem, dst_lead=(k,)).wait()
        return c

    lax.fori_loop(0, tc, drain, 0, unroll=8)

    route = route_ref[...]
    ffn = route[:, 2:3] * _from_tiles(gbuf, (0,)) + route[:, 3:4] * _from_tiles(gbuf, (1,))
    o_ref[...] = _layer_norm(DEEPNORM_ALPHA * _from_tiles(x1t_ref) + ffn, g2_ref[...], b2_ref[...])


def _combine_call(pos_flat, x1t, route, ys, g2, b2, tc=512):
    n = x1t.shape[0] * SUB
    return pl.pallas_call(
        functools.partial(_combine_kernel, n=n, tc=tc),
        grid_spec=pltpu.PrefetchScalarGridSpec(
            num_scalar_prefetch=1,
            grid=(n // tc,),
            in_specs=[pl.BlockSpec((tc // SUB, TOK_SUB, SUB, LANES), lambda i, p: (i, 0, 0, 0)),
                      pl.BlockSpec((tc, ROUTER_LANES), lambda i, p: (i, 0)),
                      pl.BlockSpec((1, D_MODEL), lambda i, p: (0, 0)),
                      pl.BlockSpec((1, D_MODEL), lambda i, p: (0, 0)),
                      pl.BlockSpec(memory_space=pl.ANY)],
            out_specs=pl.BlockSpec((tc, D_MODEL), lambda i, p: (i, 0)),
            scratch_shapes=[pltpu.VMEM((TOP_K, tc // SUB, TOK_SUB, SUB, LANES), F32), pltpu.SemaphoreType.DMA],
        ),
        out_shape=jax.ShapeDtypeStruct((n, D_MODEL), F32),
        compiler_params=pltpu.CompilerParams(
            dimension_semantics=("arbitrary",), vmem_limit_bytes=VMEM_LIMIT_BYTES),
        name="moe_combine_ln2",
    )(pos_flat, x1t, route, g2, b2, ys)


def _layer(x2, batch, seq, w_in, na_rpb, dn_conv_w, a_log_f, a_log_b, dt_bias_f, dt_bias_b, dn_norm_w,
           w_proj_na, w_proj_dn, w_out, ln1_g, ln1_b, w_rg, b_rg, w_re, b_re, w_gu, w_dn, ln2_g, ln2_b):
    n_act = 3 * NA_WIDTH + 4 * DN_WIDTH
    n_small = 4 * DN_HEADS
    w_main = jnp.concatenate([w_in[:, n_act + n_small:], w_in[:, :n_act]], 1).astype(BF16)
    ws = w_in[:, n_act:n_act + n_small].reshape(D_MODEL, 4, DN_HEADS // DN_HG, DN_HG)
    w_g = ws.transpose(2, 0, 1, 3).reshape(DN_HEADS // DN_HG, D_MODEL, 4 * DN_HG).astype(BF16)
    w_gt = w_g.transpose(0, 2, 1)

    def per_group(f, b):
        z = jnp.zeros((DN_HEADS // DN_HG, 2 * DN_HG), F32)
        return jnp.concatenate([z, f.reshape(-1, DN_HG), b.reshape(-1, DN_HG)], 1)

    pa, pd = per_group(a_log_f, a_log_b), per_group(dt_bias_f, dt_bias_b)
    prow = jnp.stack([pa, pd], 1)
    pcol = jnp.stack([pa, pd], 2)

    h_all, g_nat, g_t = _proj_call(x2, w_main, w_g, w_gt)
    bias = _na_bias_tables(na_rpb, seq // GRID_W)
    y_na = _na_call(h_all, bias, batch, seq)
    y_dn = _dn_call(h_all, g_nat, g_t, dn_conv_w, prow, pcol, dn_norm_w.reshape(1, DN_HEAD_DIM), batch, seq)

    wr = jnp.zeros((D_MODEL, ROUTER_LANES), F32)
    wr = wr.at[:, :N_GROUPS].set(w_rg).at[:, N_GROUPS:N_GROUPS + N_EXPERTS].set(w_re)
    br = jnp.zeros((1, ROUTER_LANES), F32)
    br = br.at[0, :N_GROUPS].set(b_rg).at[0, N_GROUPS:N_GROUPS + N_EXPERTS].set(b_re)
    x1t, route, counts = _merge_call(x2, y_na, y_dn, h_all, w_proj_na.astype(BF16), w_proj_dn.astype(BF16),
                                     w_out.astype(BF16), ln1_g.reshape(1, -1), ln1_b.reshape(1, -1), wr, br)
    pos, meta = _plan_call(route, counts)
    pos_flat = pos[:TOP_K].reshape(-1)
    xs = _dispatch_call(pos_flat, x1t)
    ys = _experts_call(meta[0, :_moe_tiles(x2.shape[0])], meta[1, :1], xs, w_gu, w_dn)
    return _combine_call(pos_flat, x1t, route, ys, ln2_g.reshape(1, -1), ln2_b.reshape(1, -1))


def kernel(x, w_in, na_rpb, dn_conv_w, dn_a_log_f, dn_a_log_b, dn_dt_bias_f, dn_dt_bias_b, dn_norm_w, w_proj_na, w_proj_dn, w_out, ln1_g, ln1_b, w_router_group, b_router_group, w_router_expert, b_router_expert, w_expert_gate_up, w_expert_down, ln2_g, ln2_b):
    batch, seq, d = x.shape
    x2 = x.reshape(batch * seq, d)
    for l in range(w_in.shape[0]):
        x2 = _layer(x2, batch, seq, w_in[l], na_rpb[l], dn_conv_w[l], dn_a_log_f[l], dn_a_log_b[l],
                    dn_dt_bias_f[l], dn_dt_bias_b[l], dn_norm_w[l], w_proj_na[l], w_proj_dn[l], w_out[l],
                    ln1_g[l], ln1_b[l], w_router_group[l], b_router_group[l], w_router_expert[l],
                    b_router_expert[l], w_expert_gate_up[l], w_expert_down[l], ln2_g[l], ln2_b[l])
    return x2.reshape(batch, seq, d)
```

```python
import functools

import numpy as np
import jax
import jax.numpy as jnp
from jax import lax
from jax.experimental import pallas as pl
from jax.experimental.pallas import tpu as pltpu

F32 = jnp.float32
BF16 = jnp.bfloat16

D_MODEL = 1024
GRID_W = 64
NA_HEADS = 8
NA_HEAD_DIM = 64
NA_WIN_ROWS = 8
NA_WIN_COLS = 16
NA_WIDTH = NA_HEADS * NA_HEAD_DIM
DN_HEADS = 8
DN_HEAD_DIM = 64
DN_WIDTH = DN_HEADS * DN_HEAD_DIM
DN_CONV = 5
CHUNK = 64
N_GROUPS = 4
EXPERTS_PER_GROUP = 8
N_EXPERTS = N_GROUPS * EXPERTS_PER_GROUP
D_EXPERT = 256
TOP_K = 2
DEPTH = 1
DEEPNORM_ALPHA = (2.0 * DEPTH) ** 0.25
LN_EPS = 1e-5
RMS_EPS = 1e-6

LANES = 128
VMEM_LIMIT_BYTES = 56 * 1024 * 1024

COL_GNA, COL_GDN = 0, 1024
COL_QNA, COL_KNA, COL_VNA = 2048, 2560, 3072
COL_QDN, COL_KDN, COL_VDN, COL_ZDN = 3584, 4096, 4608, 5120
H_COLS = 5632
PROJ_CHUNK = 512

NA_QROWS = 4
NA_KROWS = 12
NA_TQ = NA_QROWS * GRID_W
NA_TK = NA_KROWS * GRID_W

DN_HG = 4
DN_GW = DN_HG * DN_HEAD_DIM
DN_INV_BASE = 16
DN_PREP_UNROLL = 2
ROUTER_LANES = 128
NEG_BIG = -1e30
TOK_SUB = D_MODEL // LANES
SUB = 8
MOE_TM = 256


def _sigmoid(x):
    return 1.0 / (1.0 + jnp.exp(-x))


def _silu(x):
    return x * _sigmoid(x)


def _softplus(x):
    return jnp.maximum(x, 0.0) + jnp.log(1.0 + jnp.exp(-jnp.abs(x)))


def _split3(x):
    x1 = x.astype(BF16)
    r1 = x - x1.astype(F32)
    x2 = r1.astype(BF16)
    r2 = r1 - x2.astype(F32)
    return x1, x2, r2.astype(BF16)


def _dot(a, b):
    return jnp.dot(a, b, preferred_element_type=F32)


def _dot_nt(a, b):
    return lax.dot_general(a, b, (((1,), (1,)), ((), ())), preferred_element_type=F32)


def _dot_exact_lhs(a_bf16_exact, x):
    x1, x2, x3 = _split3(x)
    return _dot(a_bf16_exact, x1) + _dot(a_bf16_exact, x2) + _dot(a_bf16_exact, x3)


def _dot_exact_rhs(x, a_bf16_exact):
    x1, x2, x3 = _split3(x)
    return _dot(x1, a_bf16_exact) + _dot(x2, a_bf16_exact) + _dot(x3, a_bf16_exact)


def _dot_f32(a, b):
    a1, a2, a3 = _split3(a)
    b1, b2, b3 = _split3(b)
    return (_dot(a1, b1) + (_dot(a1, b2) + _dot(a2, b1))
            + (_dot(a1, b3) + _dot(a3, b1) + _dot(a2, b2)))


def _layer_norm(r, g, b):
    mu = jnp.mean(r, -1, keepdims=True)
    d = r - mu
    var = jnp.mean(d * d, -1, keepdims=True)
    return d * lax.rsqrt(var + LN_EPS) * g + b


def _to_tiles(x):
    x3 = x.reshape(x.shape[0] // SUB, SUB, D_MODEL)
    return [x3[:, :, kk * LANES:(kk + 1) * LANES] for kk in range(TOK_SUB)]


def _from_tiles(ref, lead=()):
    slabs = [ref[lead + (slice(None), kk)] for kk in range(TOK_SUB)]
    x3 = jnp.concatenate(slabs, -1)
    return x3.reshape(x3.shape[0] * SUB, D_MODEL)


def _row_of(ref, row, lead=()):
    hi, lo = row if isinstance(row, tuple) else (lax.shift_right_logical(row, 3), jnp.bitwise_and(row, SUB - 1))
    return ref.at[lead + (hi, slice(None), lo, slice(None))]


def _proj_kernel(x_ref, w_ref, wg_ref, wgt_ref, h_ref, g_ref, gt_ref):
    xb = x_ref[...].astype(BF16)
    for c in range(H_COLS // PROJ_CHUNK):
        cs = slice(c * PROJ_CHUNK, (c + 1) * PROJ_CHUNK)
        h_ref[:, cs] = _dot(xb, w_ref[:, cs]).astype(BF16)
    for hg in range(DN_HEADS // DN_HG):
        g_ref[hg] = _dot(xb, wg_ref[hg])
        gt_ref[hg] = _dot_nt(wgt_ref[hg], xb)


def _proj_call(x2, w_main, w_g, w_gt, tm=512):
    n = x2.shape[0]
    ng = DN_HEADS // DN_HG
    gw = 4 * DN_HG
    return pl.pallas_call(
        _proj_kernel,
        grid=(n // tm,),
        in_specs=[
            pl.BlockSpec((tm, D_MODEL), lambda i: (i, 0)),
            pl.BlockSpec((D_MODEL, H_COLS), lambda i: (0, 0)),
            pl.BlockSpec((ng, D_MODEL, gw), lambda i: (0, 0, 0)),
            pl.BlockSpec((ng, gw, D_MODEL), lambda i: (0, 0, 0)),
        ],
        out_specs=[
            pl.BlockSpec((tm, H_COLS), lambda i: (i, 0)),
            pl.BlockSpec((ng, tm, gw), lambda i: (0, i, 0)),
            pl.BlockSpec((ng, gw, tm), lambda i: (0, 0, i)),
        ],
        out_shape=[
            jax.ShapeDtypeStruct((n, H_COLS), BF16),
            jax.ShapeDtypeStruct((ng, n, gw), F32),
            jax.ShapeDtypeStruct((ng, gw, n), F32),
        ],
        compiler_params=pltpu.CompilerParams(
            dimension_semantics=("arbitrary",), vmem_limit_bytes=VMEM_LIMIT_BYTES),
        name="proj",
    )(x2, w_main, w_g, w_gt)


def _na_bias_tables(rpb, rows):
    kr_win = min(NA_WIN_ROWS, rows)
    ndr, ndc = 2 * NA_WIN_ROWS - 1, 2 * NA_WIN_COLS - 1
    pair = LANES // GRID_W
    c = np.arange(GRID_W)
    col_start = np.clip(c - NA_WIN_COLS // 2, 0, GRID_W - NA_WIN_COLS)
    col_ok = (c[None, :] >= col_start[:, None]) & (c[None, :] < col_start[:, None] + NA_WIN_COLS)
    dc = np.clip(c[None, :] - c[:, None], -(NA_WIN_COLS - 1), NA_WIN_COLS - 1) + (NA_WIN_COLS - 1)
    dc = np.where(col_ok, dc, ndc)
    col_hot = (dc[None] == np.arange(ndc + 1)[:, None, None]).astype(np.float32)
    col_hot2 = np.zeros((pair, ndc + 1, GRID_W, pair, GRID_W), np.float32)
    for u in range(pair):
        col_hot2[u, :, :, u, :] = col_hot
    col_hot2 = col_hot2.reshape(pair, ndc + 1, GRID_W, LANES)
    row_hot = []
    for r0 in (0, 2 * NA_QROWS, rows - NA_QROWS):
        kw0 = int(np.clip(r0 - NA_WIN_ROWS // 2, 0, rows - NA_KROWS))
        qr = r0 + np.arange(NA_QROWS)
        kr = kw0 + np.arange(NA_KROWS)
        row_start = np.clip(qr - kr_win // 2, 0, rows - kr_win)
        row_ok = (kr[None, :] >= row_start[:, None]) & (kr[None, :] < row_start[:, None] + kr_win)
        dr = np.where(row_ok, kr[None, :] - qr[:, None] + (NA_WIN_ROWS - 1), ndr)
        row_hot.append((dr[:, :, None] == np.arange(ndr + 1)).astype(np.float32))
    row_hot = np.stack(row_hot).reshape(3, NA_QROWS, NA_KROWS // pair, pair * (ndr + 1))
    ext = jnp.pad(rpb, ((0, 0), (0, 1), (0, 1)), constant_values=NEG_BIG)
    hi = lax.Precision.HIGHEST
    x = jnp.einsum('hrd,udql->hurql', ext, col_hot2, precision=hi)
    x = x.reshape(rpb.shape[0], pair * (ndr + 1), GRID_W, LANES)
    return jnp.einsum('vimR,hRql->vhimql', row_hot, x, precision=hi)


def _na_kernel(q_ref, k_ref, v_ref, bias_ref, o_ref, *, rows):
    qt = pl.program_id(2)
    kw0 = jnp.clip(qt * NA_QROWS - NA_WIN_ROWS // 2, 0, rows - NA_KROWS) * GRID_W
    kw0 = pl.multiple_of(kw0, GRID_W)
    scale = NA_HEAD_DIM ** -0.5
    for hh in range(LANES // NA_HEAD_DIM):
        ls = slice(hh * NA_HEAD_DIM, (hh + 1) * NA_HEAD_DIM)
        q = q_ref[:, ls]
        k = k_ref[pl.ds(kw0, NA_TK), ls]
        v = v_ref[pl.ds(kw0, NA_TK), ls]
        bias = jnp.concatenate([jnp.concatenate([bias_ref[0, hh, i, m] for m in range(NA_TK // LANES)], 1)
                                for i in range(NA_QROWS)], 0)
        s = _dot_nt(q, k) * scale + bias
        m = jnp.max(s, -1, keepdims=True)
        p = jnp.exp(s - m)
        l = jnp.sum(p, -1, keepdims=True)
        o = _dot(p.astype(BF16), v) / l
        o_ref[:, ls] = o.astype(BF16)


def _na_call(h_all, bias, batch, seq):
    rows = seq // GRID_W
    nqt = rows // NA_QROWS
    hp = NA_WIDTH // LANES
    n = batch * seq

    def bias_idx(b, p, t):
        return (jnp.where(t == 0, 0, jnp.where(t == nqt - 1, 2, 1)), p, 0, 0, 0, 0)

    return pl.pallas_call(
        functools.partial(_na_kernel, rows=rows),
        grid=(batch, hp, nqt),
        in_specs=[
            pl.BlockSpec((NA_TQ, LANES), lambda b, p, t: (b * nqt + t, COL_QNA // LANES + p)),
            pl.BlockSpec((seq, LANES), lambda b, p, t: (b, COL_KNA // LANES + p)),
            pl.BlockSpec((seq, LANES), lambda b, p, t: (b, COL_VNA // LANES + p)),
            pl.BlockSpec((1, LANES // NA_HEAD_DIM, NA_QROWS, NA_TK // LANES, GRID_W, LANES), bias_idx),
        ],
        out_specs=pl.BlockSpec((NA_TQ, LANES), lambda b, p, t: (b * nqt + t, p)),
        out_shape=jax.ShapeDtypeStruct((n, NA_WIDTH), BF16),
        compiler_params=pltpu.CompilerParams(
            dimension_semantics=("arbitrary", "arbitrary", "arbitrary"),
            vmem_limit_bytes=VMEM_LIMIT_BYTES),
        name="natten",
    )(h_all, h_all, h_all, bias)


def _dn_kernel(q_ref, k_ref, v_ref, z_ref, g_ref, gt_ref, cwq_ref, cwk_ref, cwv_ref,
               prow_ref, pcol_ref, nw_ref, o_ref,
               qs, ks, vs, gn_s, gt_s, wq_s, ik_s, u_s, eg_s, st_s, *, seq):
    nchunk = seq // CHUNK
    hd = DN_HEAD_DIM
    rb = 256

    lane16 = lax.broadcasted_iota(jnp.int32, (1, 4 * DN_HG), 1)
    graw = g_ref[0]
    gl = -jnp.exp(prow_ref[0, 0:1, :]) * _softplus(graw + prow_ref[0, 1:2, :])
    gn_s[...] = jnp.where(lane16 < 2 * DN_HG, _sigmoid(graw), gl)
    sub16 = lax.broadcasted_iota(jnp.int32, (4 * DN_HG, 1), 0)
    grawt = gt_ref[0]
    glt = -jnp.exp(pcol_ref[0, :, 0:1]) * _softplus(grawt + pcol_ref[0, :, 1:2])
    gtt = jnp.where(sub16 < 2 * DN_HG, _sigmoid(grawt), glt)
    for n in range(nchunk):
        gt_s[n] = gtt[:, n * CHUNK:(n + 1) * CHUNK]

    pad = 16
    half = DN_CONV // 2
    for src, cw_ref, dst, norm, mul in ((q_ref, cwq_ref, qs, True, hd ** -0.5),
                                        (k_ref, cwk_ref, ks, True, 1.0),
                                        (v_ref, cwv_ref, vs, False, 1.0)):
        cw = cw_ref[...]
        for r0 in range(0, seq, rb):
            lo, hi = r0 - pad, r0 + rb + pad
            parts = []
            if lo < 0:
                parts.append(jnp.zeros((pad, DN_GW), F32))
            parts.append(src[max(lo, 0):min(hi, seq), :].astype(F32))
            if hi > seq:
                parts.append(jnp.zeros((pad, DN_GW), F32))
            xin = jnp.concatenate(parts, 0) if len(parts) > 1 else parts[0]
            nrow = rb + 2 * pad
            y = jnp.zeros((rb, DN_GW), F32)
            for i in range(DN_CONV):
                sh = (half - i) % nrow
                xs = xin if sh == 0 else pltpu.roll(xin, sh, 0)
                y = y + xs[pad:pad + rb, :] * cw[i:i + 1, :]
            y = _silu(y)
            for hl in range(DN_HG):
                seg = y[:, hl * hd:(hl + 1) * hd]
                if norm:
                    seg = seg * (lax.rsqrt(jnp.sum(seg * seg, -1, keepdims=True) + RMS_EPS) * mul)
                dst[hl, r0:r0 + rb, :] = seg.astype(BF16)

    ri = lax.broadcasted_iota(jnp.int32, (CHUNK, CHUNK), 0)
    ci = lax.broadcasted_iota(jnp.int32, (CHUNK, CHUNK), 1)
    eye = (ri == ci).astype(F32)
    tril = (ri >= ci).astype(BF16)
    triu = (ri <= ci).astype(BF16)
    lower_incl, lower_strict = ri >= ci, ri > ci
    upper_incl, upper_strict = ri <= ci, ri < ci
    base_sh = DN_INV_BASE.bit_length() - 1
    base_mask = ((ri >> base_sh) == (ci >> base_sh)).astype(F32)
    level_masks = []
    s = DN_INV_BASE
    while s < CHUNK:
        sh = s.bit_length() - 1
        level_masks.append((((ri >> (sh + 1)) == (ci >> (sh + 1))) & ((ri >> sh) != (ci >> sh))).astype(F32))
        s *= 2

    dirs = ((lower_incl, lower_strict, CHUNK - 1), (upper_incl, upper_strict, 0))

    def prep(it, carry):
        probs = []
        for c in range(DN_PREP_UNROLL):
            n = it * DN_PREP_UNROLL + c
            r0 = pl.multiple_of(n * CHUNK, CHUNK)
            gcol = gn_s[pl.ds(r0, CHUNK), :]
            grow = gt_s[n]
            cs_col = (_dot_exact_lhs(tril, gcol), _dot_exact_lhs(triu, gcol))
            cs_row = (_dot_exact_rhs(grow, triu), _dot_exact_rhs(grow, tril))
            for hl in range(DN_HG):
                q = qs[hl, pl.ds(r0, CHUNK), :]
                k = ks[hl, pl.ds(r0, CHUNK), :]
                v = vs[hl, pl.ds(r0, CHUNK), :]
                pr = _dot_nt(jnp.concatenate([q, k], 0), k)
                kt = k.astype(F32).T
                for d in range(2):
                    col = 2 * DN_HG + d * DN_HG + hl
                    probs.append(dict(n=n, r0=r0, hl=hl, d=d, q=q, k=k, v=v, pr=pr, kt=kt,
                                      bcol=gcol[:, d * DN_HG + hl:d * DN_HG + hl + 1],
                                      ccol=cs_col[d][:, col:col + 1], crow=cs_row[d][col:col + 1, :]))
        for p in probs:
            incl, strict, last = dirs[p["d"]]
            kf, vf, qf = p["k"].astype(F32), p["v"].astype(F32), p["q"].astype(F32)
            qk, kk = p["pr"][:CHUNK], p["pr"][CHUNK:]
            cb = jnp.broadcast_to(p["ccol"], (CHUNK, CHUNK))
            bb = jnp.broadcast_to(p["bcol"], (CHUNK, CHUNK))
            crow = p["crow"]
            dec = jnp.exp(jnp.where(incl, cb - crow, -jnp.inf))
            p["nmat"] = jnp.where(strict, bb * kk * dec, 0.0)
            p["intra"] = (qk * dec).astype(BF16)
            eb = jnp.exp(cb)
            p["rhs"] = jnp.concatenate([vf * bb, kf * (bb * eb)], 1).astype(BF16)
            glast = p["ccol"][last:last + 1, :]
            p["qdec"] = (qf * eb).astype(BF16)
            p["kdect"] = (p["kt"] * jnp.exp(glast - crow)).astype(BF16)
            p["eg"] = jnp.broadcast_to(jnp.exp(glast), (8, LANES))
            p["nb"] = p["nmat"] * base_mask
        pows = [_dot(p["nb"].astype(BF16), p["nb"].astype(BF16)) for p in probs]
        for p in probs:
            p["t"] = eye - p["nb"]
        e = 2
        while e < DN_INV_BASE:
            last = 2 * e >= DN_INV_BASE
            ts = [_dot(p["t"].astype(BF16), (eye + w).astype(BF16)) for p, w in zip(probs, pows)]
            if not last:
                pows = [_dot(w.astype(BF16), w.astype(BF16)) for w in pows]
            for p, t in zip(probs, ts):
                p["t"] = t
            e *= 2
        for m in level_masks:
            xs = [_dot((p["nmat"] * m).astype(BF16), p["t"].astype(BF16)) for p in probs]
            ys = [_dot(p["t"].astype(BF16), x.astype(BF16)) for p, x in zip(probs, xs)]
            for p, y in zip(probs, ys):
                p["t"] = p["t"] - y
        sols = [_dot(p["t"].astype(BF16), p["rhs"]) for p in probs]
        for p, sol in zip(probs, sols):
            d, hl, n = p["d"], p["hl"], p["n"]
            u_s[d, hl, pl.ds(p["r0"], CHUNK), :] = sol[:, :hd]
            wq_s[d, hl, n, 0:CHUNK, :] = sol[:, hd:].astype(BF16)
            wq_s[d, hl, n, CHUNK:2 * CHUNK, :] = p["qdec"]
            ik_s[d, hl, n, 0:CHUNK, :] = p["intra"]
            ik_s[d, hl, n, CHUNK:2 * CHUNK, :] = p["kdect"]
            eg_s[d, hl, n] = p["eg"]
        return carry

    lax.fori_loop(0, nchunk // DN_PREP_UNROLL, prep, 0)

    st_s[...] = jnp.zeros_like(st_s)

    def scan(i, carry):
        probs = []
        for d in range(2):
            n = i if d == 0 else nchunk - 1 - i
            r0 = pl.multiple_of(n * CHUNK, CHUNK)
            for hl in range(DN_HG):
                probs.append((d, hl, n, r0))
        sts = [st_s[d, hl] for d, hl, n, r0 in probs]
        ts = [_dot(wq_s[d, hl, n], st.astype(BF16))
              for (d, hl, n, r0), st in zip(probs, sts)]
        vnews = [u_s[d, hl, pl.ds(r0, CHUNK), :] - t[:CHUNK] for (d, hl, n, r0), t in zip(probs, ts)]
        t2s = [_dot(ik_s[d, hl, n], vn.astype(BF16))
               for (d, hl, n, r0), vn in zip(probs, vnews)]
        for (d, hl, n, r0), st, t, t2 in zip(probs, sts, ts, t2s):
            u_s[d, hl, pl.ds(r0, CHUNK), :] = t[CHUNK:] + t2[:CHUNK]
            st_s[d, hl] = st * eg_s[d, hl, n][0:1, 0:hd] + t2[CHUNK:]
        return carry

    lax.fori_loop(0, nchunk, scan, 0)

    nw = nw_ref[...]
    for r0 in range(0, seq, rb):
        for hl in range(DN_HG):
            o = u_s[0, hl, r0:r0 + rb, :] + u_s[1, hl, r0:r0 + rb, :]
            o = o * lax.rsqrt(jnp.mean(o * o, -1, keepdims=True) + RMS_EPS) * nw
            zz = z_ref[r0:r0 + rb, hl * hd:(hl + 1) * hd].astype(F32)
            o_ref[r0:r0 + rb, hl * hd:(hl + 1) * hd] = (o * _silu(zz)).astype(BF16)


def _dn_call(h_all, g_nat, g_t, cw, prow, pcol, norm_w, batch, seq):
    n = batch * seq
    ng = DN_HEADS // DN_HG
    gw = 4 * DN_HG
    nchunk = seq // CHUNK
    cb = DN_GW // LANES

    def col(c0):
        return lambda b, g: (b, c0 // DN_GW + g)

    return pl.pallas_call(
        functools.partial(_dn_kernel, seq=seq),
        grid=(batch, ng),
        in_specs=[
            pl.BlockSpec((seq, DN_GW), col(COL_QDN)),
            pl.BlockSpec((seq, DN_GW), col(COL_KDN)),
            pl.BlockSpec((seq, DN_GW), col(COL_VDN)),
            pl.BlockSpec((seq, DN_GW), col(COL_ZDN)),
            pl.BlockSpec((1, seq, gw), lambda b, g: (g, b, 0)),
            pl.BlockSpec((1, gw, seq), lambda b, g: (g, 0, b)),
            pl.BlockSpec((DN_CONV, DN_GW), lambda b, g: (0, g)),
            pl.BlockSpec((DN_CONV, DN_GW), lambda b, g: (0, DN_WIDTH // DN_GW + g)),
            pl.BlockSpec((DN_CONV, DN_GW), lambda b, g: (0, 2 * DN_WIDTH // DN_GW + g)),
            pl.BlockSpec((1, 2, gw), lambda b, g: (g, 0, 0)),
            pl.BlockSpec((1, gw, 2), lambda b, g: (g, 0, 0)),
            pl.BlockSpec((1, DN_HEAD_DIM), lambda b, g: (0, 0)),
        ],
        out_specs=pl.BlockSpec((seq, DN_GW), lambda b, g: (b, g)),
        out_shape=jax.ShapeDtypeStruct((n, DN_WIDTH), BF16),
        scratch_shapes=[
            pltpu.VMEM((DN_HG, seq, DN_HEAD_DIM), BF16),
            pltpu.VMEM((DN_HG, seq, DN_HEAD_DIM), BF16),
            pltpu.VMEM((DN_HG, seq, DN_HEAD_DIM), BF16),
            pltpu.VMEM((seq, gw), F32),
            pltpu.VMEM((nchunk, gw, CHUNK), F32),
            pltpu.VMEM((2, DN_HG, nchunk, 2 * CHUNK, DN_HEAD_DIM), BF16),
            pltpu.VMEM((2, DN_HG, nchunk, 2 * CHUNK, DN_HEAD_DIM), BF16),
            pltpu.VMEM((2, DN_HG, seq, DN_HEAD_DIM), F32),
            pltpu.VMEM((2, DN_HG, nchunk, 8, LANES), F32),
            pltpu.VMEM((2, DN_HG, CHUNK, DN_HEAD_DIM), F32),
        ],
        compiler_params=pltpu.CompilerParams(
            dimension_semantics=("arbitrary", "arbitrary"), vmem_limit_bytes=VMEM_LIMIT_BYTES),
        name="deltanet",
    )(h_all, h_all, h_all, h_all, g_nat, g_t, cw, cw, cw, prow, pcol, norm_w)


def _merge_kernel(x_ref, yna_ref, ydn_ref, gna_ref, gdn_ref, wpn_ref, wpd_ref, wo_ref,
                  g1_ref, b1_ref, wr_ref, br_ref, x1t_ref, route_ref, cnt_ref, carry_ref):
    a = _dot(yna_ref[...], wpn_ref[...])
    b = _dot(ydn_ref[...], wpd_ref[...])
    merged = _sigmoid(gna_ref[...].astype(F32)) * a + _sigmoid(gdn_ref[...].astype(F32)) * b
    mix = _dot(merged.astype(BF16), wo_ref[...])
    x1 = _layer_norm(DEEPNORM_ALPHA * x_ref[...] + mix, g1_ref[...], b1_ref[...])
    for kk, slab in enumerate(_to_tiles(x1)):
        x1t_ref[:, kk] = slab

    logits = _dot_f32(x1, wr_ref[...]) + br_ref[...]
    tm = logits.shape[0]
    lane = lax.broadcasted_iota(jnp.int32, logits.shape, 1)
    lanef = lane.astype(F32)
    big = float(ROUTER_LANES)
    gmask = lane < N_GROUPS
    gl = jnp.where(gmask, logits, -jnp.inf)
    gmax = jnp.max(gl, -1, keepdims=True)
    gidx = jnp.min(jnp.where(gl == gmax, lanef, big), -1, keepdims=True)
    pg = 1.0 / jnp.sum(jnp.where(gmask, jnp.exp(gl - gmax), 0.0), -1, keepdims=True)
    egrp = jnp.floor((lanef - N_GROUPS) * (1.0 / EXPERTS_PER_GROUP))
    emask = (lane >= N_GROUPS) & (lane < N_GROUPS + N_EXPERTS) & (egrp == gidx)
    el = jnp.where(emask, logits, -jnp.inf)
    m1 = jnp.max(el, -1, keepdims=True)
    i1 = jnp.min(jnp.where(el == m1, lanef, big), -1, keepdims=True)
    el2 = jnp.where(lanef == i1, -jnp.inf, el)
    m2 = jnp.max(el2, -1, keepdims=True)
    i2 = jnp.min(jnp.where(el2 == m2, lanef, big), -1, keepdims=True)
    t = jnp.exp(m2 - m1)
    p1 = pg / (1.0 + t)
    p2 = pg * t / (1.0 + t)

    @pl.when(pl.program_id(0) == 0)
    def _():
        carry_ref[...] = jnp.zeros_like(carry_ref)

    oh1 = (lanef == i1).astype(F32)
    oh2 = (lanef == i2).astype(F32)
    oh = oh1 + oh2
    ri = lax.broadcasted_iota(jnp.int32, (tm, tm), 0)
    ci = lax.broadcasted_iota(jnp.int32, (tm, tm), 1)
    before = _dot((ri > ci).astype(BF16), oh.astype(BF16)) + carry_ref[...]
    rank1 = jnp.sum(before * oh1, -1, keepdims=True)
    rank2 = jnp.sum(before * oh2, -1, keepdims=True)
    carry = carry_ref[...] + jnp.sum(oh, 0, keepdims=True)
    carry_ref[...] = carry
    cnt_ref[...] = jnp.broadcast_to(carry, cnt_ref.shape)
    route = jnp.zeros_like(logits)
    for j, col in enumerate((i1 - N_GROUPS, i2 - N_GROUPS, p1, p2, rank1, rank2)):
        route = jnp.where(lane == j, col, route)
    route_ref[...] = route


def _merge_call(x2, y_na, y_dn, h_all, wpn, wpd, wo, g1, b1, wr, br, tm=512):
    n = x2.shape[0]
    const = lambda i: (0, 0)
    return pl.pallas_call(
        _merge_kernel,
        grid=(n // tm,),
        in_specs=[
            pl.BlockSpec((tm, D_MODEL), lambda i: (i, 0)),
            pl.BlockSpec((tm, NA_WIDTH), lambda i: (i, 0)),
            pl.BlockSpec((tm, DN_WIDTH), lambda i: (i, 0)),
            pl.BlockSpec((tm, D_MODEL), lambda i: (i, COL_GNA // D_MODEL)),
            pl.BlockSpec((tm, D_MODEL), lambda i: (i, COL_GDN // D_MODEL)),
            pl.BlockSpec((NA_WIDTH, D_MODEL), const),
            pl.BlockSpec((DN_WIDTH, D_MODEL), const),
            pl.BlockSpec((D_MODEL, D_MODEL), const),
            pl.BlockSpec((1, D_MODEL), const),
            pl.BlockSpec((1, D_MODEL), const),
            pl.BlockSpec((D_MODEL, ROUTER_LANES), const),
            pl.BlockSpec((1, ROUTER_LANES), const),
        ],
        out_specs=[
            pl.BlockSpec((tm // SUB, TOK_SUB, SUB, LANES), lambda i: (i, 0, 0, 0)),
            pl.BlockSpec((tm, ROUTER_LANES), lambda i: (i, 0)),
            pl.BlockSpec((8, ROUTER_LANES), const),
        ],
        out_shape=[
            jax.ShapeDtypeStruct((n // SUB, TOK_SUB, SUB, LANES), F32),
            jax.ShapeDtypeStruct((n, ROUTER_LANES), F32),
            jax.ShapeDtypeStruct((8, ROUTER_LANES), F32),
        ],
        scratch_shapes=[pltpu.VMEM((1, ROUTER_LANES), F32)],
        compiler_params=pltpu.CompilerParams(
            dimension_semantics=("arbitrary",), vmem_limit_bytes=VMEM_LIMIT_BYTES),
        name="merge_ln1_router",
    )(x2, y_na, y_dn, h_all, h_all, wpn, wpd, wo, g1, b1, wr, br)


def _moe_tiles(n):
    return (TOP_K * n) // MOE_TM + N_EXPERTS


def _plan_kernel(route_ref, cnt_ref, pos_ref, meta_ref):
    cnt_col = cnt_ref[...].T[:, 0:1]
    sub = lax.broadcasted_iota(jnp.int32, (ROUTER_LANES, 1), 0)
    is_e = (sub >= N_GROUPS) & (sub < N_GROUPS + N_EXPERTS)
    padded = jnp.where(is_e, jnp.ceil(cnt_col * (1.0 / MOE_TM)) * MOE_TM, 0.0)
    ri = lax.broadcasted_iota(jnp.int32, (ROUTER_LANES, ROUTER_LANES), 0)
    ci = lax.broadcasted_iota(jnp.int32, (ROUTER_LANES, ROUTER_LANES), 1)
    pb = jnp.broadcast_to(padded, (ROUTER_LANES, ROUTER_LANES))
    off = _dot_exact_lhs((ri > ci).astype(BF16), pb)[:, 0:1]
    end = off + padded

    rt = route_ref[...].T
    tm = rt.shape[1]
    subt = lax.broadcasted_iota(jnp.int32, (ROUTER_LANES, tm), 0).astype(F32) - N_GROUPS
    rows = []
    for k in range(TOP_K):
        e_row, r_row = rt[k:k + 1, :], rt[4 + k:5 + k, :]
        start = jnp.sum(jnp.where(subt == e_row, off, 0.0), 0, keepdims=True)
        rows.append(start + r_row)
    pos = jnp.concatenate(rows + [jnp.zeros((8 - TOP_K, tm), F32)], 0)
    pos_ref[...] = pos.astype(jnp.int32)

    tl = lax.broadcasted_iota(jnp.int32, (ROUTER_LANES, meta_ref.shape[1]), 1).astype(F32) * MOE_TM
    tile_e = jnp.sum(jnp.where(is_e & (end <= tl), 1.0, 0.0), 0, keepdims=True)
    ntile = jnp.max(end, 0, keepdims=True) * (1.0 / MOE_TM)
    last_col = jnp.where(is_e & (padded > 0.0), end * (1.0 / MOE_TM) - 1.0, -1.0)
    last_row = jnp.broadcast_to(last_col, (ROUTER_LANES, ROUTER_LANES)).T[0:1, :]
    last_row = jnp.concatenate([last_row, jnp.full((1, meta_ref.shape[1] - ROUTER_LANES), -1.0, F32)], 1)
    row = lax.broadcasted_iota(jnp.int32, meta_ref.shape, 0)
    meta = jnp.where(row == 0, jnp.minimum(tile_e, N_EXPERTS - 1.0),
                     jnp.where(row == 1, ntile, jnp.where(row == 2, last_row, 0.0)))
    meta_ref[...] = meta.astype(jnp.int32)


def _plan_call(route, counts, tm=512):
    n = route.shape[0]
    mt = -(-_moe_tiles(n) // LANES) * LANES
    return pl.pallas_call(
        _plan_kernel,
        grid=(n // tm,),
        in_specs=[pl.BlockSpec((tm, ROUTER_LANES), lambda i: (i, 0)),
                  pl.BlockSpec((8, ROUTER_LANES), lambda i: (0, 0))],
        out_specs=[pl.BlockSpec((8, tm), lambda i: (0, i)),
                   pl.BlockSpec((8, mt), lambda i: (0, 0))],
        out_shape=[jax.ShapeDtypeStruct((8, n), jnp.int32),
                   jax.ShapeDtypeStruct((8, mt), jnp.int32)],
        compiler_params=pltpu.CompilerParams(dimension_semantics=("arbitrary",)),
        name="moe_plan",
    )(route, counts)


def _row_copy(src, dst, src_row, dst_row, sem, dst_lead=()):
    return pltpu.make_async_copy(_row_of(src, src_row), _row_of(dst, dst_row, dst_lead), sem)


def _dispatch_kernel(pos_ref, last_ref, nt_ref, x1t_ref, xs_hbm, zbuf, sem, zsem, *, n, tb, nt_max):
    tile_rows = MOE_TM // SUB

    def zero_copy(j):
        return pltpu.make_async_copy(zbuf, xs_hbm.at[pl.ds(j * tile_rows, tile_rows)], zsem)

    @pl.when(pl.program_id(0) == 0)
    def _():
        zbuf[...] = jnp.zeros_like(zbuf)
        for e in range(N_EXPERTS):
            @pl.when(last_ref[e] >= 0)
            def _():
                zero_copy(last_ref[e]).start()
        lax.fori_loop(nt_ref[0], nt_max, lambda j, c: (zero_copy(j).start(), c)[1], 0)
        for e in range(N_EXPERTS):
            @pl.when(last_ref[e] >= 0)
            def _():
                zero_copy(0).wait()
        lax.fori_loop(nt_ref[0], nt_max, lambda j, c: (zero_copy(0).wait(), c)[1], 0)

    base = pl.program_id(0) * tb

    def issue(j, c):
        for u in range(SUB):
            for k in range(TOP_K):
                _row_copy(x1t_ref, xs_hbm, (j, u), pos_ref[k * n + base + j * SUB + u], sem).start(priority=k)
        return c

    lax.fori_loop(0, tb // SUB, issue, 0)

    def drain(t, c):
        for k in range(TOP_K):
            _row_copy(x1t_ref, xs_hbm, 0, 0, sem).wait()
        return c

    lax.fori_loop(0, tb, drain, 0, unroll=8)


def _dispatch_call(pos_flat, last_tile, ntile, x1t, tb=2048):
    n = x1t.shape[0] * SUB
    nt_max = _moe_tiles(n)
    return pl.pallas_call(
        functools.partial(_dispatch_kernel, n=n, tb=tb, nt_max=nt_max),
        grid_spec=pltpu.PrefetchScalarGridSpec(
            num_scalar_prefetch=3,
            grid=(n // tb,),
            in_specs=[pl.BlockSpec((tb // SUB, TOK_SUB, SUB, LANES), lambda i, p, l, m: (i, 0, 0, 0))],
            out_specs=pl.BlockSpec(memory_space=pl.ANY),
            scratch_shapes=[pltpu.VMEM((MOE_TM // SUB, TOK_SUB, SUB, LANES), F32),
                            pltpu.SemaphoreType.DMA, pltpu.SemaphoreType.DMA],
        ),
        out_shape=jax.ShapeDtypeStruct((nt_max * MOE_TM // SUB, TOK_SUB, SUB, LANES), F32),
        compiler_params=pltpu.CompilerParams(
            dimension_semantics=("arbitrary",), has_side_effects=True, vmem_limit_bytes=VMEM_LIMIT_BYTES),
        name="moe_dispatch",
    )(pos_flat, last_tile, ntile, x1t)


def _experts_kernel(te_ref, nt_ref, xs_ref, wgu_ref, wd_ref, ys_ref):
    @pl.when(pl.program_id(0) < nt_ref[0])
    def _():
        hgu = _dot(_from_tiles(xs_ref).astype(BF16), wgu_ref[0].astype(BF16))
        hid = _silu(hgu[:, :D_EXPERT]) * hgu[:, D_EXPERT:]
        y = _dot(hid.astype(BF16), wd_ref[0].astype(BF16))
        for kk, slab in enumerate(_to_tiles(y)):
            ys_ref[:, kk] = slab


def _experts_call(tile_e, ntile, xs, wgu, wd):
    nt_max = xs.shape[0] * SUB // MOE_TM

    def tile(j, te, nt):
        return (jnp.minimum(j, nt[0] - 1), 0, 0, 0)

    def expert(j, te, nt):
        return (te[jnp.minimum(j, nt[0] - 1)], 0, 0)

    return pl.pallas_call(
        _experts_kernel,
        grid_spec=pltpu.PrefetchScalarGridSpec(
            num_scalar_prefetch=2,
            grid=(nt_max,),
            in_specs=[pl.BlockSpec((MOE_TM // SUB, TOK_SUB, SUB, LANES), tile),
                      pl.BlockSpec((1, D_MODEL, 2 * D_EXPERT), expert),
                      pl.BlockSpec((1, D_EXPERT, D_MODEL), expert)],
            out_specs=pl.BlockSpec((MOE_TM // SUB, TOK_SUB, SUB, LANES), tile),
        ),
        out_shape=jax.ShapeDtypeStruct(xs.shape, F32),
        input_output_aliases={2: 0},
        compiler_params=pltpu.CompilerParams(
            dimension_semantics=("arbitrary",), vmem_limit_bytes=VMEM_LIMIT_BYTES),
        name="moe_experts",
    )(tile_e, ntile, xs, wgu, wd)


def _combine_kernel(pos_ref, x1t_ref, route_ref, g2_ref, b2_ref, ys_hbm, o_ref, gbuf, sems, *, n, tc):
    i = pl.program_id(0)
    slot = i % 2

    def issue(step, sl):
        def body(j, c):
            for u in range(SUB):
                for k in range(TOP_K):
                    _row_copy(ys_hbm, gbuf, pos_ref[k * n + step * tc + j * SUB + u], (j, u), sems.at[sl],
                              dst_lead=(sl, k)).start(priority=k)
            return c
        lax.fori_loop(0, tc // SUB, body, 0)

    @pl.when(i == 0)
    def _():
        issue(0, 0)

    @pl.when(i + 1 < pl.num_programs(0))
    def _():
        issue(i + 1, 1 - slot)

    def drain(t, c):
        for k in range(TOP_K):
            _row_copy(ys_hbm, gbuf, 0, 0, sems.at[slot], dst_lead=(slot, k)).wait()
        return c

    lax.fori_loop(0, tc, drain, 0, unroll=8)

    route = route_ref[...]
    ffn = route[:, 2:3] * _from_tiles(gbuf, (slot, 0)) + route[:, 3:4] * _from_tiles(gbuf, (slot, 1))
    o_ref[...] = _layer_norm(DEEPNORM_ALPHA * _from_tiles(x1t_ref) + ffn, g2_ref[...], b2_ref[...])


def _combine_call(pos_flat, x1t, route, ys, g2, b2, tc=512):
    n = x1t.shape[0] * SUB
    return pl.pallas_call(
        functools.partial(_combine_kernel, n=n, tc=tc),
        grid_spec=pltpu.PrefetchScalarGridSpec(
            num_scalar_prefetch=1,
            grid=(n // tc,),
            in_specs=[pl.BlockSpec((tc // SUB, TOK_SUB, SUB, LANES), lambda i, p: (i, 0, 0, 0)),
                      pl.BlockSpec((tc, ROUTER_LANES), lambda i, p: (i, 0)),
                      pl.BlockSpec((1, D_MODEL), lambda i, p: (0, 0)),
                      pl.BlockSpec((1, D_MODEL), lambda i, p: (0, 0)),
                      pl.BlockSpec(memory_space=pl.ANY)],
            out_specs=pl.BlockSpec((tc, D_MODEL), lambda i, p: (i, 0)),
            scratch_shapes=[pltpu.VMEM((2, TOP_K, tc // SUB, TOK_SUB, SUB, LANES), F32),
                            pltpu.SemaphoreType.DMA((2,))],
        ),
        out_shape=jax.ShapeDtypeStruct((n, D_MODEL), F32),
        compiler_params=pltpu.CompilerParams(
            dimension_semantics=("arbitrary",), vmem_limit_bytes=VMEM_LIMIT_BYTES),
        name="moe_combine_ln2",
    )(pos_flat, x1t, route, g2, b2, ys)


def _layer(x2, batch, seq, w_in, na_rpb, dn_conv_w, a_log_f, a_log_b, dt_bias_f, dt_bias_b, dn_norm_w,
           w_proj_na, w_proj_dn, w_out, ln1_g, ln1_b, w_rg, b_rg, w_re, b_re, w_gu, w_dn, ln2_g, ln2_b):
    n_act = 3 * NA_WIDTH + 4 * DN_WIDTH
    n_small = 4 * DN_HEADS
    w_main = jnp.concatenate([w_in[:, n_act + n_small:], w_in[:, :n_act]], 1).astype(BF16)
    ws = w_in[:, n_act:n_act + n_small].reshape(D_MODEL, 4, DN_HEADS // DN_HG, DN_HG)
    w_g = ws.transpose(2, 0, 1, 3).reshape(DN_HEADS // DN_HG, D_MODEL, 4 * DN_HG).astype(BF16)
    w_gt = w_g.transpose(0, 2, 1)

    def per_group(f, b):
        z = jnp.zeros((DN_HEADS // DN_HG, 2 * DN_HG), F32)
        return jnp.concatenate([z, f.reshape(-1, DN_HG), b.reshape(-1, DN_HG)], 1)

    pa, pd = per_group(a_log_f, a_log_b), per_group(dt_bias_f, dt_bias_b)
    prow = jnp.stack([pa, pd], 1)
    pcol = jnp.stack([pa, pd], 2)

    h_all, g_nat, g_t = _proj_call(x2, w_main, w_g, w_gt)
    bias = _na_bias_tables(na_rpb, seq // GRID_W)
    y_na = _na_call(h_all, bias, batch, seq)
    y_dn = _dn_call(h_all, g_nat, g_t, dn_conv_w, prow, pcol, dn_norm_w.reshape(1, DN_HEAD_DIM), batch, seq)

    wr = jnp.zeros((D_MODEL, ROUTER_LANES), F32)
    wr = wr.at[:, :N_GROUPS].set(w_rg).at[:, N_GROUPS:N_GROUPS + N_EXPERTS].set(w_re)
    br = jnp.zeros((1, ROUTER_LANES), F32)
    br = br.at[0, :N_GROUPS].set(b_rg).at[0, N_GROUPS:N_GROUPS + N_EXPERTS].set(b_re)
    x1t, route, counts = _merge_call(x2, y_na, y_dn, h_all, w_proj_na.astype(BF16), w_proj_dn.astype(BF16),
                                     w_out.astype(BF16), ln1_g.reshape(1, -1), ln1_b.reshape(1, -1), wr, br)
    pos, meta = _plan_call(route, counts)
    pos_flat = pos[:TOP_K].reshape(-1)
    ntile = meta[1, :1]
    xs = _dispatch_call(pos_flat, meta[2, N_GROUPS:N_GROUPS + N_EXPERTS], ntile, x1t)
    ys = _experts_call(meta[0, :_moe_tiles(x2.shape[0])], ntile, xs, w_gu, w_dn)
    return _combine_call(pos_flat, x1t, route, ys, ln2_g.reshape(1, -1), ln2_b.reshape(1, -1))


def kernel(x, w_in, na_rpb, dn_conv_w, dn_a_log_f, dn_a_log_b, dn_dt_bias_f, dn_dt_bias_b, dn_norm_w, w_proj_na, w_proj_dn, w_out, ln1_g, ln1_b, w_router_group, b_router_group, w_router_expert, b_router_expert, w_expert_gate_up, w_expert_down, ln2_g, ln2_b):
    batch, seq, d = x.shape
    x2 = x.reshape(batch * seq, d)
    for l in range(w_in.shape[0]):
        x2 = _layer(x2, batch, seq, w_in[l], na_rpb[l], dn_conv_w[l], dn_a_log_f[l], dn_a_log_b[l],
                    dn_dt_bias_f[l], dn_dt_bias_b[l], dn_norm_w[l], w_proj_na[l], w_proj_dn[l], w_out[l],
                    ln1_g[l], ln1_b[l], w_router_group[l], b_router_group[l], w_router_expert[l],
                    b_router_expert[l], w_expert_gate_up[l], w_expert_down[l], ln2_g[l], ln2_b[l])
    return x2.reshape(batch, seq, d)
```

```python
import functools

import numpy as np
import jax
import jax.numpy as jnp
from jax import lax
from jax.experimental import pallas as pl
from jax.experimental.pallas import tpu as pltpu

F32 = jnp.float32
BF16 = jnp.bfloat16

D_MODEL = 1024
GRID_W = 64
NA_HEADS = 8
NA_HEAD_DIM = 64
NA_WIN_ROWS = 8
NA_WIN_COLS = 16
NA_WIDTH = NA_HEADS * NA_HEAD_DIM
DN_HEADS = 8
DN_HEAD_DIM = 64
DN_WIDTH = DN_HEADS * DN_HEAD_DIM
DN_CONV = 5
CHUNK = 64
N_GROUPS = 4
EXPERTS_PER_GROUP = 8
N_EXPERTS = N_GROUPS * EXPERTS_PER_GROUP
D_EXPERT = 256
TOP_K = 2
DEPTH = 1
DEEPNORM_ALPHA = (2.0 * DEPTH) ** 0.25
LN_EPS = 1e-5
RMS_EPS = 1e-6

LANES = 128
VMEM_LIMIT_BYTES = 56 * 1024 * 1024

COL_GNA, COL_GDN = 0, 1024
COL_QNA, COL_KNA, COL_VNA = 2048, 2560, 3072
COL_QDN, COL_KDN, COL_VDN, COL_ZDN = 3584, 4096, 4608, 5120
H_COLS = 5632
PROJ_CHUNK = 512

NA_QROWS = 4
NA_KROWS = 12
NA_TQ = NA_QROWS * GRID_W
NA_TK = NA_KROWS * GRID_W

DN_HG = 4
DN_GW = DN_HG * DN_HEAD_DIM
DN_INV_BASE = 16
DN_ROW_BLOCK = 128
DN_CONV_HALO = 16
DN_PREP_UNROLL = 4
ROUTER_LANES = 128
NEG_BIG = -1e30
TOK_SUB = D_MODEL // LANES
SUB = 8
MOE_TM = 256


def _sigmoid(x):
    return 1.0 / (1.0 + jnp.exp(-x))


def _silu(x):
    return x * _sigmoid(x)


def _softplus(x):
    return jnp.maximum(x, 0.0) + jnp.log(1.0 + jnp.exp(-jnp.abs(x)))


def _split3(x):
    x1 = x.astype(BF16)
    r1 = x - x1.astype(F32)
    x2 = r1.astype(BF16)
    r2 = r1 - x2.astype(F32)
    return x1, x2, r2.astype(BF16)


def _dot(a, b):
    return jnp.dot(a, b, preferred_element_type=F32)


def _dot_nt(a, b):
    return lax.dot_general(a, b, (((1,), (1,)), ((), ())), preferred_element_type=F32)


def _dot_exact_lhs(a_bf16_exact, x):
    x1, x2, x3 = _split3(x)
    return _dot(a_bf16_exact, x1) + _dot(a_bf16_exact, x2) + _dot(a_bf16_exact, x3)


def _dot_exact_rhs(x, a_bf16_exact):
    x1, x2, x3 = _split3(x)
    return _dot(x1, a_bf16_exact) + _dot(x2, a_bf16_exact) + _dot(x3, a_bf16_exact)


def _dot_f32(a, b):
    a1, a2, a3 = _split3(a)
    b1, b2, b3 = _split3(b)
    return (_dot(a1, b1) + (_dot(a1, b2) + _dot(a2, b1))
            + (_dot(a1, b3) + _dot(a3, b1) + _dot(a2, b2)))


def _layer_norm(r, g, b):
    mu = jnp.mean(r, -1, keepdims=True)
    d = r - mu
    var = jnp.mean(d * d, -1, keepdims=True)
    return d * lax.rsqrt(var + LN_EPS) * g + b


def _to_tiles(x):
    x3 = x.reshape(x.shape[0] // SUB, SUB, D_MODEL)
    return [x3[:, :, kk * LANES:(kk + 1) * LANES] for kk in range(TOK_SUB)]


def _from_tiles(ref, lead=()):
    slabs = [ref[lead + (slice(None), kk)] for kk in range(TOK_SUB)]
    x3 = jnp.concatenate(slabs, -1)
    return x3.reshape(x3.shape[0] * SUB, D_MODEL)


def _row_of(ref, row, lead=()):
    hi, lo = row if isinstance(row, tuple) else (lax.shift_right_logical(row, 3), jnp.bitwise_and(row, SUB - 1))
    return ref.at[lead + (hi, slice(None), lo, slice(None))]


def _proj_kernel(x_ref, w_ref, wg_ref, wgt_ref, h_ref, g_ref, gt_ref):
    xb = x_ref[...].astype(BF16)
    for c in range(H_COLS // PROJ_CHUNK):
        cs = slice(c * PROJ_CHUNK, (c + 1) * PROJ_CHUNK)
        h_ref[:, cs] = _dot(xb, w_ref[:, cs]).astype(BF16)
    for hg in range(DN_HEADS // DN_HG):
        g_ref[hg] = _dot(xb, wg_ref[hg])
        gt_ref[hg] = _dot_nt(wgt_ref[hg], xb)


def _proj_call(x2, w_main, w_g, w_gt, tm=512):
    n = x2.shape[0]
    ng = DN_HEADS // DN_HG
    gw = 4 * DN_HG
    return pl.pallas_call(
        _proj_kernel,
        grid=(n // tm,),
        in_specs=[
            pl.BlockSpec((tm, D_MODEL), lambda i: (i, 0)),
            pl.BlockSpec((D_MODEL, H_COLS), lambda i: (0, 0)),
            pl.BlockSpec((ng, D_MODEL, gw), lambda i: (0, 0, 0)),
            pl.BlockSpec((ng, gw, D_MODEL), lambda i: (0, 0, 0)),
        ],
        out_specs=[
            pl.BlockSpec((tm, H_COLS), lambda i: (i, 0)),
            pl.BlockSpec((ng, tm, gw), lambda i: (0, i, 0)),
            pl.BlockSpec((ng, gw, tm), lambda i: (0, 0, i)),
        ],
        out_shape=[
            jax.ShapeDtypeStruct((n, H_COLS), BF16),
            jax.ShapeDtypeStruct((ng, n, gw), F32),
            jax.ShapeDtypeStruct((ng, gw, n), F32),
        ],
        compiler_params=pltpu.CompilerParams(
            dimension_semantics=("arbitrary",), vmem_limit_bytes=VMEM_LIMIT_BYTES),
        name="proj",
    )(x2, w_main, w_g, w_gt)


def _na_bias_tables(rpb, rows):
    kr_win = min(NA_WIN_ROWS, rows)
    ndr, ndc = 2 * NA_WIN_ROWS - 1, 2 * NA_WIN_COLS - 1
    pair = LANES // GRID_W
    c = np.arange(GRID_W)
    col_start = np.clip(c - NA_WIN_COLS // 2, 0, GRID_W - NA_WIN_COLS)
    col_ok = (c[None, :] >= col_start[:, None]) & (c[None, :] < col_start[:, None] + NA_WIN_COLS)
    dc = np.clip(c[None, :] - c[:, None], -(NA_WIN_COLS - 1), NA_WIN_COLS - 1) + (NA_WIN_COLS - 1)
    dc = np.where(col_ok, dc, ndc)
    col_hot = (dc[None] == np.arange(ndc + 1)[:, None, None]).astype(np.float32)
    col_hot2 = np.zeros((pair, ndc + 1, GRID_W, pair, GRID_W), np.float32)
    for u in range(pair):
        col_hot2[u, :, :, u, :] = col_hot
    col_hot2 = col_hot2.reshape(pair, ndc + 1, GRID_W, LANES)
    row_hot = []
    for r0 in (0, 2 * NA_QROWS, rows - NA_QROWS):
        kw0 = int(np.clip(r0 - NA_WIN_ROWS // 2, 0, rows - NA_KROWS))
        qr = r0 + np.arange(NA_QROWS)
        kr = kw0 + np.arange(NA_KROWS)
        row_start = np.clip(qr - kr_win // 2, 0, rows - kr_win)
        row_ok = (kr[None, :] >= row_start[:, None]) & (kr[None, :] < row_start[:, None] + kr_win)
        dr = np.where(row_ok, kr[None, :] - qr[:, None] + (NA_WIN_ROWS - 1), ndr)
        row_hot.append((dr[:, :, None] == np.arange(ndr + 1)).astype(np.float32))
    row_hot = np.stack(row_hot).reshape(3, NA_QROWS, NA_KROWS // pair, pair * (ndr + 1))
    ext = jnp.pad(rpb, ((0, 0), (0, 1), (0, 1)), constant_values=NEG_BIG)
    hi = lax.Precision.HIGHEST
    x = jnp.einsum('hrd,udql->hurql', ext, col_hot2, precision=hi)
    x = x.reshape(rpb.shape[0], pair * (ndr + 1), GRID_W, LANES)
    return jnp.einsum('vimR,hRql->vhimql', row_hot, x, precision=hi)


def _na_kernel(q_ref, k_ref, v_ref, bias_ref, o_ref, *, rows):
    qt = pl.program_id(2)
    kw0 = jnp.clip(qt * NA_QROWS - NA_WIN_ROWS // 2, 0, rows - NA_KROWS) * GRID_W
    kw0 = pl.multiple_of(kw0, GRID_W)
    scale = NA_HEAD_DIM ** -0.5
    for hh in range(LANES // NA_HEAD_DIM):
        ls = slice(hh * NA_HEAD_DIM, (hh + 1) * NA_HEAD_DIM)
        q = q_ref[:, ls]
        k = k_ref[pl.ds(kw0, NA_TK), ls]
        v = v_ref[pl.ds(kw0, NA_TK), ls]
        bias = jnp.concatenate([jnp.concatenate([bias_ref[0, hh, i, m] for m in range(NA_TK // LANES)], 1)
                                for i in range(NA_QROWS)], 0)
        s = _dot_nt(q, k) * scale + bias
        m = jnp.max(s, -1, keepdims=True)
        p = jnp.exp(s - m)
        l = jnp.sum(p, -1, keepdims=True)
        o = _dot(p.astype(BF16), v) / l
        o_ref[:, ls] = o.astype(BF16)


def _na_call(h_all, bias, batch, seq):
    rows = seq // GRID_W
    nqt = rows // NA_QROWS
    hp = NA_WIDTH // LANES
    n = batch * seq

    def bias_idx(b, p, t):
        return (jnp.where(t == 0, 0, jnp.where(t == nqt - 1, 2, 1)), p, 0, 0, 0, 0)

    return pl.pallas_call(
        functools.partial(_na_kernel, rows=rows),
        grid=(batch, hp, nqt),
        in_specs=[
            pl.BlockSpec((NA_TQ, LANES), lambda b, p, t: (b * nqt + t, COL_QNA // LANES + p)),
            pl.BlockSpec((seq, LANES), lambda b, p, t: (b, COL_KNA // LANES + p)),
            pl.BlockSpec((seq, LANES), lambda b, p, t: (b, COL_VNA // LANES + p)),
            pl.BlockSpec((1, LANES // NA_HEAD_DIM, NA_QROWS, NA_TK // LANES, GRID_W, LANES), bias_idx),
        ],
        out_specs=pl.BlockSpec((NA_TQ, LANES), lambda b, p, t: (b * nqt + t, p)),
        out_shape=jax.ShapeDtypeStruct((n, NA_WIDTH), BF16),
        compiler_params=pltpu.CompilerParams(
            dimension_semantics=("arbitrary", "arbitrary", "arbitrary"),
            vmem_limit_bytes=VMEM_LIMIT_BYTES),
        name="natten",
    )(h_all, h_all, h_all, bias)


def _dn_kernel(q_ref, k_ref, v_ref, z_ref, g_ref, gt_ref, cwq_ref, cwk_ref, cwv_ref,
               prow_ref, pcol_ref, nw_ref, o_ref,
               qs, ks, vs, gn_s, gt_s, wq_s, ik_s, kdt_s, u_s, eg_s, st_s, *, seq):
    nchunk = seq // CHUNK
    hd = DN_HEAD_DIM
    npair = DN_HG // 2
    rb = DN_ROW_BLOCK

    lane_p = lax.broadcasted_iota(jnp.int32, (1, LANES), 1)
    first = lane_p < hd

    def per_head(x, fn):
        s0 = jnp.sum(jnp.where(first, x, 0.0), -1, keepdims=True)
        s1 = jnp.sum(jnp.where(first, 0.0, x), -1, keepdims=True)
        return jnp.where(first, fn(s0), fn(s1))

    lane16 = lax.broadcasted_iota(jnp.int32, (1, 4 * DN_HG), 1)
    graw = g_ref[0]
    gl = -jnp.exp(prow_ref[0, 0:1, :]) * _softplus(graw + prow_ref[0, 1:2, :])
    gn_s[...] = jnp.where(lane16 < 2 * DN_HG, _sigmoid(graw), gl)
    sub16 = lax.broadcasted_iota(jnp.int32, (4 * DN_HG, 1), 0)
    grawt = gt_ref[0]
    glt = -jnp.exp(pcol_ref[0, :, 0:1]) * _softplus(grawt + pcol_ref[0, :, 1:2])
    gtt = jnp.where(sub16 < 2 * DN_HG, _sigmoid(grawt), glt)
    for n in range(nchunk):
        gt_s[n] = gtt[:, n * CHUNK:(n + 1) * CHUNK]

    pad = DN_CONV_HALO
    half = DN_CONV // 2
    for src, cw_ref, dst, norm, mul in ((q_ref, cwq_ref, qs, True, hd ** -0.5),
                                        (k_ref, cwk_ref, ks, True, 1.0),
                                        (v_ref, cwv_ref, vs, False, 1.0)):
        cw = cw_ref[...]
        for r0 in range(0, seq, rb):
            lo, hi = r0 - pad, r0 + rb + pad
            parts = []
            if lo < 0:
                parts.append(jnp.zeros((pad, DN_GW), F32))
            parts.append(src[max(lo, 0):min(hi, seq), :].astype(F32))
            if hi > seq:
                parts.append(jnp.zeros((pad, DN_GW), F32))
            xin = jnp.concatenate(parts, 0) if len(parts) > 1 else parts[0]
            nrow = rb + 2 * pad
            y = jnp.zeros((rb, DN_GW), F32)
            for i in range(DN_CONV):
                sh = (half - i) % nrow
                xs = xin if sh == 0 else pltpu.roll(xin, sh, 0)
                y = y + xs[pad:pad + rb, :] * cw[i:i + 1, :]
            y = _silu(y)
            for pp in range(npair):
                seg = y[:, pp * LANES:(pp + 1) * LANES]
                if norm:
                    seg = seg * per_head(seg * seg, lambda s: lax.rsqrt(s + RMS_EPS) * mul)
                dst[pp, r0:r0 + rb, :] = seg.astype(BF16)

    ri = lax.broadcasted_iota(jnp.int32, (CHUNK, LANES), 0)
    ci = lax.broadcasted_iota(jnp.int32, (CHUNK, LANES), 1) & (CHUNK - 1)
    eye = (ri == ci).astype(F32)
    r1 = lax.broadcasted_iota(jnp.int32, (CHUNK, CHUNK), 0)
    c1 = lax.broadcasted_iota(jnp.int32, (CHUNK, CHUNK), 1)
    tril = (r1 >= c1).astype(BF16)
    triu = (r1 <= c1).astype(BF16)
    dirs = ((ri >= ci, ri > ci, CHUNK - 1), (ri <= ci, ri < ci, 0))
    base_sh = DN_INV_BASE.bit_length() - 1
    base_mask = ((ri >> base_sh) == (ci >> base_sh)).astype(F32)
    level_masks = []
    s = DN_INV_BASE
    while s < CHUNK:
        sh = s.bit_length() - 1
        level_masks.append((((ri >> (sh + 1)) == (ci >> (sh + 1))) & ((ri >> sh) != (ci >> sh))).astype(F32))
        s *= 2
    rb2 = lax.broadcasted_iota(jnp.int32, (LANES, LANES), 0)
    cb2 = lax.broadcasted_iota(jnp.int32, (LANES, LANES), 1)
    bd_mask = ((rb2 >> 6) == (cb2 >> 6)).astype(F32)
    bd_mask_b = bd_mask.astype(BF16)

    def bd(x):
        xb = x.astype(BF16)
        return jnp.concatenate([xb, xb], 0) * bd_mask_b

    def mm(a, b):
        return _dot(a.astype(BF16), bd(b))

    def prep(it, carry):
        probs = []
        for c in range(DN_PREP_UNROLL):
            n = it * DN_PREP_UNROLL + c
            r0 = pl.multiple_of(n * CHUNK, CHUNK)
            gcol = gn_s[pl.ds(r0, CHUNK), :]
            grow = gt_s[n]
            cs_col = (_dot_exact_lhs(tril, gcol), _dot_exact_lhs(triu, gcol))
            cs_row = (_dot_exact_rhs(grow, triu), _dot_exact_rhs(grow, tril))
            for pp in range(npair):
                q = qs[pp, pl.ds(r0, CHUNK), :]
                k = ks[pp, pl.ds(r0, CHUNK), :]
                v = vs[pp, pl.ds(r0, CHUNK), :]
                pr = _dot_nt(jnp.concatenate([q, k], 0), bd(k))
                kt = k.astype(F32).T
                for d in range(2):
                    h0 = 2 * pp
                    cg = 2 * DN_HG + d * DN_HG + h0
                    cbeta = d * DN_HG + h0
                    pair_col = lambda a, c0: jnp.where(first, jnp.broadcast_to(a[:, c0:c0 + 1], (CHUNK, LANES)),
                                                       jnp.broadcast_to(a[:, c0 + 1:c0 + 2], (CHUNK, LANES)))
                    probs.append(dict(
                        n=n, r0=r0, pp=pp, d=d, q=q, k=k, v=v, pr=pr, kt=kt,
                        cb=pair_col(cs_col[d], cg), bb=pair_col(gcol, cbeta),
                        crow=jnp.concatenate([cs_row[d][cg:cg + 1, :], cs_row[d][cg + 1:cg + 2, :]], 1),
                        crows=(cs_row[d][cg:cg + 1, :], cs_row[d][cg + 1:cg + 2, :])))
        for p in probs:
            incl, strict, last = dirs[p["d"]]
            kf, vf, qf = p["k"].astype(F32), p["v"].astype(F32), p["q"].astype(F32)
            qk, kk = p["pr"][:CHUNK], p["pr"][CHUNK:]
            cb, bb = p["cb"], p["bb"]
            dec = jnp.exp(jnp.where(incl, cb - p["crow"], -jnp.inf))
            p["nmat"] = jnp.where(strict, bb * kk * dec, 0.0)
            p["intra"] = (qk * dec).astype(BF16)
            eb = jnp.exp(cb)
            p["rhs_u"] = (vf * bb).astype(BF16)
            p["rhs_w"] = (kf * (bb * eb)).astype(BF16)
            glast = cb[last:last + 1, :]
            p["qdec"] = (qf * eb).astype(BF16)
            fac = jnp.concatenate([jnp.broadcast_to(jnp.exp(glast[:, hh * hd:hh * hd + 1] - p["crows"][hh]),
                                                    (hd, CHUNK)) for hh in range(2)], 0)
            p["kdect"] = (p["kt"] * fac).astype(BF16)
            p["eg"] = jnp.broadcast_to(jnp.exp(glast), (8, LANES))
            p["nb"] = p["nmat"] * base_mask
        pows = [mm(p["nb"], p["nb"]) for p in probs]
        for p in probs:
            p["t"] = eye - p["nb"]
        e = 2
        while e < DN_INV_BASE:
            ts = [mm(p["t"], eye + w) for p, w in zip(probs, pows)]
            if 2 * e < DN_INV_BASE:
                pows = [mm(w, w) for w in pows]
            for p, t in zip(probs, ts):
                p["t"] = t
            e *= 2
        for m in level_masks:
            xs = [mm(p["nmat"] * m, p["t"]) for p in probs]
            ys = [mm(p["t"], x) for p, x in zip(probs, xs)]
            for p, y in zip(probs, ys):
                p["t"] = p["t"] - y
        us = [mm(p["t"], p["rhs_u"]) for p in probs]
        ws = [mm(p["t"], p["rhs_w"]) for p in probs]
        for p, u, w in zip(probs, us, ws):
            d, pp, n = p["d"], p["pp"], p["n"]
            u_s[d, pp, pl.ds(p["r0"], CHUNK), :] = u
            wq_s[d, pp, n, 0:CHUNK, :] = w.astype(BF16)
            wq_s[d, pp, n, CHUNK:2 * CHUNK, :] = p["qdec"]
            ik_s[d, pp, n] = p["intra"]
            kdt_s[d, pp, n] = p["kdect"]
            eg_s[d, pp, n] = p["eg"]
        return carry

    lax.fori_loop(0, nchunk // DN_PREP_UNROLL, prep, 0)

    st_s[...] = jnp.zeros_like(st_s)

    def scan(i, carry):
        probs = []
        for d in range(2):
            n = i if d == 0 else nchunk - 1 - i
            r0 = pl.multiple_of(n * CHUNK, CHUNK)
            for pp in range(npair):
                probs.append((d, pp, n, r0))
        sts = [st_s[d, pp] for d, pp, n, r0 in probs]
        ts = [_dot(wq_s[d, pp, n], st.astype(BF16))
              for (d, pp, n, r0), st in zip(probs, sts)]
        vnews = [u_s[d, pp, pl.ds(r0, CHUNK), :] - t[:CHUNK] for (d, pp, n, r0), t in zip(probs, ts)]
        ois = [_dot(ik_s[d, pp, n], bd(vn)) for (d, pp, n, r0), vn in zip(probs, vnews)]
        dss = [_dot(kdt_s[d, pp, n], vn.astype(BF16)) for (d, pp, n, r0), vn in zip(probs, vnews)]
        for (d, pp, n, r0), st, t, oi, ds in zip(probs, sts, ts, ois, dss):
            u_s[d, pp, pl.ds(r0, CHUNK), :] = t[CHUNK:] + oi
            st_s[d, pp] = st * eg_s[d, pp, n][0:1, :] + ds * bd_mask
        return carry

    lax.fori_loop(0, nchunk, scan, 0)

    nw = jnp.concatenate([nw_ref[...], nw_ref[...]], 1)
    for r0 in range(0, seq, rb):
        for pp in range(npair):
            o = u_s[0, pp, r0:r0 + rb, :] + u_s[1, pp, r0:r0 + rb, :]
            o = o * per_head(o * o, lambda s: lax.rsqrt(s * (1.0 / hd) + RMS_EPS)) * nw
            zz = z_ref[r0:r0 + rb, pp * LANES:(pp + 1) * LANES].astype(F32)
            o_ref[r0:r0 + rb, pp * LANES:(pp + 1) * LANES] = (o * _silu(zz)).astype(BF16)


def _dn_call(h_all, g_nat, g_t, cw, prow, pcol, norm_w, batch, seq):
    n = batch * seq
    ng = DN_HEADS // DN_HG
    gw = 4 * DN_HG
    nchunk = seq // CHUNK
    npair = DN_HG // 2

    def col(c0):
        return lambda b, g: (b, c0 // DN_GW + g)

    return pl.pallas_call(
        functools.partial(_dn_kernel, seq=seq),
        grid=(batch, ng),
        in_specs=[
            pl.BlockSpec((seq, DN_GW), col(COL_QDN)),
            pl.BlockSpec((seq, DN_GW), col(COL_KDN)),
            pl.BlockSpec((seq, DN_GW), col(COL_VDN)),
            pl.BlockSpec((seq, DN_GW), col(COL_ZDN)),
            pl.BlockSpec((1, seq, gw), lambda b, g: (g, b, 0)),
            pl.BlockSpec((1, gw, seq), lambda b, g: (g, 0, b)),
            pl.BlockSpec((DN_CONV, DN_GW), lambda b, g: (0, g)),
            pl.BlockSpec((DN_CONV, DN_GW), lambda b, g: (0, DN_WIDTH // DN_GW + g)),
            pl.BlockSpec((DN_CONV, DN_GW), lambda b, g: (0, 2 * DN_WIDTH // DN_GW + g)),
            pl.BlockSpec((1, 2, gw), lambda b, g: (g, 0, 0)),
            pl.BlockSpec((1, gw, 2), lambda b, g: (g, 0, 0)),
            pl.BlockSpec((1, DN_HEAD_DIM), lambda b, g: (0, 0)),
        ],
        out_specs=pl.BlockSpec((seq, DN_GW), lambda b, g: (b, g)),
        out_shape=jax.ShapeDtypeStruct((n, DN_WIDTH), BF16),
        scratch_shapes=[
            pltpu.VMEM((npair, seq, LANES), BF16),
            pltpu.VMEM((npair, seq, LANES), BF16),
            pltpu.VMEM((npair, seq, LANES), BF16),
            pltpu.VMEM((seq, gw), F32),
            pltpu.VMEM((nchunk, gw, CHUNK), F32),
            pltpu.VMEM((2, npair, nchunk, 2 * CHUNK, LANES), BF16),
            pltpu.VMEM((2, npair, nchunk, CHUNK, LANES), BF16),
            pltpu.VMEM((2, npair, nchunk, LANES, CHUNK), BF16),
            pltpu.VMEM((2, npair, seq, LANES), F32),
            pltpu.VMEM((2, npair, nchunk, 8, LANES), F32),
            pltpu.VMEM((2, npair, LANES, LANES), F32),
        ],
        compiler_params=pltpu.CompilerParams(
            dimension_semantics=("arbitrary", "arbitrary"), vmem_limit_bytes=VMEM_LIMIT_BYTES),
        name="deltanet",
    )(h_all, h_all, h_all, h_all, g_nat, g_t, cw, cw, cw, prow, pcol, norm_w)


def _merge_kernel(x_ref, yna_ref, ydn_ref, gna_ref, gdn_ref, wpn_ref, wpd_ref, wo_ref,
                  g1_ref, b1_ref, wr_ref, br_ref, x1t_ref, route_ref, cnt_ref, carry_ref):
    a = _dot(yna_ref[...], wpn_ref[...])
    b = _dot(ydn_ref[...], wpd_ref[...])
    merged = _sigmoid(gna_ref[...].astype(F32)) * a + _sigmoid(gdn_ref[...].astype(F32)) * b
    mix = _dot(merged.astype(BF16), wo_ref[...])
    x1 = _layer_norm(DEEPNORM_ALPHA * x_ref[...] + mix, g1_ref[...], b1_ref[...])
    for kk, slab in enumerate(_to_tiles(x1)):
        x1t_ref[:, kk] = slab

    logits = _dot_f32(x1, wr_ref[...]) + br_ref[...]
    tm = logits.shape[0]
    lane = lax.broadcasted_iota(jnp.int32, logits.shape, 1)
    lanef = lane.astype(F32)
    big = float(ROUTER_LANES)
    gmask = lane < N_GROUPS
    gl = jnp.where(gmask, logits, -jnp.inf)
    gmax = jnp.max(gl, -1, keepdims=True)
    gidx = jnp.min(jnp.where(gl == gmax, lanef, big), -1, keepdims=True)
    pg = 1.0 / jnp.sum(jnp.where(gmask, jnp.exp(gl - gmax), 0.0), -1, keepdims=True)
    egrp = jnp.floor((lanef - N_GROUPS) * (1.0 / EXPERTS_PER_GROUP))
    emask = (lane >= N_GROUPS) & (lane < N_GROUPS + N_EXPERTS) & (egrp == gidx)
    el = jnp.where(emask, logits, -jnp.inf)
    m1 = jnp.max(el, -1, keepdims=True)
    i1 = jnp.min(jnp.where(el == m1, lanef, big), -1, keepdims=True)
    el2 = jnp.where(lanef == i1, -jnp.inf, el)
    m2 = jnp.max(el2, -1, keepdims=True)
    i2 = jnp.min(jnp.where(el2 == m2, lanef, big), -1, keepdims=True)
    t = jnp.exp(m2 - m1)
    p1 = pg / (1.0 + t)
    p2 = pg * t / (1.0 + t)

    @pl.when(pl.program_id(0) == 0)
    def _():
        carry_ref[...] = jnp.zeros_like(carry_ref)

    oh1 = (lanef == i1).astype(F32)
    oh2 = (lanef == i2).astype(F32)
    oh = oh1 + oh2
    ri = lax.broadcasted_iota(jnp.int32, (tm, tm), 0)
    ci = lax.broadcasted_iota(jnp.int32, (tm, tm), 1)
    before = _dot((ri > ci).astype(BF16), oh.astype(BF16)) + carry_ref[...]
    rank1 = jnp.sum(before * oh1, -1, keepdims=True)
    rank2 = jnp.sum(before * oh2, -1, keepdims=True)
    carry = carry_ref[...] + jnp.sum(oh, 0, keepdims=True)
    carry_ref[...] = carry
    cnt_ref[...] = jnp.broadcast_to(carry, cnt_ref.shape)
    route = jnp.zeros_like(logits)
    for j, col in enumerate((i1 - N_GROUPS, i2 - N_GROUPS, p1, p2, rank1, rank2)):
        route = jnp.where(lane == j, col, route)
    route_ref[...] = route


def _merge_call(x2, y_na, y_dn, h_all, wpn, wpd, wo, g1, b1, wr, br, tm=512):
    n = x2.shape[0]
    const = lambda i: (0, 0)
    return pl.pallas_call(
        _merge_kernel,
        grid=(n // tm,),
        in_specs=[
            pl.BlockSpec((tm, D_MODEL), lambda i: (i, 0)),
            pl.BlockSpec((tm, NA_WIDTH), lambda i: (i, 0)),
            pl.BlockSpec((tm, DN_WIDTH), lambda i: (i, 0)),
            pl.BlockSpec((tm, D_MODEL), lambda i: (i, COL_GNA // D_MODEL)),
            pl.BlockSpec((tm, D_MODEL), lambda i: (i, COL_GDN // D_MODEL)),
            pl.BlockSpec((NA_WIDTH, D_MODEL), const),
            pl.BlockSpec((DN_WIDTH, D_MODEL), const),
            pl.BlockSpec((D_MODEL, D_MODEL), const),
            pl.BlockSpec((1, D_MODEL), const),
            pl.BlockSpec((1, D_MODEL), const),
            pl.BlockSpec((D_MODEL, ROUTER_LANES), const),
            pl.BlockSpec((1, ROUTER_LANES), const),
        ],
        out_specs=[
            pl.BlockSpec((tm // SUB, TOK_SUB, SUB, LANES), lambda i: (i, 0, 0, 0)),
            pl.BlockSpec((tm, ROUTER_LANES), lambda i: (i, 0)),
            pl.BlockSpec((8, ROUTER_LANES), const),
        ],
        out_shape=[
            jax.ShapeDtypeStruct((n // SUB, TOK_SUB, SUB, LANES), F32),
            jax.ShapeDtypeStruct((n, ROUTER_LANES), F32),
            jax.ShapeDtypeStruct((8, ROUTER_LANES), F32),
        ],
        scratch_shapes=[pltpu.VMEM((1, ROUTER_LANES), F32)],
        compiler_params=pltpu.CompilerParams(
            dimension_semantics=("arbitrary",), vmem_limit_bytes=VMEM_LIMIT_BYTES),
        name="merge_ln1_router",
    )(x2, y_na, y_dn, h_all, h_all, wpn, wpd, wo, g1, b1, wr, br)


def _moe_tiles(n):
    return (TOP_K * n) // MOE_TM + N_EXPERTS


def _plan_kernel(route_ref, cnt_ref, pos_ref, meta_ref):
    cnt_col = cnt_ref[...].T[:, 0:1]
    sub = lax.broadcasted_iota(jnp.int32, (ROUTER_LANES, 1), 0)
    is_e = (sub >= N_GROUPS) & (sub < N_GROUPS + N_EXPERTS)
    padded = jnp.where(is_e, jnp.ceil(cnt_col * (1.0 / MOE_TM)) * MOE_TM, 0.0)
    ri = lax.broadcasted_iota(jnp.int32, (ROUTER_LANES, ROUTER_LANES), 0)
    ci = lax.broadcasted_iota(jnp.int32, (ROUTER_LANES, ROUTER_LANES), 1)
    pb = jnp.broadcast_to(padded, (ROUTER_LANES, ROUTER_LANES))
    off = _dot_exact_lhs((ri > ci).astype(BF16), pb)[:, 0:1]
    end = off + padded

    rt = route_ref[...].T
    tm = rt.shape[1]
    subt = lax.broadcasted_iota(jnp.int32, (ROUTER_LANES, tm), 0).astype(F32) - N_GROUPS
    rows = []
    for k in range(TOP_K):
        e_row, r_row = rt[k:k + 1, :], rt[4 + k:5 + k, :]
        start = jnp.sum(jnp.where(subt == e_row, off, 0.0), 0, keepdims=True)
        rows.append(start + r_row)
    pos = jnp.concatenate(rows + [jnp.zeros((8 - TOP_K, tm), F32)], 0)
    pos_ref[...] = pos.astype(jnp.int32)

    tl = lax.broadcasted_iota(jnp.int32, (ROUTER_LANES, meta_ref.shape[1]), 1).astype(F32) * MOE_TM
    tile_e = jnp.sum(jnp.where(is_e & (end <= tl), 1.0, 0.0), 0, keepdims=True)
    ntile = jnp.max(end, 0, keepdims=True) * (1.0 / MOE_TM)
    last_col = jnp.where(is_e & (padded > 0.0), end * (1.0 / MOE_TM) - 1.0, -1.0)
    last_row = jnp.broadcast_to(last_col, (ROUTER_LANES, ROUTER_LANES)).T[0:1, :]
    last_row = jnp.concatenate([last_row, jnp.full((1, meta_ref.shape[1] - ROUTER_LANES), -1.0, F32)], 1)
    row = lax.broadcasted_iota(jnp.int32, meta_ref.shape, 0)
    meta = jnp.where(row == 0, jnp.minimum(tile_e, N_EXPERTS - 1.0),
                     jnp.where(row == 1, ntile, jnp.where(row == 2, last_row, 0.0)))
    meta_ref[...] = meta.astype(jnp.int32)


def _plan_call(route, counts, tm=512):
    n = route.shape[0]
    mt = -(-_moe_tiles(n) // LANES) * LANES
    return pl.pallas_call(
        _plan_kernel,
        grid=(n // tm,),
        in_specs=[pl.BlockSpec((tm, ROUTER_LANES), lambda i: (i, 0)),
                  pl.BlockSpec((8, ROUTER_LANES), lambda i: (0, 0))],
        out_specs=[pl.BlockSpec((8, tm), lambda i: (0, i)),
                   pl.BlockSpec((8, mt), lambda i: (0, 0))],
        out_shape=[jax.ShapeDtypeStruct((8, n), jnp.int32),
                   jax.ShapeDtypeStruct((8, mt), jnp.int32)],
        compiler_params=pltpu.CompilerParams(dimension_semantics=("arbitrary",)),
        name="moe_plan",
    )(route, counts)


def _row_copy(src, dst, src_row, dst_row, sem, dst_lead=()):
    return pltpu.make_async_copy(_row_of(src, src_row), _row_of(dst, dst_row, dst_lead), sem)


def _dispatch_kernel(pos_ref, last_ref, nt_ref, x1t_ref, xs_hbm, zbuf, sem, zsem, *, n, tb, nt_max):
    tile_rows = MOE_TM // SUB

    def zero_copy(j):
        return pltpu.make_async_copy(zbuf, xs_hbm.at[pl.ds(j * tile_rows, tile_rows)], zsem)

    @pl.when(pl.program_id(0) == 0)
    def _():
        zbuf[...] = jnp.zeros_like(zbuf)
        for e in range(N_EXPERTS):
            @pl.when(last_ref[e] >= 0)
            def _():
                zero_copy(last_ref[e]).start()
        lax.fori_loop(nt_ref[0], nt_max, lambda j, c: (zero_copy(j).start(), c)[1], 0)
        for e in range(N_EXPERTS):
            @pl.when(last_ref[e] >= 0)
            def _():
                zero_copy(0).wait()
        lax.fori_loop(nt_ref[0], nt_max, lambda j, c: (zero_copy(0).wait(), c)[1], 0)

    base = pl.program_id(0) * tb

    def issue(j, c):
        for u in range(SUB):
            for k in range(TOP_K):
                _row_copy(x1t_ref, xs_hbm, (j, u), pos_ref[k * n + base + j * SUB + u], sem).start(priority=k)
        return c

    lax.fori_loop(0, tb // SUB, issue, 0)

    def drain(t, c):
        for k in range(TOP_K):
            _row_copy(x1t_ref, xs_hbm, 0, 0, sem).wait()
        return c

    lax.fori_loop(0, tb, drain, 0, unroll=8)


def _dispatch_call(pos_flat, last_tile, ntile, x1t, tb=2048):
    n = x1t.shape[0] * SUB
    nt_max = _moe_tiles(n)
    return pl.pallas_call(
        functools.partial(_dispatch_kernel, n=n, tb=tb, nt_max=nt_max),
        grid_spec=pltpu.PrefetchScalarGridSpec(
            num_scalar_prefetch=3,
            grid=(n // tb,),
            in_specs=[pl.BlockSpec((tb // SUB, TOK_SUB, SUB, LANES), lambda i, p, l, m: (i, 0, 0, 0))],
            out_specs=pl.BlockSpec(memory_space=pl.ANY),
            scratch_shapes=[pltpu.VMEM((MOE_TM // SUB, TOK_SUB, SUB, LANES), F32),
                            pltpu.SemaphoreType.DMA, pltpu.SemaphoreType.DMA],
        ),
        out_shape=jax.ShapeDtypeStruct((nt_max * MOE_TM // SUB, TOK_SUB, SUB, LANES), F32),
        compiler_params=pltpu.CompilerParams(
            dimension_semantics=("arbitrary",), has_side_effects=True, vmem_limit_bytes=VMEM_LIMIT_BYTES),
        name="moe_dispatch",
    )(pos_flat, last_tile, ntile, x1t)


def _experts_kernel(te_ref, nt_ref, xs_ref, wgu_ref, wd_ref, ys_ref):
    @pl.when(pl.program_id(0) < nt_ref[0])
    def _():
        hgu = _dot(_from_tiles(xs_ref).astype(BF16), wgu_ref[0].astype(BF16))
        hid = _silu(hgu[:, :D_EXPERT]) * hgu[:, D_EXPERT:]
        y = _dot(hid.astype(BF16), wd_ref[0].astype(BF16))
        for kk, slab in enumerate(_to_tiles(y)):
            ys_ref[:, kk] = slab


def _experts_call(tile_e, ntile, xs, wgu, wd):
    nt_max = xs.shape[0] * SUB // MOE_TM

    def tile(j, te, nt):
        return (jnp.minimum(j, nt[0] - 1), 0, 0, 0)

    def expert(j, te, nt):
        return (te[jnp.minimum(j, nt[0] - 1)], 0, 0)

    return pl.pallas_call(
        _experts_kernel,
        grid_spec=pltpu.PrefetchScalarGridSpec(
            num_scalar_prefetch=2,
            grid=(nt_max,),
            in_specs=[pl.BlockSpec((MOE_TM // SUB, TOK_SUB, SUB, LANES), tile),
                      pl.BlockSpec((1, D_MODEL, 2 * D_EXPERT), expert),
                      pl.BlockSpec((1, D_EXPERT, D_MODEL), expert)],
            out_specs=pl.BlockSpec((MOE_TM // SUB, TOK_SUB, SUB, LANES), tile),
        ),
        out_shape=jax.ShapeDtypeStruct(xs.shape, F32),
        input_output_aliases={2: 0},
        compiler_params=pltpu.CompilerParams(
            dimension_semantics=("arbitrary",), vmem_limit_bytes=VMEM_LIMIT_BYTES),
        name="moe_experts",
    )(tile_e, ntile, xs, wgu, wd)


def _combine_kernel(pos_ref, x1t_ref, route_ref, g2_ref, b2_ref, ys_hbm, o_ref, gbuf, sems, *, n, tc):
    i = pl.program_id(0)
    slot = i % 2

    def issue(step, sl):
        def body(j, c):
            for u in range(SUB):
                for k in range(TOP_K):
                    _row_copy(ys_hbm, gbuf, pos_ref[k * n + step * tc + j * SUB + u], (j, u), sems.at[sl],
                              dst_lead=(sl, k)).start(priority=k)
            return c
        lax.fori_loop(0, tc // SUB, body, 0)

    @pl.when(i == 0)
    def _():
        issue(0, 0)

    @pl.when(i + 1 < pl.num_programs(0))
    def _():
        issue(i + 1, 1 - slot)

    def drain(t, c):
        for k in range(TOP_K):
            _row_copy(ys_hbm, gbuf, 0, 0, sems.at[slot], dst_lead=(slot, k)).wait()
        return c

    lax.fori_loop(0, tc, drain, 0, unroll=8)

    route = route_ref[...]
    ffn = route[:, 2:3] * _from_tiles(gbuf, (slot, 0)) + route[:, 3:4] * _from_tiles(gbuf, (slot, 1))
    o_ref[...] = _layer_norm(DEEPNORM_ALPHA * _from_tiles(x1t_ref) + ffn, g2_ref[...], b2_ref[...])


def _combine_call(pos_flat, x1t, route, ys, g2, b2, tc=512):
    n = x1t.shape[0] * SUB
    return pl.pallas_call(
        functools.partial(_combine_kernel, n=n, tc=tc),
        grid_spec=pltpu.PrefetchScalarGridSpec(
            num_scalar_prefetch=1,
            grid=(n // tc,),
            in_specs=[pl.BlockSpec((tc // SUB, TOK_SUB, SUB, LANES), lambda i, p: (i, 0, 0, 0)),
                      pl.BlockSpec((tc, ROUTER_LANES), lambda i, p: (i, 0)),
                      pl.BlockSpec((1, D_MODEL), lambda i, p: (0, 0)),
                      pl.BlockSpec((1, D_MODEL), lambda i, p: (0, 0)),
                      pl.BlockSpec(memory_space=pl.ANY)],
            out_specs=pl.BlockSpec((tc, D_MODEL), lambda i, p: (i, 0)),
            scratch_shapes=[pltpu.VMEM((2, TOP_K, tc // SUB, TOK_SUB, SUB, LANES), F32),
                            pltpu.SemaphoreType.DMA((2,))],
        ),
        out_shape=jax.ShapeDtypeStruct((n, D_MODEL), F32),
        compiler_params=pltpu.CompilerParams(
            dimension_semantics=("arbitrary",), vmem_limit_bytes=VMEM_LIMIT_BYTES),
        name="moe_combine_ln2",
    )(pos_flat, x1t, route, g2, b2, ys)


def _layer(x2, batch, seq, w_in, na_rpb, dn_conv_w, a_log_f, a_log_b, dt_bias_f, dt_bias_b, dn_norm_w,
           w_proj_na, w_proj_dn, w_out, ln1_g, ln1_b, w_rg, b_rg, w_re, b_re, w_gu, w_dn, ln2_g, ln2_b):
    n_act = 3 * NA_WIDTH + 4 * DN_WIDTH
    n_small = 4 * DN_HEADS
    w_main = jnp.concatenate([w_in[:, n_act + n_small:], w_in[:, :n_act]], 1).astype(BF16)
    ws = w_in[:, n_act:n_act + n_small].reshape(D_MODEL, 4, DN_HEADS // DN_HG, DN_HG)
    w_g = ws.transpose(2, 0, 1, 3).reshape(DN_HEADS // DN_HG, D_MODEL, 4 * DN_HG).astype(BF16)
    w_gt = w_g.transpose(0, 2, 1)

    def per_group(f, b):
        z = jnp.zeros((DN_HEADS // DN_HG, 2 * DN_HG), F32)
        return jnp.concatenate([z, f.reshape(-1, DN_HG), b.reshape(-1, DN_HG)], 1)

    pa, pd = per_group(a_log_f, a_log_b), per_group(dt_bias_f, dt_bias_b)
    prow = jnp.stack([pa, pd], 1)
    pcol = jnp.stack([pa, pd], 2)

    h_all, g_nat, g_t = _proj_call(x2, w_main, w_g, w_gt)
    bias = _na_bias_tables(na_rpb, seq // GRID_W)
    y_na = _na_call(h_all, bias, batch, seq)
    y_dn = _dn_call(h_all, g_nat, g_t, dn_conv_w, prow, pcol, dn_norm_w.reshape(1, DN_HEAD_DIM), batch, seq)

    wr = jnp.zeros((D_MODEL, ROUTER_LANES), F32)
    wr = wr.at[:, :N_GROUPS].set(w_rg).at[:, N_GROUPS:N_GROUPS + N_EXPERTS].set(w_re)
    br = jnp.zeros((1, ROUTER_LANES), F32)
    br = br.at[0, :N_GROUPS].set(b_rg).at[0, N_GROUPS:N_GROUPS + N_EXPERTS].set(b_re)
    x1t, route, counts = _merge_call(x2, y_na, y_dn, h_all, w_proj_na.astype(BF16), w_proj_dn.astype(BF16),
                                     w_out.astype(BF16), ln1_g.reshape(1, -1), ln1_b.reshape(1, -1), wr, br)
    pos, meta = _plan_call(route, counts)
    pos_flat = pos[:TOP_K].reshape(-1)
    ntile = meta[1, :1]
    xs = _dispatch_call(pos_flat, meta[2, N_GROUPS:N_GROUPS + N_EXPERTS], ntile, x1t)
    ys = _experts_call(meta[0, :_moe_tiles(x2.shape[0])], ntile, xs, w_gu, w_dn)
    return _combine_call(pos_flat, x1t, route, ys, ln2_g.reshape(1, -1), ln2_b.reshape(1, -1))


def kernel(x, w_in, na_rpb, dn_conv_w, dn_a_log_f, dn_a_log_b, dn_dt_bias_f, dn_dt_bias_b, dn_norm_w, w_proj_na, w_proj_dn, w_out, ln1_g, ln1_b, w_router_group, b_router_group, w_router_expert, b_router_expert, w_expert_gate_up, w_expert_down, ln2_g, ln2_b):
    batch, seq, d = x.shape
    x2 = x.reshape(batch * seq, d)
    for l in range(w_in.shape[0]):
        x2 = _layer(x2, batch, seq, w_in[l], na_rpb[l], dn_conv_w[l], dn_a_log_f[l], dn_a_log_b[l],
                    dn_dt_bias_f[l], dn_dt_bias_b[l], dn_norm_w[l], w_proj_na[l], w_proj_dn[l], w_out[l],
                    ln1_g[l], ln1_b[l], w_router_group[l], b_router_group[l], w_router_expert[l],
                    b_router_expert[l], w_expert_gate_up[l], w_expert_down[l], ln2_g[l], ln2_b[l])
    return x2.reshape(batch, seq, d)
```

```python
import functools
import itertools

import numpy as np
import jax
import jax.numpy as jnp
from jax import lax
from jax.experimental import pallas as pl
from jax.experimental.pallas import tpu as pltpu

F32 = jnp.float32
BF16 = jnp.bfloat16

D_MODEL = 1024
GRID_W = 64
NA_HEADS = 8
NA_HEAD_DIM = 64
NA_WIN_ROWS = 8
NA_WIN_COLS = 16
NA_WIDTH = NA_HEADS * NA_HEAD_DIM
DN_HEADS = 8
DN_HEAD_DIM = 64
DN_WIDTH = DN_HEADS * DN_HEAD_DIM
DN_CONV = 5
CHUNK = 64
N_GROUPS = 4
EXPERTS_PER_GROUP = 8
N_EXPERTS = N_GROUPS * EXPERTS_PER_GROUP
D_EXPERT = 256
TOP_K = 2
DEPTH = 1
DEEPNORM_ALPHA = (2.0 * DEPTH) ** 0.25
LN_EPS = 1e-5
RMS_EPS = 1e-6

LANES = 128
VMEM_LIMIT_BYTES = 56 * 1024 * 1024

COL_GNA, COL_GDN = 0, 1024
COL_QNA, COL_KNA, COL_VNA = 2048, 2560, 3072
COL_QDN, COL_KDN, COL_VDN, COL_ZDN = 3584, 4096, 4608, 5120
H_COLS = 5632
PROJ_CHUNK = 512

NA_QROWS = 4
NA_KROWS = 12
NA_TQ = NA_QROWS * GRID_W
NA_TK = NA_KROWS * GRID_W
NA_TILES_PER_STEP = 2

DN_HG = 4
DN_GW = DN_HG * DN_HEAD_DIM
DN_INV_BASE = 16
DN_ROW_BLOCK = 128
DN_CONV_HALO = 16
DN_PREP_UNROLL = 4
ROUTER_LANES = 128
NEG_BIG = -1e30
TOK_SUB = D_MODEL // LANES
SUB = 8
MOE_TM = 256


def _sigmoid(x):
    return 1.0 / (1.0 + jnp.exp(-x))


def _silu(x):
    return x * _sigmoid(x)


def _softplus(x):
    return jnp.maximum(x, 0.0) + jnp.log(1.0 + jnp.exp(-jnp.abs(x)))


def _split3(x):
    x1 = x.astype(BF16)
    r1 = x - x1.astype(F32)
    x2 = r1.astype(BF16)
    r2 = r1 - x2.astype(F32)
    return x1, x2, r2.astype(BF16)


def _dot(a, b):
    return jnp.dot(a, b, preferred_element_type=F32)


def _dot_nt(a, b):
    return lax.dot_general(a, b, (((1,), (1,)), ((), ())), preferred_element_type=F32)


def _dot_exact_lhs(a_bf16_exact, x):
    x1, x2, x3 = _split3(x)
    return _dot(a_bf16_exact, x1) + _dot(a_bf16_exact, x2) + _dot(a_bf16_exact, x3)


def _dot_exact_rhs(x, a_bf16_exact):
    x1, x2, x3 = _split3(x)
    return _dot(x1, a_bf16_exact) + _dot(x2, a_bf16_exact) + _dot(x3, a_bf16_exact)


def _dot_f32(a, b):
    a1, a2, _ = _split3(a)
    b1, b2, _ = _split3(b)
    return _dot(a1, b1) + (_dot(a1, b2) + _dot(a2, b1))


def _layer_norm(r, g, b):
    mu = jnp.mean(r, -1, keepdims=True)
    d = r - mu
    var = jnp.mean(d * d, -1, keepdims=True)
    return d * lax.rsqrt(var + LN_EPS) * g + b


def _to_tiles(x):
    x3 = x.reshape(x.shape[0] // SUB, SUB, D_MODEL)
    return [x3[:, :, kk * LANES:(kk + 1) * LANES] for kk in range(TOK_SUB)]


def _from_tiles(ref, lead=()):
    slabs = [ref[lead + (slice(None), kk)] for kk in range(TOK_SUB)]
    x3 = jnp.concatenate(slabs, -1)
    return x3.reshape(x3.shape[0] * SUB, D_MODEL)


def _row_of(ref, row, lead=()):
    hi, lo = row if isinstance(row, tuple) else (lax.shift_right_logical(row, 3), jnp.bitwise_and(row, SUB - 1))
    return ref.at[lead + (hi, slice(None), lo, slice(None))]


def _proj_kernel(x_ref, w_ref, wg_ref, wgt_ref, h_ref, g_ref, gt_ref):
    xb = x_ref[...].astype(BF16)
    for c in range(H_COLS // PROJ_CHUNK):
        cs = slice(c * PROJ_CHUNK, (c + 1) * PROJ_CHUNK)
        h_ref[:, cs] = _dot(xb, w_ref[:, cs]).astype(BF16)
    for hg in range(DN_HEADS // DN_HG):
        g_ref[hg] = _dot(xb, wg_ref[hg])
        gt_ref[hg] = _dot_nt(wgt_ref[hg], xb)


def _proj_call(x2, w_main, w_g, w_gt, tm=512):
    n = x2.shape[0]
    ng = DN_HEADS // DN_HG
    gw = 4 * DN_HG
    return pl.pallas_call(
        _proj_kernel,
        grid=(n // tm,),
        in_specs=[
            pl.BlockSpec((tm, D_MODEL), lambda i: (i, 0)),
            pl.BlockSpec((D_MODEL, H_COLS), lambda i: (0, 0)),
            pl.BlockSpec((ng, D_MODEL, gw), lambda i: (0, 0, 0)),
            pl.BlockSpec((ng, gw, D_MODEL), lambda i: (0, 0, 0)),
        ],
        out_specs=[
            pl.BlockSpec((tm, H_COLS), lambda i: (i, 0)),
            pl.BlockSpec((ng, tm, gw), lambda i: (0, i, 0)),
            pl.BlockSpec((ng, gw, tm), lambda i: (0, 0, i)),
        ],
        out_shape=[
            jax.ShapeDtypeStruct((n, H_COLS), BF16),
            jax.ShapeDtypeStruct((ng, n, gw), F32),
            jax.ShapeDtypeStruct((ng, gw, n), F32),
        ],
        compiler_params=pltpu.CompilerParams(
            dimension_semantics=("arbitrary",), vmem_limit_bytes=VMEM_LIMIT_BYTES),
        name="proj",
    )(x2, w_main, w_g, w_gt)


def _na_bias_tables(rpb, rows):
    kr_win = min(NA_WIN_ROWS, rows)
    ndr, ndc = 2 * NA_WIN_ROWS - 1, 2 * NA_WIN_COLS - 1
    pair = LANES // GRID_W
    c = np.arange(GRID_W)
    col_start = np.clip(c - NA_WIN_COLS // 2, 0, GRID_W - NA_WIN_COLS)
    col_ok = (c[None, :] >= col_start[:, None]) & (c[None, :] < col_start[:, None] + NA_WIN_COLS)
    dc = np.clip(c[None, :] - c[:, None], -(NA_WIN_COLS - 1), NA_WIN_COLS - 1) + (NA_WIN_COLS - 1)
    dc = np.where(col_ok, dc, ndc)
    col_hot = (dc[None] == np.arange(ndc + 1)[:, None, None]).astype(np.float32)
    col_hot2 = np.zeros((pair, ndc + 1, GRID_W, pair, GRID_W), np.float32)
    for u in range(pair):
        col_hot2[u, :, :, u, :] = col_hot
    col_hot2 = col_hot2.reshape(pair, ndc + 1, GRID_W, LANES)
    row_hot = []
    for r0 in (0, 2 * NA_QROWS, rows - NA_QROWS):
        kw0 = int(np.clip(r0 - NA_WIN_ROWS // 2, 0, rows - NA_KROWS))
        qr = r0 + np.arange(NA_QROWS)
        kr = kw0 + np.arange(NA_KROWS)
        row_start = np.clip(qr - kr_win // 2, 0, rows - kr_win)
        row_ok = (kr[None, :] >= row_start[:, None]) & (kr[None, :] < row_start[:, None] + kr_win)
        dr = np.where(row_ok, kr[None, :] - qr[:, None] + (NA_WIN_ROWS - 1), ndr)
        row_hot.append((dr[:, :, None] == np.arange(ndr + 1)).astype(np.float32))
    row_hot = np.stack(row_hot).reshape(3, NA_QROWS, NA_KROWS // pair, pair * (ndr + 1))
    ext = jnp.pad(rpb, ((0, 0), (0, 1), (0, 1)), constant_values=NEG_BIG)
    hi = lax.Precision.HIGHEST
    x = jnp.einsum('hrd,udql->hurql', ext, col_hot2, precision=hi)
    x = x.reshape(rpb.shape[0], pair * (ndr + 1), GRID_W, LANES)
    return jnp.einsum('vimR,hRql->vhimql', row_hot, x, precision=hi)


def _na_kernel(q_ref, k_ref, v_ref, bias_ref, o_ref, *, rows):
    nqt = rows // NA_QROWS
    scale = NA_HEAD_DIM ** -0.5
    probs = []
    for j in range(NA_TILES_PER_STEP):
        qt = pl.program_id(2) * NA_TILES_PER_STEP + j
        kw0 = jnp.clip(qt * NA_QROWS - NA_WIN_ROWS // 2, 0, rows - NA_KROWS) * GRID_W
        kw0 = pl.multiple_of(kw0, GRID_W)
        var = jnp.where(qt == 0, 0, jnp.where(qt == nqt - 1, 2, 1))
        for hh in range(LANES // NA_HEAD_DIM):
            probs.append((j, hh, kw0, var, slice(hh * NA_HEAD_DIM, (hh + 1) * NA_HEAD_DIM)))
    ss = [_dot_nt(q_ref[j * NA_TQ:(j + 1) * NA_TQ, ls], k_ref[pl.ds(kw0, NA_TK), ls])
          for j, hh, kw0, var, ls in probs]
    ps, ls_ = [], []
    for (j, hh, kw0, var, ls), s in zip(probs, ss):
        bias = jnp.concatenate([jnp.concatenate([bias_ref[var, hh, i, m] for m in range(NA_TK // LANES)], 1)
                                for i in range(NA_QROWS)], 0)
        s = s * scale + bias
        p = jnp.exp(s - jnp.max(s, -1, keepdims=True))
        ls_.append(jnp.sum(p, -1, keepdims=True))
        ps.append(p.astype(BF16))
    os_ = [_dot(p, v_ref[pl.ds(kw0, NA_TK), ls]) for (j, hh, kw0, var, ls), p in zip(probs, ps)]
    for (j, hh, kw0, var, ls), o, l in zip(probs, os_, ls_):
        o_ref[j * NA_TQ:(j + 1) * NA_TQ, ls] = (o / l).astype(BF16)


def _na_call(h_all, bias, batch, seq):
    rows = seq // GRID_W
    nst = rows // NA_QROWS // NA_TILES_PER_STEP
    hp = NA_WIDTH // LANES
    n = batch * seq
    tq = NA_TQ * NA_TILES_PER_STEP
    return pl.pallas_call(
        functools.partial(_na_kernel, rows=rows),
        grid=(hp, batch, nst),
        in_specs=[
            pl.BlockSpec((tq, LANES), lambda p, b, t: (b * nst + t, COL_QNA // LANES + p)),
            pl.BlockSpec((seq, LANES), lambda p, b, t: (b, COL_KNA // LANES + p)),
            pl.BlockSpec((seq, LANES), lambda p, b, t: (b, COL_VNA // LANES + p)),
            pl.BlockSpec((3, LANES // NA_HEAD_DIM, NA_QROWS, NA_TK // LANES, GRID_W, LANES),
                         lambda p, b, t: (0, p, 0, 0, 0, 0)),
        ],
        out_specs=pl.BlockSpec((tq, LANES), lambda p, b, t: (b * nst + t, p)),
        out_shape=jax.ShapeDtypeStruct((n, NA_WIDTH), BF16),
        compiler_params=pltpu.CompilerParams(
            dimension_semantics=("arbitrary", "arbitrary", "arbitrary"),
            vmem_limit_bytes=VMEM_LIMIT_BYTES),
        name="natten",
    )(h_all, h_all, h_all, bias)


def _dn_kernel(q_ref, k_ref, v_ref, z_ref, g_ref, gt_ref, cwq_ref, cwk_ref, cwv_ref,
               prow_ref, pcol_ref, nw_ref, o_ref,
               qs, ks, vs, gn_s, gt_s, wq_s, ik_s, kdt_s, u_s, eg_s, st_s, *, seq):
    nchunk = seq // CHUNK
    hd = DN_HEAD_DIM
    npair = DN_HG // 2
    rb = DN_ROW_BLOCK

    lane_p = lax.broadcasted_iota(jnp.int32, (1, LANES), 1)
    first = lane_p < hd

    def per_head(x, fn):
        s0 = jnp.sum(jnp.where(first, x, 0.0), -1, keepdims=True)
        s1 = jnp.sum(jnp.where(first, 0.0, x), -1, keepdims=True)
        return jnp.where(first, fn(s0), fn(s1))

    lane16 = lax.broadcasted_iota(jnp.int32, (1, 4 * DN_HG), 1)
    graw = g_ref[0]
    gl = -jnp.exp(prow_ref[0, 0:1, :]) * _softplus(graw + prow_ref[0, 1:2, :])
    gn_s[...] = jnp.where(lane16 < 2 * DN_HG, _sigmoid(graw), gl)
    sub16 = lax.broadcasted_iota(jnp.int32, (4 * DN_HG, 1), 0)
    grawt = gt_ref[0]
    glt = -jnp.exp(pcol_ref[0, :, 0:1]) * _softplus(grawt + pcol_ref[0, :, 1:2])
    gtt = jnp.where(sub16 < 2 * DN_HG, _sigmoid(grawt), glt)
    for n in range(nchunk):
        gt_s[n] = gtt[:, n * CHUNK:(n + 1) * CHUNK]

    pad = DN_CONV_HALO
    half = DN_CONV // 2
    for src, cw_ref, dst, norm, mul in ((q_ref, cwq_ref, qs, True, hd ** -0.5),
                                        (k_ref, cwk_ref, ks, True, 1.0),
                                        (v_ref, cwv_ref, vs, False, 1.0)):
        cw = cw_ref[...]
        for r0 in range(0, seq, rb):
            lo, hi = r0 - pad, r0 + rb + pad
            parts = []
            if lo < 0:
                parts.append(jnp.zeros((pad, DN_GW), F32))
            parts.append(src[max(lo, 0):min(hi, seq), :].astype(F32))
            if hi > seq:
                parts.append(jnp.zeros((pad, DN_GW), F32))
            xin = jnp.concatenate(parts, 0) if len(parts) > 1 else parts[0]
            nrow = rb + 2 * pad
            y = jnp.zeros((rb, DN_GW), F32)
            for i in range(DN_CONV):
                sh = (half - i) % nrow
                xs = xin if sh == 0 else pltpu.roll(xin, sh, 0)
                y = y + xs[pad:pad + rb, :] * cw[i:i + 1, :]
            y = _silu(y)
            for pp in range(npair):
                seg = y[:, pp * LANES:(pp + 1) * LANES]
                if norm:
                    seg = seg * per_head(seg * seg, lambda s: lax.rsqrt(s + RMS_EPS) * mul)
                dst[pp, r0:r0 + rb, :] = seg.astype(BF16)

    ri = lax.broadcasted_iota(jnp.int32, (CHUNK, LANES), 0)
    ci = lax.broadcasted_iota(jnp.int32, (CHUNK, LANES), 1) & (CHUNK - 1)
    eye = (ri == ci).astype(F32)
    r1 = lax.broadcasted_iota(jnp.int32, (CHUNK, CHUNK), 0)
    c1 = lax.broadcasted_iota(jnp.int32, (CHUNK, CHUNK), 1)
    tril = (r1 >= c1).astype(BF16)
    triu = (r1 <= c1).astype(BF16)
    dirs = ((ri >= ci, ri > ci, CHUNK - 1), (ri <= ci, ri < ci, 0))
    base_sh = DN_INV_BASE.bit_length() - 1
    base_mask = ((ri >> base_sh) == (ci >> base_sh)).astype(F32)
    level_masks = []
    s = DN_INV_BASE
    while s < CHUNK:
        sh = s.bit_length() - 1
        level_masks.append((((ri >> (sh + 1)) == (ci >> (sh + 1))) & ((ri >> sh) != (ci >> sh))).astype(F32))
        s *= 2
    rb2 = lax.broadcasted_iota(jnp.int32, (LANES, LANES), 0)
    cb2 = lax.broadcasted_iota(jnp.int32, (LANES, LANES), 1)
    bd_mask = ((rb2 >> 6) == (cb2 >> 6)).astype(F32)
    bd_mask_b = bd_mask.astype(BF16)

    def bd(x):
        xb = x.astype(BF16)
        return jnp.concatenate([xb, xb], 0) * bd_mask_b

    def mm(a, b):
        return _dot(a.astype(BF16), bd(b))

    ngroup = nchunk // DN_PREP_UNROLL

    def chunk_of(d, i):
        return i if d == 0 else nchunk - 1 - i

    def prep_stages(grp):
        probs = []
        for c in range(DN_PREP_UNROLL):
            for d in range(2):
                n = chunk_of(d, grp * DN_PREP_UNROLL + c)
                r0 = pl.multiple_of(n * CHUNK, CHUNK)
                gcol = gn_s[pl.ds(r0, CHUNK), :]
                grow = gt_s[n]
                cs_col = _dot_exact_lhs(tril if d == 0 else triu, gcol)
                cs_row = _dot_exact_rhs(grow, triu if d == 0 else tril)
                for pp in range(npair):
                    q = qs[pp, pl.ds(r0, CHUNK), :]
                    k = ks[pp, pl.ds(r0, CHUNK), :]
                    v = vs[pp, pl.ds(r0, CHUNK), :]
                    pr = _dot_nt(jnp.concatenate([q, k], 0), bd(k))
                    h0 = 2 * pp
                    cg = 2 * DN_HG + d * DN_HG + h0
                    cbeta = d * DN_HG + h0
                    pair_col = lambda a, c0: jnp.where(first, jnp.broadcast_to(a[:, c0:c0 + 1], (CHUNK, LANES)),
                                                       jnp.broadcast_to(a[:, c0 + 1:c0 + 2], (CHUNK, LANES)))
                    probs.append(dict(
                        n=n, r0=r0, pp=pp, d=d, q=q, k=k, v=v, pr=pr, kt=k.astype(F32).T,
                        cb=pair_col(cs_col, cg), bb=pair_col(gcol, cbeta),
                        crow=jnp.concatenate([cs_row[cg:cg + 1, :], cs_row[cg + 1:cg + 2, :]], 1),
                        crows=(cs_row[cg:cg + 1, :], cs_row[cg + 1:cg + 2, :])))
        yield
        for p in probs:
            incl, strict, last = dirs[p["d"]]
            kf, vf, qf = p["k"].astype(F32), p["v"].astype(F32), p["q"].astype(F32)
            qk, kk = p["pr"][:CHUNK], p["pr"][CHUNK:]
            cb, bb = p["cb"], p["bb"]
            dec = jnp.exp(jnp.where(incl, cb - p["crow"], -jnp.inf))
            p["nmat"] = jnp.where(strict, bb * kk * dec, 0.0)
            p["intra"] = (qk * dec).astype(BF16)
            eb = jnp.exp(cb)
            p["rhs_u"] = (vf * bb).astype(BF16)
            p["rhs_w"] = (kf * (bb * eb)).astype(BF16)
            glast = cb[last:last + 1, :]
            p["qdec"] = (qf * eb).astype(BF16)
            fac = jnp.concatenate([jnp.broadcast_to(jnp.exp(glast[:, hh * hd:hh * hd + 1] - p["crows"][hh]),
                                                    (hd, CHUNK)) for hh in range(2)], 0)
            p["kdect"] = (p["kt"] * fac).astype(BF16)
            p["eg"] = jnp.broadcast_to(jnp.exp(glast), (8, LANES))
            p["nb"] = p["nmat"] * base_mask
        pows = [mm(p["nb"], p["nb"]) for p in probs]
        for p in probs:
            p["t"] = eye - p["nb"]
        yield
        e = 2
        while e < DN_INV_BASE:
            ts = [mm(p["t"], eye + w) for p, w in zip(probs, pows)]
            if 2 * e < DN_INV_BASE:
                pows = [mm(w, w) for w in pows]
            for p, t in zip(probs, ts):
                p["t"] = t
            e *= 2
            yield
        for m in level_masks:
            xs = [mm(p["nmat"] * m, p["t"]) for p in probs]
            yield
            ys = [mm(p["t"], x) for p, x in zip(probs, xs)]
            for p, y in zip(probs, ys):
                p["t"] = p["t"] - y
            yield
        us = [mm(p["t"], p["rhs_u"]) for p in probs]
        ws = [mm(p["t"], p["rhs_w"]) for p in probs]
        yield
        for p, u, w in zip(probs, us, ws):
            d, pp, n = p["d"], p["pp"], p["n"]
            u_s[d, pp, pl.ds(p["r0"], CHUNK), :] = u
            wq_s[d, pp, n, 0:CHUNK, :] = w.astype(BF16)
            wq_s[d, pp, n, CHUNK:2 * CHUNK, :] = p["qdec"]
            ik_s[d, pp, n] = p["intra"]
            kdt_s[d, pp, n] = p["kdect"]
            eg_s[d, pp, n] = p["eg"]

    def scan_stages(grp):
        for c in range(DN_PREP_UNROLL):
            probs = []
            for d in range(2):
                n = chunk_of(d, grp * DN_PREP_UNROLL + c)
                r0 = pl.multiple_of(n * CHUNK, CHUNK)
                for pp in range(npair):
                    probs.append((d, pp, n, r0))
            sts = [st_s[d, pp] for d, pp, n, r0 in probs]
            ts = [_dot(wq_s[d, pp, n], st.astype(BF16))
                  for (d, pp, n, r0), st in zip(probs, sts)]
            yield
            vnews = [u_s[d, pp, pl.ds(r0, CHUNK), :] - t[:CHUNK] for (d, pp, n, r0), t in zip(probs, ts)]
            ois = [_dot(ik_s[d, pp, n], bd(vn)) for (d, pp, n, r0), vn in zip(probs, vnews)]
            dss = [_dot(kdt_s[d, pp, n], vn.astype(BF16)) for (d, pp, n, r0), vn in zip(probs, vnews)]
            yield
            for (d, pp, n, r0), st, t, oi, ds in zip(probs, sts, ts, ois, dss):
                u_s[d, pp, pl.ds(r0, CHUNK), :] = t[CHUNK:] + oi
                st_s[d, pp] = st * eg_s[d, pp, n][0:1, :] + ds * bd_mask

    def interleave(*gens):
        for _ in itertools.zip_longest(*gens):
            pass

    st_s[...] = jnp.zeros_like(st_s)
    interleave(prep_stages(0))

    def pipelined(grp, carry):
        interleave(prep_stages(grp + 1), scan_stages(grp))
        return carry

    lax.fori_loop(0, ngroup - 1, pipelined, 0)
    interleave(scan_stages(ngroup - 1))

    nw = jnp.concatenate([nw_ref[...], nw_ref[...]], 1)
    for r0 in range(0, seq, rb):
        for pp in range(npair):
            o = u_s[0, pp, r0:r0 + rb, :] + u_s[1, pp, r0:r0 + rb, :]
            o = o * per_head(o * o, lambda s: lax.rsqrt(s * (1.0 / hd) + RMS_EPS)) * nw
            zz = z_ref[r0:r0 + rb, pp * LANES:(pp + 1) * LANES].astype(F32)
            o_ref[r0:r0 + rb, pp * LANES:(pp + 1) * LANES] = (o * _silu(zz)).astype(BF16)


def _dn_call(h_all, g_nat, g_t, cw, prow, pcol, norm_w, batch, seq):
    n = batch * seq
    ng = DN_HEADS // DN_HG
    gw = 4 * DN_HG
    nchunk = seq // CHUNK
    npair = DN_HG // 2

    def col(c0):
        return lambda b, g: (b, c0 // DN_GW + g)

    return pl.pallas_call(
        functools.partial(_dn_kernel, seq=seq),
        grid=(batch, ng),
        in_specs=[
            pl.BlockSpec((seq, DN_GW), col(COL_QDN)),
            pl.BlockSpec((seq, DN_GW), col(COL_KDN)),
            pl.BlockSpec((seq, DN_GW), col(COL_VDN)),
            pl.BlockSpec((seq, DN_GW), col(COL_ZDN)),
            pl.BlockSpec((1, seq, gw), lambda b, g: (g, b, 0)),
            pl.BlockSpec((1, gw, seq), lambda b, g: (g, 0, b)),
            pl.BlockSpec((DN_CONV, DN_GW), lambda b, g: (0, g)),
            pl.BlockSpec((DN_CONV, DN_GW), lambda b, g: (0, DN_WIDTH // DN_GW + g)),
            pl.BlockSpec((DN_CONV, DN_GW), lambda b, g: (0, 2 * DN_WIDTH // DN_GW + g)),
            pl.BlockSpec((1, 2, gw), lambda b, g: (g, 0, 0)),
            pl.BlockSpec((1, gw, 2), lambda b, g: (g, 0, 0)),
            pl.BlockSpec((1, DN_HEAD_DIM), lambda b, g: (0, 0)),
        ],
        out_specs=pl.BlockSpec((seq, DN_GW), lambda b, g: (b, g)),
        out_shape=jax.ShapeDtypeStruct((n, DN_WIDTH), BF16),
        scratch_shapes=[
            pltpu.VMEM((npair, seq, LANES), BF16),
            pltpu.VMEM((npair, seq, LANES), BF16),
            pltpu.VMEM((npair, seq, LANES), BF16),
            pltpu.VMEM((seq, gw), F32),
            pltpu.VMEM((nchunk, gw, CHUNK), F32),
            pltpu.VMEM((2, npair, nchunk, 2 * CHUNK, LANES), BF16),
            pltpu.VMEM((2, npair, nchunk, CHUNK, LANES), BF16),
            pltpu.VMEM((2, npair, nchunk, LANES, CHUNK), BF16),
            pltpu.VMEM((2, npair, seq, LANES), F32),
            pltpu.VMEM((2, npair, nchunk, 8, LANES), F32),
            pltpu.VMEM((2, npair, LANES, LANES), F32),
        ],
        compiler_params=pltpu.CompilerParams(
            dimension_semantics=("arbitrary", "arbitrary"), vmem_limit_bytes=VMEM_LIMIT_BYTES),
        name="deltanet",
    )(h_all, h_all, h_all, h_all, g_nat, g_t, cw, cw, cw, prow, pcol, norm_w)


def _merge_kernel(x_ref, yna_ref, ydn_ref, gna_ref, gdn_ref, wpn_ref, wpd_ref, wo_ref,
                  g1_ref, b1_ref, wr_ref, br_ref, x1t_ref, route_ref, cnt_ref, carry_ref):
    a = _dot(yna_ref[...], wpn_ref[...])
    b = _dot(ydn_ref[...], wpd_ref[...])
    merged = _sigmoid(gna_ref[...].astype(F32)) * a + _sigmoid(gdn_ref[...].astype(F32)) * b
    mix = _dot(merged.astype(BF16), wo_ref[...])
    x1 = _layer_norm(DEEPNORM_ALPHA * x_ref[...] + mix, g1_ref[...], b1_ref[...])
    for kk, slab in enumerate(_to_tiles(x1)):
        x1t_ref[:, kk] = slab

    logits = _dot_f32(x1, wr_ref[...]) + br_ref[...]
    tm = logits.shape[0]
    lane = lax.broadcasted_iota(jnp.int32, logits.shape, 1)
    lanef = lane.astype(F32)
    big = float(ROUTER_LANES)
    gmask = lane < N_GROUPS
    gl = jnp.where(gmask, logits, -jnp.inf)
    gmax = jnp.max(gl, -1, keepdims=True)
    gidx = jnp.min(jnp.where(gl == gmax, lanef, big), -1, keepdims=True)
    pg = 1.0 / jnp.sum(jnp.where(gmask, jnp.exp(gl - gmax), 0.0), -1, keepdims=True)
    egrp = jnp.floor((lanef - N_GROUPS) * (1.0 / EXPERTS_PER_GROUP))
    emask = (lane >= N_GROUPS) & (lane < N_GROUPS + N_EXPERTS) & (egrp == gidx)
    el = jnp.where(emask, logits, -jnp.inf)
    m1 = jnp.max(el, -1, keepdims=True)
    i1 = jnp.min(jnp.where(el == m1, lanef, big), -1, keepdims=True)
    el2 = jnp.where(lanef == i1, -jnp.inf, el)
    m2 = jnp.max(el2, -1, keepdims=True)
    i2 = jnp.min(jnp.where(el2 == m2, lanef, big), -1, keepdims=True)
    t = jnp.exp(m2 - m1)
    p1 = pg / (1.0 + t)
    p2 = pg * t / (1.0 + t)

    @pl.when(pl.program_id(0) == 0)
    def _():
        carry_ref[...] = jnp.zeros_like(carry_ref)

    oh1 = (lanef == i1).astype(F32)
    oh2 = (lanef == i2).astype(F32)
    oh = oh1 + oh2
    ri = lax.broadcasted_iota(jnp.int32, (tm, tm), 0)
    ci = lax.broadcasted_iota(jnp.int32, (tm, tm), 1)
    before = _dot((ri > ci).astype(BF16), oh.astype(BF16)) + carry_ref[...]
    rank1 = jnp.sum(before * oh1, -1, keepdims=True)
    rank2 = jnp.sum(before * oh2, -1, keepdims=True)
    carry = carry_ref[...] + jnp.sum(oh, 0, keepdims=True)
    carry_ref[...] = carry
    cnt_ref[...] = jnp.broadcast_to(carry, cnt_ref.shape)
    route = jnp.zeros_like(logits)
    for j, col in enumerate((i1 - N_GROUPS, i2 - N_GROUPS, p1, p2, rank1, rank2)):
        route = jnp.where(lane == j, col, route)
    route_ref[...] = route


def _merge_call(x2, y_na, y_dn, h_all, wpn, wpd, wo, g1, b1, wr, br, tm=512):
    n = x2.shape[0]
    const = lambda i: (0, 0)
    return pl.pallas_call(
        _merge_kernel,
        grid=(n // tm,),
        in_specs=[
            pl.BlockSpec((tm, D_MODEL), lambda i: (i, 0)),
            pl.BlockSpec((tm, NA_WIDTH), lambda i: (i, 0)),
            pl.BlockSpec((tm, DN_WIDTH), lambda i: (i, 0)),
            pl.BlockSpec((tm, D_MODEL), lambda i: (i, COL_GNA // D_MODEL)),
            pl.BlockSpec((tm, D_MODEL), lambda i: (i, COL_GDN // D_MODEL)),
            pl.BlockSpec((NA_WIDTH, D_MODEL), const),
            pl.BlockSpec((DN_WIDTH, D_MODEL), const),
            pl.BlockSpec((D_MODEL, D_MODEL), const),
            pl.BlockSpec((1, D_MODEL), const),
            pl.BlockSpec((1, D_MODEL), const),
            pl.BlockSpec((D_MODEL, ROUTER_LANES), const),
            pl.BlockSpec((1, ROUTER_LANES), const),
        ],
        out_specs=[
            pl.BlockSpec((tm // SUB, TOK_SUB, SUB, LANES), lambda i: (i, 0, 0, 0)),
            pl.BlockSpec((tm, ROUTER_LANES), lambda i: (i, 0)),
            pl.BlockSpec((8, ROUTER_LANES), const),
        ],
        out_shape=[
            jax.ShapeDtypeStruct((n // SUB, TOK_SUB, SUB, LANES), F32),
            jax.ShapeDtypeStruct((n, ROUTER_LANES), F32),
            jax.ShapeDtypeStruct((8, ROUTER_LANES), F32),
        ],
        scratch_shapes=[pltpu.VMEM((1, ROUTER_LANES), F32)],
        compiler_params=pltpu.CompilerParams(
            dimension_semantics=("arbitrary",), vmem_limit_bytes=VMEM_LIMIT_BYTES),
        name="merge_ln1_router",
    )(x2, y_na, y_dn, h_all, h_all, wpn, wpd, wo, g1, b1, wr, br)


def _moe_tiles(n):
    return (TOP_K * n) // MOE_TM + N_EXPERTS


def _plan_kernel(route_ref, cnt_ref, pos_ref, meta_ref):
    cnt_col = cnt_ref[...].T[:, 0:1]
    sub = lax.broadcasted_iota(jnp.int32, (ROUTER_LANES, 1), 0)
    is_e = (sub >= N_GROUPS) & (sub < N_GROUPS + N_EXPERTS)
    padded = jnp.where(is_e, jnp.ceil(cnt_col * (1.0 / MOE_TM)) * MOE_TM, 0.0)
    ri = lax.broadcasted_iota(jnp.int32, (ROUTER_LANES, ROUTER_LANES), 0)
    ci = lax.broadcasted_iota(jnp.int32, (ROUTER_LANES, ROUTER_LANES), 1)
    pb = jnp.broadcast_to(padded, (ROUTER_LANES, ROUTER_LANES))
    off = _dot_exact_lhs((ri > ci).astype(BF16), pb)[:, 0:1]
    end = off + padded

    rt = route_ref[...].T
    tm = rt.shape[1]
    subt = lax.broadcasted_iota(jnp.int32, (ROUTER_LANES, tm), 0).astype(F32) - N_GROUPS
    rows = []
    for k in range(TOP_K):
        e_row, r_row = rt[k:k + 1, :], rt[4 + k:5 + k, :]
        start = jnp.sum(jnp.where(subt == e_row, off, 0.0), 0, keepdims=True)
        rows.append(start + r_row)
    pos = jnp.concatenate(rows + [jnp.zeros((8 - TOP_K, tm), F32)], 0)
    pos_ref[...] = pos.astype(jnp.int32)

    tl = lax.broadcasted_iota(jnp.int32, (ROUTER_LANES, meta_ref.shape[1]), 1).astype(F32) * MOE_TM
    tile_e = jnp.sum(jnp.where(is_e & (end <= tl), 1.0, 0.0), 0, keepdims=True)
    ntile = jnp.max(end, 0, keepdims=True) * (1.0 / MOE_TM)
    last_col = jnp.where(is_e & (padded > 0.0), end * (1.0 / MOE_TM) - 1.0, -1.0)
    last_row = jnp.broadcast_to(last_col, (ROUTER_LANES, ROUTER_LANES)).T[0:1, :]
    last_row = jnp.concatenate([last_row, jnp.full((1, meta_ref.shape[1] - ROUTER_LANES), -1.0, F32)], 1)
    row = lax.broadcasted_iota(jnp.int32, meta_ref.shape, 0)
    meta = jnp.where(row == 0, jnp.minimum(tile_e, N_EXPERTS - 1.0),
                     jnp.where(row == 1, ntile, jnp.where(row == 2, last_row, 0.0)))
    meta_ref[...] = meta.astype(jnp.int32)


def _plan_call(route, counts, tm=512):
    n = route.shape[0]
    mt = -(-_moe_tiles(n) // LANES) * LANES
    return pl.pallas_call(
        _plan_kernel,
        grid=(n // tm,),
        in_specs=[pl.BlockSpec((tm, ROUTER_LANES), lambda i: (i, 0)),
                  pl.BlockSpec((8, ROUTER_LANES), lambda i: (0, 0))],
        out_specs=[pl.BlockSpec((8, tm), lambda i: (0, i)),
                   pl.BlockSpec((8, mt), lambda i: (0, 0))],
        out_shape=[jax.ShapeDtypeStruct((8, n), jnp.int32),
                   jax.ShapeDtypeStruct((8, mt), jnp.int32)],
        compiler_params=pltpu.CompilerParams(dimension_semantics=("arbitrary",)),
        name="moe_plan",
    )(route, counts)


def _row_copy(src, dst, src_row, dst_row, sem, dst_lead=()):
    return pltpu.make_async_copy(_row_of(src, src_row), _row_of(dst, dst_row, dst_lead), sem)


def _dispatch_kernel(pos_ref, last_ref, nt_ref, x1t_ref, xs_hbm, zbuf, sem, zsem, *, n, tb, nt_max):
    tile_rows = MOE_TM // SUB

    def zero_copy(j):
        return pltpu.make_async_copy(zbuf, xs_hbm.at[pl.ds(j * tile_rows, tile_rows)], zsem)

    @pl.when(pl.program_id(0) == 0)
    def _():
        zbuf[...] = jnp.zeros_like(zbuf)
        for e in range(N_EXPERTS):
            @pl.when(last_ref[e] >= 0)
            def _():
                zero_copy(last_ref[e]).start()
        lax.fori_loop(nt_ref[0], nt_max, lambda j, c: (zero_copy(j).start(), c)[1], 0)
        for e in range(N_EXPERTS):
            @pl.when(last_ref[e] >= 0)
            def _():
                zero_copy(0).wait()
        lax.fori_loop(nt_ref[0], nt_max, lambda j, c: (zero_copy(0).wait(), c)[1], 0)

    base = pl.program_id(0) * tb

    def issue(j, c):
        for u in range(SUB):
            for k in range(TOP_K):
                _row_copy(x1t_ref, xs_hbm, (j, u), pos_ref[k * n + base + j * SUB + u], sem).start(priority=k)
        return c

    lax.fori_loop(0, tb // SUB, issue, 0)

    def drain(t, c):
        for k in range(TOP_K):
            _row_copy(x1t_ref, xs_hbm, 0, 0, sem).wait()
        return c

    lax.fori_loop(0, tb, drain, 0, unroll=8)


def _dispatch_call(pos_flat, last_tile, ntile, x1t, tb=2048):
    n = x1t.shape[0] * SUB
    nt_max = _moe_tiles(n)
    return pl.pallas_call(
        functools.partial(_dispatch_kernel, n=n, tb=tb, nt_max=nt_max),
        grid_spec=pltpu.PrefetchScalarGridSpec(
            num_scalar_prefetch=3,
            grid=(n // tb,),
            in_specs=[pl.BlockSpec((tb // SUB, TOK_SUB, SUB, LANES), lambda i, p, l, m: (i, 0, 0, 0))],
            out_specs=pl.BlockSpec(memory_space=pl.ANY),
            scratch_shapes=[pltpu.VMEM((MOE_TM // SUB, TOK_SUB, SUB, LANES), F32),
                            pltpu.SemaphoreType.DMA, pltpu.SemaphoreType.DMA],
        ),
        out_shape=jax.ShapeDtypeStruct((nt_max * MOE_TM // SUB, TOK_SUB, SUB, LANES), F32),
        compiler_params=pltpu.CompilerParams(
            dimension_semantics=("arbitrary",), has_side_effects=True, vmem_limit_bytes=VMEM_LIMIT_BYTES),
        name="moe_dispatch",
    )(pos_flat, last_tile, ntile, x1t)


def _experts_kernel(te_ref, nt_ref, xs_ref, wgu_ref, wd_ref, ys_ref):
    @pl.when(pl.program_id(0) < nt_ref[0])
    def _():
        hgu = _dot(_from_tiles(xs_ref).astype(BF16), wgu_ref[0].astype(BF16))
        hid = _silu(hgu[:, :D_EXPERT]) * hgu[:, D_EXPERT:]
        y = _dot(hid.astype(BF16), wd_ref[0].astype(BF16))
        for kk, slab in enumerate(_to_tiles(y)):
            ys_ref[:, kk] = slab


def _experts_call(tile_e, ntile, xs, wgu, wd):
    nt_max = xs.shape[0] * SUB // MOE_TM

    def tile(j, te, nt):
        return (jnp.minimum(j, nt[0] - 1), 0, 0, 0)

    def expert(j, te, nt):
        return (te[jnp.minimum(j, nt[0] - 1)], 0, 0)

    return pl.pallas_call(
        _experts_kernel,
        grid_spec=pltpu.PrefetchScalarGridSpec(
            num_scalar_prefetch=2,
            grid=(nt_max,),
            in_specs=[pl.BlockSpec((MOE_TM // SUB, TOK_SUB, SUB, LANES), tile),
                      pl.BlockSpec((1, D_MODEL, 2 * D_EXPERT), expert),
                      pl.BlockSpec((1, D_EXPERT, D_MODEL), expert)],
            out_specs=pl.BlockSpec((MOE_TM // SUB, TOK_SUB, SUB, LANES), tile),
        ),
        out_shape=jax.ShapeDtypeStruct(xs.shape, F32),
        input_output_aliases={2: 0},
        compiler_params=pltpu.CompilerParams(
            dimension_semantics=("arbitrary",), vmem_limit_bytes=VMEM_LIMIT_BYTES),
        name="moe_experts",
    )(tile_e, ntile, xs, wgu, wd)


def _combine_kernel(pos_ref, x1t_ref, route_ref, g2_ref, b2_ref, ys_hbm, o_ref, gbuf, sems, *, n, tc):
    i = pl.program_id(0)
    slot = i % 2

    def issue(step, sl):
        def body(j, c):
            for u in range(SUB):
                for k in range(TOP_K):
                    _row_copy(ys_hbm, gbuf, pos_ref[k * n + step * tc + j * SUB + u], (j, u), sems.at[sl],
                              dst_lead=(sl, k)).start(priority=k)
            return c
        lax.fori_loop(0, tc // SUB, body, 0)

    @pl.when(i == 0)
    def _():
        issue(0, 0)

    @pl.when(i + 1 < pl.num_programs(0))
    def _():
        issue(i + 1, 1 - slot)

    def drain(t, c):
        for k in range(TOP_K):
            _row_copy(ys_hbm, gbuf, 0, 0, sems.at[slot], dst_lead=(slot, k)).wait()
        return c

    lax.fori_loop(0, tc, drain, 0, unroll=8)

    route = route_ref[...]
    ffn = route[:, 2:3] * _from_tiles(gbuf, (slot, 0)) + route[:, 3:4] * _from_tiles(gbuf, (slot, 1))
    o_ref[...] = _layer_norm(DEEPNORM_ALPHA * _from_tiles(x1t_ref) + ffn, g2_ref[...], b2_ref[...])


def _combine_call(pos_flat, x1t, route, ys, g2, b2, tc=512):
    n = x1t.shape[0] * SUB
    return pl.pallas_call(
        functools.partial(_combine_kernel, n=n, tc=tc),
        grid_spec=pltpu.PrefetchScalarGridSpec(
            num_scalar_prefetch=1,
            grid=(n // tc,),
            in_specs=[pl.BlockSpec((tc // SUB, TOK_SUB, SUB, LANES), lambda i, p: (i, 0, 0, 0)),
                      pl.BlockSpec((tc, ROUTER_LANES), lambda i, p: (i, 0)),
                      pl.BlockSpec((1, D_MODEL), lambda i, p: (0, 0)),
                      pl.BlockSpec((1, D_MODEL), lambda i, p: (0, 0)),
                      pl.BlockSpec(memory_space=pl.ANY)],
            out_specs=pl.BlockSpec((tc, D_MODEL), lambda i, p: (i, 0)),
            scratch_shapes=[pltpu.VMEM((2, TOP_K, tc // SUB, TOK_SUB, SUB, LANES), F32),
                            pltpu.SemaphoreType.DMA((2,))],
        ),
        out_shape=jax.ShapeDtypeStruct((n, D_MODEL), F32),
        compiler_params=pltpu.CompilerParams(
            dimension_semantics=("arbitrary",), vmem_limit_bytes=VMEM_LIMIT_BYTES),
        name="moe_combine_ln2",
    )(pos_flat, x1t, route, g2, b2, ys)


def _layer(x2, batch, seq, w_in, na_rpb, dn_conv_w, a_log_f, a_log_b, dt_bias_f, dt_bias_b, dn_norm_w,
           w_proj_na, w_proj_dn, w_out, ln1_g, ln1_b, w_rg, b_rg, w_re, b_re, w_gu, w_dn, ln2_g, ln2_b):
    n_act = 3 * NA_WIDTH + 4 * DN_WIDTH
    n_small = 4 * DN_HEADS
    w_main = jnp.concatenate([w_in[:, n_act + n_small:], w_in[:, :n_act]], 1).astype(BF16)
    ws = w_in[:, n_act:n_act + n_small].reshape(D_MODEL, 4, DN_HEADS // DN_HG, DN_HG)
    w_g = ws.transpose(2, 0, 1, 3).reshape(DN_HEADS // DN_HG, D_MODEL, 4 * DN_HG).astype(BF16)
    w_gt = w_g.transpose(0, 2, 1)

    def per_group(f, b):
        z = jnp.zeros((DN_HEADS // DN_HG, 2 * DN_HG), F32)
        return jnp.concatenate([z, f.reshape(-1, DN_HG), b.reshape(-1, DN_HG)], 1)

    pa, pd = per_group(a_log_f, a_log_b), per_group(dt_bias_f, dt_bias_b)
    prow = jnp.stack([pa, pd], 1)
    pcol = jnp.stack([pa, pd], 2)

    h_all, g_nat, g_t = _proj_call(x2, w_main, w_g, w_gt)
    bias = _na_bias_tables(na_rpb, seq // GRID_W)
    y_na = _na_call(h_all, bias, batch, seq)
    y_dn = _dn_call(h_all, g_nat, g_t, dn_conv_w, prow, pcol, dn_norm_w.reshape(1, DN_HEAD_DIM), batch, seq)

    wr = jnp.zeros((D_MODEL, ROUTER_LANES), F32)
    wr = wr.at[:, :N_GROUPS].set(w_rg).at[:, N_GROUPS:N_GROUPS + N_EXPERTS].set(w_re)
    br = jnp.zeros((1, ROUTER_LANES), F32)
    br = br.at[0, :N_GROUPS].set(b_rg).at[0, N_GROUPS:N_GROUPS + N_EXPERTS].set(b_re)
    x1t, route, counts = _merge_call(x2, y_na, y_dn, h_all, w_proj_na.astype(BF16), w_proj_dn.astype(BF16),
                                     w_out.astype(BF16), ln1_g.reshape(1, -1), ln1_b.reshape(1, -1), wr, br)
    pos, meta = _plan_call(route, counts)
    pos_flat = pos[:TOP_K].reshape(-1)
    ntile = meta[1, :1]
    xs = _dispatch_call(pos_flat, meta[2, N_GROUPS:N_GROUPS + N_EXPERTS], ntile, x1t)
    ys = _experts_call(meta[0, :_moe_tiles(x2.shape[0])], ntile, xs, w_gu, w_dn)
    return _combine_call(pos_flat, x1t, route, ys, ln2_g.reshape(1, -1), ln2_b.reshape(1, -1))


def kernel(x, w_in, na_rpb, dn_conv_w, dn_a_log_f, dn_a_log_b, dn_dt_bias_f, dn_dt_bias_b, dn_norm_w, w_proj_na, w_proj_dn, w_out, ln1_g, ln1_b, w_router_group, b_router_group, w_router_expert, b_router_expert, w_expert_gate_up, w_expert_down, ln2_g, ln2_b):
    batch, seq, d = x.shape
    x2 = x.reshape(batch * seq, d)
    for l in range(w_in.shape[0]):
        x2 = _layer(x2, batch, seq, w_in[l], na_rpb[l], dn_conv_w[l], dn_a_log_f[l], dn_a_log_b[l],
                    dn_dt_bias_f[l], dn_dt_bias_b[l], dn_norm_w[l], w_proj_na[l], w_proj_dn[l], w_out[l],
                    ln1_g[l], ln1_b[l], w_router_group[l], b_router_group[l], w_router_expert[l],
                    b_router_expert[l], w_expert_gate_up[l], w_expert_down[l], ln2_g[l], ln2_b[l])
    return x2.reshape(batch, seq, d)
```

```python
import functools
import itertools

import numpy as np
import jax
import jax.numpy as jnp
from jax import lax
from jax.experimental import pallas as pl
from jax.experimental.pallas import tpu as pltpu

F32 = jnp.float32
BF16 = jnp.bfloat16

D_MODEL = 1024
GRID_W = 64
NA_HEADS = 8
NA_HEAD_DIM = 64
NA_WIN_ROWS = 8
NA_WIN_COLS = 16
NA_WIDTH = NA_HEADS * NA_HEAD_DIM
DN_HEADS = 8
DN_HEAD_DIM = 64
DN_WIDTH = DN_HEADS * DN_HEAD_DIM
DN_CONV = 5
CHUNK = 64
N_GROUPS = 4
EXPERTS_PER_GROUP = 8
N_EXPERTS = N_GROUPS * EXPERTS_PER_GROUP
D_EXPERT = 256
TOP_K = 2
DEPTH = 1
DEEPNORM_ALPHA = (2.0 * DEPTH) ** 0.25
LN_EPS = 1e-5
RMS_EPS = 1e-6

LANES = 128
VMEM_LIMIT_BYTES = 56 * 1024 * 1024

COL_GNA, COL_GDN = 0, 1024
COL_QNA, COL_KNA, COL_VNA = 2048, 2560, 3072
COL_QDN, COL_KDN, COL_VDN, COL_ZDN = 3584, 4096, 4608, 5120
H_COLS = 5632
PROJ_CHUNK = 512

NA_QROWS = 4
NA_KROWS = 12
NA_TQ = NA_QROWS * GRID_W
NA_TK = NA_KROWS * GRID_W
NA_TILES_PER_STEP = 4

DN_HG = 4
DN_GW = DN_HG * DN_HEAD_DIM
DN_INV_BASE = 16
DN_ROW_BLOCK = 128
DN_CONV_HALO = 16
DN_PREP_UNROLL = 4
ROUTER_LANES = 128
NEG_BIG = -1e30
TOK_SUB = D_MODEL // LANES
SUB = 8
MOE_TM = 256


def _sigmoid(x):
    return 1.0 / (1.0 + jnp.exp(-x))


def _silu(x):
    return x * _sigmoid(x)


def _softplus(x):
    return jnp.maximum(x, 0.0) + jnp.log(1.0 + jnp.exp(-jnp.abs(x)))


def _split3(x):
    x1 = x.astype(BF16)
    r1 = x - x1.astype(F32)
    x2 = r1.astype(BF16)
    r2 = r1 - x2.astype(F32)
    return x1, x2, r2.astype(BF16)


def _dot(a, b):
    return jnp.dot(a, b, preferred_element_type=F32)


def _dot_nt(a, b):
    return lax.dot_general(a, b, (((1,), (1,)), ((), ())), preferred_element_type=F32)


def _dot_exact_lhs(a_bf16_exact, x):
    x1, x2, x3 = _split3(x)
    return _dot(a_bf16_exact, x1) + _dot(a_bf16_exact, x2) + _dot(a_bf16_exact, x3)


def _dot_exact_rhs(x, a_bf16_exact):
    x1, x2, x3 = _split3(x)
    return _dot(x1, a_bf16_exact) + _dot(x2, a_bf16_exact) + _dot(x3, a_bf16_exact)


def _dot_f32(a, b):
    a1, a2, _ = _split3(a)
    b1, b2, _ = _split3(b)
    return _dot(a1, b1) + (_dot(a1, b2) + _dot(a2, b1))


def _layer_norm(r, g, b):
    mu = jnp.mean(r, -1, keepdims=True)
    d = r - mu
    var = jnp.mean(d * d, -1, keepdims=True)
    return d * lax.rsqrt(var + LN_EPS) * g + b


def _to_tiles(x):
    x3 = x.reshape(x.shape[0] // SUB, SUB, D_MODEL)
    return [x3[:, :, kk * LANES:(kk + 1) * LANES] for kk in range(TOK_SUB)]


def _from_tiles(ref, lead=()):
    slabs = [ref[lead + (slice(None), kk)] for kk in range(TOK_SUB)]
    x3 = jnp.concatenate(slabs, -1)
    return x3.reshape(x3.shape[0] * SUB, D_MODEL)


def _row_of(ref, row, lead=()):
    hi, lo = row if isinstance(row, tuple) else (lax.shift_right_logical(row, 3), jnp.bitwise_and(row, SUB - 1))
    return ref.at[lead + (hi, slice(None), lo, slice(None))]


def _proj_kernel(x_ref, w_ref, wg_ref, wgt_ref, h_ref, g_ref, gt_ref):
    xb = x_ref[...].astype(BF16)
    for c in range(H_COLS // PROJ_CHUNK):
        cs = slice(c * PROJ_CHUNK, (c + 1) * PROJ_CHUNK)
        h_ref[:, cs] = _dot(xb, w_ref[:, cs]).astype(BF16)
    for hg in range(DN_HEADS // DN_HG):
        g_ref[hg] = _dot(xb, wg_ref[hg])
        gt_ref[hg] = _dot_nt(wgt_ref[hg], xb)


def _proj_call(x2, w_main, w_g, w_gt, tm=512):
    n = x2.shape[0]
    ng = DN_HEADS // DN_HG
    gw = 4 * DN_HG
    return pl.pallas_call(
        _proj_kernel,
        grid=(n // tm,),
        in_specs=[
            pl.BlockSpec((tm, D_MODEL), lambda i: (i, 0)),
            pl.BlockSpec((D_MODEL, H_COLS), lambda i: (0, 0)),
            pl.BlockSpec((ng, D_MODEL, gw), lambda i: (0, 0, 0)),
            pl.BlockSpec((ng, gw, D_MODEL), lambda i: (0, 0, 0)),
        ],
        out_specs=[
            pl.BlockSpec((tm, H_COLS), lambda i: (i, 0)),
            pl.BlockSpec((ng, tm, gw), lambda i: (0, i, 0)),
            pl.BlockSpec((ng, gw, tm), lambda i: (0, 0, i)),
        ],
        out_shape=[
            jax.ShapeDtypeStruct((n, H_COLS), BF16),
            jax.ShapeDtypeStruct((ng, n, gw), F32),
            jax.ShapeDtypeStruct((ng, gw, n), F32),
        ],
        compiler_params=pltpu.CompilerParams(
            dimension_semantics=("arbitrary",), vmem_limit_bytes=VMEM_LIMIT_BYTES),
        name="proj",
    )(x2, w_main, w_g, w_gt)


def _na_bias_tables(rpb, rows):
    kr_win = min(NA_WIN_ROWS, rows)
    ndr, ndc = 2 * NA_WIN_ROWS - 1, 2 * NA_WIN_COLS - 1
    pair = LANES // GRID_W
    c = np.arange(GRID_W)
    col_start = np.clip(c - NA_WIN_COLS // 2, 0, GRID_W - NA_WIN_COLS)
    col_ok = (c[None, :] >= col_start[:, None]) & (c[None, :] < col_start[:, None] + NA_WIN_COLS)
    dc = np.clip(c[None, :] - c[:, None], -(NA_WIN_COLS - 1), NA_WIN_COLS - 1) + (NA_WIN_COLS - 1)
    dc = np.where(col_ok, dc, ndc)
    col_hot = (dc[None] == np.arange(ndc + 1)[:, None, None]).astype(np.float32)
    col_hot2 = np.zeros((pair, ndc + 1, GRID_W, pair, GRID_W), np.float32)
    for u in range(pair):
        col_hot2[u, :, :, u, :] = col_hot
    col_hot2 = col_hot2.reshape(pair, ndc + 1, GRID_W, LANES)
    row_hot = []
    for r0 in (0, 2 * NA_QROWS, rows - NA_QROWS):
        kw0 = int(np.clip(r0 - NA_WIN_ROWS // 2, 0, rows - NA_KROWS))
        qr = r0 + np.arange(NA_QROWS)
        kr = kw0 + np.arange(NA_KROWS)
        row_start = np.clip(qr - kr_win // 2, 0, rows - kr_win)
        row_ok = (kr[None, :] >= row_start[:, None]) & (kr[None, :] < row_start[:, None] + kr_win)
        dr = np.where(row_ok, kr[None, :] - qr[:, None] + (NA_WIN_ROWS - 1), ndr)
        row_hot.append((dr[:, :, None] == np.arange(ndr + 1)).astype(np.float32))
    row_hot = np.stack(row_hot).reshape(3, NA_QROWS, NA_KROWS // pair, pair * (ndr + 1))
    ext = jnp.pad(rpb, ((0, 0), (0, 1), (0, 1)), constant_values=NEG_BIG)
    hi = lax.Precision.HIGHEST
    x = jnp.einsum('hrd,udql->hurql', ext, col_hot2, precision=hi)
    x = x.reshape(rpb.shape[0], pair * (ndr + 1), GRID_W, LANES)
    return jnp.einsum('vimR,hRql->vhimql', row_hot, x, precision=hi)


def _na_kernel(q_ref, k_ref, v_ref, bias_ref, o_ref, *, rows):
    nqt = rows // NA_QROWS
    scale = NA_HEAD_DIM ** -0.5
    probs = []
    for j in range(NA_TILES_PER_STEP):
        qt = pl.program_id(2) * NA_TILES_PER_STEP + j
        kw0 = jnp.clip(qt * NA_QROWS - NA_WIN_ROWS // 2, 0, rows - NA_KROWS) * GRID_W
        kw0 = pl.multiple_of(kw0, GRID_W)
        var = jnp.where(qt == 0, 0, jnp.where(qt == nqt - 1, 2, 1))
        for hh in range(LANES // NA_HEAD_DIM):
            probs.append((j, hh, kw0, var, slice(hh * NA_HEAD_DIM, (hh + 1) * NA_HEAD_DIM)))
    ss = [_dot_nt(q_ref[j * NA_TQ:(j + 1) * NA_TQ, ls], k_ref[pl.ds(kw0, NA_TK), ls])
          for j, hh, kw0, var, ls in probs]
    ps, ls_ = [], []
    for (j, hh, kw0, var, ls), s in zip(probs, ss):
        bias = jnp.concatenate([jnp.concatenate([bias_ref[var, hh, i, m] for m in range(NA_TK // LANES)], 1)
                                for i in range(NA_QROWS)], 0)
        s = s * scale + bias
        p = jnp.exp(s - jnp.max(s, -1, keepdims=True))
        ls_.append(jnp.sum(p, -1, keepdims=True))
        ps.append(p.astype(BF16))
    os_ = [_dot(p, v_ref[pl.ds(kw0, NA_TK), ls]) for (j, hh, kw0, var, ls), p in zip(probs, ps)]
    for (j, hh, kw0, var, ls), o, l in zip(probs, os_, ls_):
        o_ref[j * NA_TQ:(j + 1) * NA_TQ, ls] = (o / l).astype(BF16)


def _na_call(h_all, bias, batch, seq):
    rows = seq // GRID_W
    nst = rows // NA_QROWS // NA_TILES_PER_STEP
    hp = NA_WIDTH // LANES
    n = batch * seq
    tq = NA_TQ * NA_TILES_PER_STEP
    return pl.pallas_call(
        functools.partial(_na_kernel, rows=rows),
        grid=(hp, batch, nst),
        in_specs=[
            pl.BlockSpec((tq, LANES), lambda p, b, t: (b * nst + t, COL_QNA // LANES + p)),
            pl.BlockSpec((seq, LANES), lambda p, b, t: (b, COL_KNA // LANES + p)),
            pl.BlockSpec((seq, LANES), lambda p, b, t: (b, COL_VNA // LANES + p)),
            pl.BlockSpec((3, LANES // NA_HEAD_DIM, NA_QROWS, NA_TK // LANES, GRID_W, LANES),
                         lambda p, b, t: (0, p, 0, 0, 0, 0)),
        ],
        out_specs=pl.BlockSpec((tq, LANES), lambda p, b, t: (b * nst + t, p)),
        out_shape=jax.ShapeDtypeStruct((n, NA_WIDTH), BF16),
        compiler_params=pltpu.CompilerParams(
            dimension_semantics=("arbitrary", "arbitrary", "arbitrary"),
            vmem_limit_bytes=VMEM_LIMIT_BYTES),
        name="natten",
    )(h_all, h_all, h_all, bias)


def _dn_kernel(q_ref, k_ref, v_ref, z_ref, g_ref, gt_ref, cwq_ref, cwk_ref, cwv_ref,
               prow_ref, pcol_ref, nw_ref, o_ref,
               qs, ks, vs, gn_s, gt_s, wq_s, ik_s, kdt_s, u_s, eg_s, st_s, *, seq):
    nchunk = seq // CHUNK
    hd = DN_HEAD_DIM
    npair = DN_HG // 2
    rb = DN_ROW_BLOCK

    lane_p = lax.broadcasted_iota(jnp.int32, (1, LANES), 1)
    first = lane_p < hd

    def per_head(x, fn):
        s0 = jnp.sum(jnp.where(first, x, 0.0), -1, keepdims=True)
        s1 = jnp.sum(jnp.where(first, 0.0, x), -1, keepdims=True)
        return jnp.where(first, fn(s0), fn(s1))

    lane16 = lax.broadcasted_iota(jnp.int32, (1, 4 * DN_HG), 1)
    graw = g_ref[0]
    gl = -jnp.exp(prow_ref[0, 0:1, :]) * _softplus(graw + prow_ref[0, 1:2, :])
    gn_s[...] = jnp.where(lane16 < 2 * DN_HG, _sigmoid(graw), gl)
    sub16 = lax.broadcasted_iota(jnp.int32, (4 * DN_HG, 1), 0)
    grawt = gt_ref[0]
    glt = -jnp.exp(pcol_ref[0, :, 0:1]) * _softplus(grawt + pcol_ref[0, :, 1:2])
    gtt = jnp.where(sub16 < 2 * DN_HG, _sigmoid(grawt), glt)
    for n in range(nchunk):
        gt_s[n] = gtt[:, n * CHUNK:(n + 1) * CHUNK]

    pad = DN_CONV_HALO
    half = DN_CONV // 2
    for src, cw_ref, dst, norm, mul in ((q_ref, cwq_ref, qs, True, hd ** -0.5),
                                        (k_ref, cwk_ref, ks, True, 1.0),
                                        (v_ref, cwv_ref, vs, False, 1.0)):
        cw = cw_ref[...]
        for r0 in range(0, seq, rb):
            lo, hi = r0 - pad, r0 + rb + pad
            parts = []
            if lo < 0:
                parts.append(jnp.zeros((pad, DN_GW), F32))
            parts.append(src[max(lo, 0):min(hi, seq), :].astype(F32))
            if hi > seq:
                parts.append(jnp.zeros((pad, DN_GW), F32))
            xin = jnp.concatenate(parts, 0) if len(parts) > 1 else parts[0]
            nrow = rb + 2 * pad
            y = jnp.zeros((rb, DN_GW), F32)
            for i in range(DN_CONV):
                sh = (half - i) % nrow
                xs = xin if sh == 0 else pltpu.roll(xin, sh, 0)
                y = y + xs[pad:pad + rb, :] * cw[i:i + 1, :]
            y = _silu(y)
            for pp in range(npair):
                seg = y[:, pp * LANES:(pp + 1) * LANES]
                if norm:
                    seg = seg * per_head(seg * seg, lambda s: lax.rsqrt(s + RMS_EPS) * mul)
                dst[pp, r0:r0 + rb, :] = seg.astype(BF16)

    ri = lax.broadcasted_iota(jnp.int32, (CHUNK, LANES), 0)
    ci = lax.broadcasted_iota(jnp.int32, (CHUNK, LANES), 1) & (CHUNK - 1)
    eye = (ri == ci).astype(F32)
    r1 = lax.broadcasted_iota(jnp.int32, (CHUNK, CHUNK), 0)
    c1 = lax.broadcasted_iota(jnp.int32, (CHUNK, CHUNK), 1)
    tril = (r1 >= c1).astype(BF16)
    triu = (r1 <= c1).astype(BF16)
    dirs = ((ri >= ci, ri > ci, CHUNK - 1), (ri <= ci, ri < ci, 0))
    base_sh = DN_INV_BASE.bit_length() - 1
    base_mask = ((ri >> base_sh) == (ci >> base_sh)).astype(F32)
    level_masks = []
    s = DN_INV_BASE
    while s < CHUNK:
        sh = s.bit_length() - 1
        level_masks.append((((ri >> (sh + 1)) == (ci >> (sh + 1))) & ((ri >> sh) != (ci >> sh))).astype(F32))
        s *= 2
    rb2 = lax.broadcasted_iota(jnp.int32, (LANES, LANES), 0)
    cb2 = lax.broadcasted_iota(jnp.int32, (LANES, LANES), 1)
    bd_mask = ((rb2 >> 6) == (cb2 >> 6)).astype(F32)
    bd_mask_b = bd_mask.astype(BF16)

    def bd(x):
        xb = x.astype(BF16)
        return jnp.concatenate([xb, xb], 0) * bd_mask_b

    def mm(a, b):
        return _dot(a.astype(BF16), bd(b))

    ngroup = nchunk // DN_PREP_UNROLL

    def chunk_of(d, i):
        return i if d == 0 else nchunk - 1 - i

    def prep_stages(grp):
        probs = []
        for c in range(DN_PREP_UNROLL):
            for d in range(2):
                n = chunk_of(d, grp * DN_PREP_UNROLL + c)
                r0 = pl.multiple_of(n * CHUNK, CHUNK)
                gcol = gn_s[pl.ds(r0, CHUNK), :]
                grow = gt_s[n]
                cs_col = _dot_exact_lhs(tril if d == 0 else triu, gcol)
                cs_row = _dot_exact_rhs(grow, triu if d == 0 else tril)
                for pp in range(npair):
                    q = qs[pp, pl.ds(r0, CHUNK), :]
                    k = ks[pp, pl.ds(r0, CHUNK), :]
                    v = vs[pp, pl.ds(r0, CHUNK), :]
                    pr = _dot_nt(jnp.concatenate([q, k], 0), bd(k))
                    h0 = 2 * pp
                    cg = 2 * DN_HG + d * DN_HG + h0
                    cbeta = d * DN_HG + h0
                    pair_col = lambda a, c0: jnp.where(first, jnp.broadcast_to(a[:, c0:c0 + 1], (CHUNK, LANES)),
                                                       jnp.broadcast_to(a[:, c0 + 1:c0 + 2], (CHUNK, LANES)))
                    probs.append(dict(
                        n=n, r0=r0, pp=pp, d=d, q=q, k=k, v=v, pr=pr, kt=k.astype(F32).T,
                        cb=pair_col(cs_col, cg), bb=pair_col(gcol, cbeta),
                        crow=jnp.concatenate([cs_row[cg:cg + 1, :], cs_row[cg + 1:cg + 2, :]], 1),
                        crows=(cs_row[cg:cg + 1, :], cs_row[cg + 1:cg + 2, :])))
        yield
        for p in probs:
            incl, strict, last = dirs[p["d"]]
            kf, vf, qf = p["k"].astype(F32), p["v"].astype(F32), p["q"].astype(F32)
            qk, kk = p["pr"][:CHUNK], p["pr"][CHUNK:]
            cb, bb = p["cb"], p["bb"]
            dec = jnp.exp(jnp.where(incl, cb - p["crow"], -jnp.inf))
            p["nmat"] = jnp.where(strict, bb * kk * dec, 0.0)
            p["intra"] = (qk * dec).astype(BF16)
            eb = jnp.exp(cb)
            p["rhs_u"] = (vf * bb).astype(BF16)
            p["rhs_w"] = (kf * (bb * eb)).astype(BF16)
            glast = cb[last:last + 1, :]
            p["qdec"] = (qf * eb).astype(BF16)
            fac = jnp.concatenate([jnp.broadcast_to(jnp.exp(glast[:, hh * hd:hh * hd + 1] - p["crows"][hh]),
                                                    (hd, CHUNK)) for hh in range(2)], 0)
            p["kdect"] = (p["kt"] * fac).astype(BF16)
            p["eg"] = jnp.broadcast_to(jnp.exp(glast), (8, LANES))
            p["nb"] = p["nmat"] * base_mask
        pows = [mm(p["nb"], p["nb"]) for p in probs]
        for p in probs:
            p["t"] = eye - p["nb"]
        yield
        e = 2
        while e < DN_INV_BASE:
            ts = [mm(p["t"], eye + w) for p, w in zip(probs, pows)]
            if 2 * e < DN_INV_BASE:
                pows = [mm(w, w) for w in pows]
            for p, t in zip(probs, ts):
                p["t"] = t
            e *= 2
            yield
        for m in level_masks:
            xs = [mm(p["nmat"] * m, p["t"]) for p in probs]
            yield
            ys = [mm(p["t"], x) for p, x in zip(probs, xs)]
            for p, y in zip(probs, ys):
                p["t"] = p["t"] - y
            yield
        us = [mm(p["t"], p["rhs_u"]) for p in probs]
        ws = [mm(p["t"], p["rhs_w"]) for p in probs]
        yield
        for p, u, w in zip(probs, us, ws):
            d, pp, n = p["d"], p["pp"], p["n"]
            u_s[d, pp, pl.ds(p["r0"], CHUNK), :] = u
            wq_s[d, pp, n, 0:CHUNK, :] = w.astype(BF16)
            wq_s[d, pp, n, CHUNK:2 * CHUNK, :] = p["qdec"]
            ik_s[d, pp, n] = p["intra"]
            kdt_s[d, pp, n] = p["kdect"]
            eg_s[d, pp, n] = p["eg"]

    def scan_stages(grp):
        for c in range(DN_PREP_UNROLL):
            probs = []
            for d in range(2):
                n = chunk_of(d, grp * DN_PREP_UNROLL + c)
                r0 = pl.multiple_of(n * CHUNK, CHUNK)
                for pp in range(npair):
                    probs.append((d, pp, n, r0))
            sts = [st_s[d, pp] for d, pp, n, r0 in probs]
            ts = [_dot(wq_s[d, pp, n], st.astype(BF16))
                  for (d, pp, n, r0), st in zip(probs, sts)]
            yield
            vnews = [u_s[d, pp, pl.ds(r0, CHUNK), :] - t[:CHUNK] for (d, pp, n, r0), t in zip(probs, ts)]
            ois = [_dot(ik_s[d, pp, n], bd(vn)) for (d, pp, n, r0), vn in zip(probs, vnews)]
            dss = [_dot(kdt_s[d, pp, n], vn.astype(BF16)) for (d, pp, n, r0), vn in zip(probs, vnews)]
            yield
            for (d, pp, n, r0), st, t, oi, ds in zip(probs, sts, ts, ois, dss):
                u_s[d, pp, pl.ds(r0, CHUNK), :] = t[CHUNK:] + oi
                st_s[d, pp] = st * eg_s[d, pp, n][0:1, :] + ds * bd_mask

    def interleave(*gens):
        for _ in itertools.zip_longest(*gens):
            pass

    st_s[...] = jnp.zeros_like(st_s)
    interleave(prep_stages(0))

    def pipelined(grp, carry):
        interleave(prep_stages(grp + 1), scan_stages(grp))
        return carry

    lax.fori_loop(0, ngroup - 1, pipelined, 0)
    interleave(scan_stages(ngroup - 1))

    nw = jnp.concatenate([nw_ref[...], nw_ref[...]], 1)
    for r0 in range(0, seq, rb):
        for pp in range(npair):
            o = u_s[0, pp, r0:r0 + rb, :] + u_s[1, pp, r0:r0 + rb, :]
            o = o * per_head(o * o, lambda s: lax.rsqrt(s * (1.0 / hd) + RMS_EPS)) * nw
            zz = z_ref[r0:r0 + rb, pp * LANES:(pp + 1) * LANES].astype(F32)
            o_ref[r0:r0 + rb, pp * LANES:(pp + 1) * LANES] = (o * _silu(zz)).astype(BF16)


def _dn_call(h_all, g_nat, g_t, cw, prow, pcol, norm_w, batch, seq):
    n = batch * seq
    ng = DN_HEADS // DN_HG
    gw = 4 * DN_HG
    nchunk = seq // CHUNK
    npair = DN_HG // 2

    def col(c0):
        return lambda b, g: (b, c0 // DN_GW + g)

    return pl.pallas_call(
        functools.partial(_dn_kernel, seq=seq),
        grid=(batch, ng),
        in_specs=[
            pl.BlockSpec((seq, DN_GW), col(COL_QDN)),
            pl.BlockSpec((seq, DN_GW), col(COL_KDN)),
            pl.BlockSpec((seq, DN_GW), col(COL_VDN)),
            pl.BlockSpec((seq, DN_GW), col(COL_ZDN)),
            pl.BlockSpec((1, seq, gw), lambda b, g: (g, b, 0)),
            pl.BlockSpec((1, gw, seq), lambda b, g: (g, 0, b)),
            pl.BlockSpec((DN_CONV, DN_GW), lambda b, g: (0, g)),
            pl.BlockSpec((DN_CONV, DN_GW), lambda b, g: (0, DN_WIDTH // DN_GW + g)),
            pl.BlockSpec((DN_CONV, DN_GW), lambda b, g: (0, 2 * DN_WIDTH // DN_GW + g)),
            pl.BlockSpec((1, 2, gw), lambda b, g: (g, 0, 0)),
            pl.BlockSpec((1, gw, 2), lambda b, g: (g, 0, 0)),
            pl.BlockSpec((1, DN_HEAD_DIM), lambda b, g: (0, 0)),
        ],
        out_specs=pl.BlockSpec((seq, DN_GW), lambda b, g: (b, g)),
        out_shape=jax.ShapeDtypeStruct((n, DN_WIDTH), BF16),
        scratch_shapes=[
            pltpu.VMEM((npair, seq, LANES), BF16),
            pltpu.VMEM((npair, seq, LANES), BF16),
            pltpu.VMEM((npair, seq, LANES), BF16),
            pltpu.VMEM((seq, gw), F32),
            pltpu.VMEM((nchunk, gw, CHUNK), F32),
            pltpu.VMEM((2, npair, nchunk, 2 * CHUNK, LANES), BF16),
            pltpu.VMEM((2, npair, nchunk, CHUNK, LANES), BF16),
            pltpu.VMEM((2, npair, nchunk, LANES, CHUNK), BF16),
            pltpu.VMEM((2, npair, seq, LANES), F32),
            pltpu.VMEM((2, npair, nchunk, 8, LANES), F32),
            pltpu.VMEM((2, npair, LANES, LANES), F32),
        ],
        compiler_params=pltpu.CompilerParams(
            dimension_semantics=("arbitrary", "arbitrary"), vmem_limit_bytes=VMEM_LIMIT_BYTES),
        name="deltanet",
    )(h_all, h_all, h_all, h_all, g_nat, g_t, cw, cw, cw, prow, pcol, norm_w)


def _merge_kernel(x_ref, yna_ref, ydn_ref, gna_ref, gdn_ref, wpn_ref, wpd_ref, wo_ref,
                  g1_ref, b1_ref, wr_ref, br_ref, x1t_ref, route_ref, cnt_ref, carry_ref):
    a = _dot(yna_ref[...], wpn_ref[...])
    b = _dot(ydn_ref[...], wpd_ref[...])
    merged = _sigmoid(gna_ref[...].astype(F32)) * a + _sigmoid(gdn_ref[...].astype(F32)) * b
    mix = _dot(merged.astype(BF16), wo_ref[...])
    x1 = _layer_norm(DEEPNORM_ALPHA * x_ref[...] + mix, g1_ref[...], b1_ref[...])
    for kk, slab in enumerate(_to_tiles(x1)):
        x1t_ref[:, kk] = slab

    logits = _dot_f32(x1, wr_ref[...]) + br_ref[...]
    tm = logits.shape[0]
    lane = lax.broadcasted_iota(jnp.int32, logits.shape, 1)
    lanef = lane.astype(F32)
    big = float(ROUTER_LANES)
    gmask = lane < N_GROUPS
    gl = jnp.where(gmask, logits, -jnp.inf)
    gmax = jnp.max(gl, -1, keepdims=True)
    gidx = jnp.min(jnp.where(gl == gmax, lanef, big), -1, keepdims=True)
    pg = 1.0 / jnp.sum(jnp.where(gmask, jnp.exp(gl - gmax), 0.0), -1, keepdims=True)
    egrp = jnp.floor((lanef - N_GROUPS) * (1.0 / EXPERTS_PER_GROUP))
    emask = (lane >= N_GROUPS) & (lane < N_GROUPS + N_EXPERTS) & (egrp == gidx)
    el = jnp.where(emask, logits, -jnp.inf)
    m1 = jnp.max(el, -1, keepdims=True)
    i1 = jnp.min(jnp.where(el == m1, lanef, big), -1, keepdims=True)
    el2 = jnp.where(lanef == i1, -jnp.inf, el)
    m2 = jnp.max(el2, -1, keepdims=True)
    i2 = jnp.min(jnp.where(el2 == m2, lanef, big), -1, keepdims=True)
    t = jnp.exp(m2 - m1)
    p1 = pg / (1.0 + t)
    p2 = pg * t / (1.0 + t)

    @pl.when(pl.program_id(0) == 0)
    def _():
        carry_ref[...] = jnp.zeros_like(carry_ref)

    oh1 = (lanef == i1).astype(F32)
    oh2 = (lanef == i2).astype(F32)
    oh = oh1 + oh2
    ri = lax.broadcasted_iota(jnp.int32, (tm, tm), 0)
    ci = lax.broadcasted_iota(jnp.int32, (tm, tm), 1)
    before = _dot((ri > ci).astype(BF16), oh.astype(BF16)) + carry_ref[...]
    rank1 = jnp.sum(before * oh1, -1, keepdims=True)
    rank2 = jnp.sum(before * oh2, -1, keepdims=True)
    carry = carry_ref[...] + jnp.sum(oh, 0, keepdims=True)
    carry_ref[...] = carry
    cnt_ref[...] = jnp.broadcast_to(carry, cnt_ref.shape)
    route = jnp.zeros_like(logits)
    for j, col in enumerate((i1 - N_GROUPS, i2 - N_GROUPS, p1, p2, rank1, rank2)):
        route = jnp.where(lane == j, col, route)
    route_ref[...] = route


def _merge_call(x2, y_na, y_dn, h_all, wpn, wpd, wo, g1, b1, wr, br, tm=512):
    n = x2.shape[0]
    const = lambda i: (0, 0)
    return pl.pallas_call(
        _merge_kernel,
        grid=(n // tm,),
        in_specs=[
            pl.BlockSpec((tm, D_MODEL), lambda i: (i, 0)),
            pl.BlockSpec((tm, NA_WIDTH), lambda i: (i, 0)),
            pl.BlockSpec((tm, DN_WIDTH), lambda i: (i, 0)),
            pl.BlockSpec((tm, D_MODEL), lambda i: (i, COL_GNA // D_MODEL)),
            pl.BlockSpec((tm, D_MODEL), lambda i: (i, COL_GDN // D_MODEL)),
            pl.BlockSpec((NA_WIDTH, D_MODEL), const),
            pl.BlockSpec((DN_WIDTH, D_MODEL), const),
            pl.BlockSpec((D_MODEL, D_MODEL), const),
            pl.BlockSpec((1, D_MODEL), const),
            pl.BlockSpec((1, D_MODEL), const),
            pl.BlockSpec((D_MODEL, ROUTER_LANES), const),
            pl.BlockSpec((1, ROUTER_LANES), const),
        ],
        out_specs=[
            pl.BlockSpec((tm // SUB, TOK_SUB, SUB, LANES), lambda i: (i, 0, 0, 0)),
            pl.BlockSpec((tm, ROUTER_LANES), lambda i: (i, 0)),
            pl.BlockSpec((8, ROUTER_LANES), const),
        ],
        out_shape=[
            jax.ShapeDtypeStruct((n // SUB, TOK_SUB, SUB, LANES), F32),
            jax.ShapeDtypeStruct((n, ROUTER_LANES), F32),
            jax.ShapeDtypeStruct((8, ROUTER_LANES), F32),
        ],
        scratch_shapes=[pltpu.VMEM((1, ROUTER_LANES), F32)],
        compiler_params=pltpu.CompilerParams(
            dimension_semantics=("arbitrary",), vmem_limit_bytes=VMEM_LIMIT_BYTES),
        name="merge_ln1_router",
    )(x2, y_na, y_dn, h_all, h_all, wpn, wpd, wo, g1, b1, wr, br)


def _moe_tiles(n):
    return (TOP_K * n) // MOE_TM + N_EXPERTS


def _plan_kernel(route_ref, cnt_ref, pos_ref, meta_ref):
    cnt_col = cnt_ref[...].T[:, 0:1]
    sub = lax.broadcasted_iota(jnp.int32, (ROUTER_LANES, 1), 0)
    is_e = (sub >= N_GROUPS) & (sub < N_GROUPS + N_EXPERTS)
    padded = jnp.where(is_e, jnp.ceil(cnt_col * (1.0 / MOE_TM)) * MOE_TM, 0.0)
    ri = lax.broadcasted_iota(jnp.int32, (ROUTER_LANES, ROUTER_LANES), 0)
    ci = lax.broadcasted_iota(jnp.int32, (ROUTER_LANES, ROUTER_LANES), 1)
    pb = jnp.broadcast_to(padded, (ROUTER_LANES, ROUTER_LANES))
    off = _dot_exact_lhs((ri > ci).astype(BF16), pb)[:, 0:1]
    end = off + padded

    rt = route_ref[...].T
    tm = rt.shape[1]
    subt = lax.broadcasted_iota(jnp.int32, (ROUTER_LANES, tm), 0).astype(F32) - N_GROUPS
    rows = []
    for k in range(TOP_K):
        e_row, r_row = rt[k:k + 1, :], rt[4 + k:5 + k, :]
        start = jnp.sum(jnp.where(subt == e_row, off, 0.0), 0, keepdims=True)
        rows.append(start + r_row)
    his = [jnp.floor(r * (1.0 / SUB)) for r in rows]
    los = [r - SUB * h for r, h in zip(rows, his)]
    pos = jnp.concatenate(rows + his + los + [jnp.zeros((8 - 3 * TOP_K, tm), F32)], 0)
    pos_ref[...] = pos.astype(jnp.int32)

    tl = lax.broadcasted_iota(jnp.int32, (ROUTER_LANES, meta_ref.shape[1]), 1).astype(F32) * MOE_TM
    tile_e = jnp.sum(jnp.where(is_e & (end <= tl), 1.0, 0.0), 0, keepdims=True)
    ntile = jnp.max(end, 0, keepdims=True) * (1.0 / MOE_TM)
    last_col = jnp.where(is_e & (padded > 0.0), end * (1.0 / MOE_TM) - 1.0, -1.0)
    last_row = jnp.broadcast_to(last_col, (ROUTER_LANES, ROUTER_LANES)).T[0:1, :]
    last_row = jnp.concatenate([last_row, jnp.full((1, meta_ref.shape[1] - ROUTER_LANES), -1.0, F32)], 1)
    row = lax.broadcasted_iota(jnp.int32, meta_ref.shape, 0)
    meta = jnp.where(row == 0, jnp.minimum(tile_e, N_EXPERTS - 1.0),
                     jnp.where(row == 1, ntile, jnp.where(row == 2, last_row, 0.0)))
    meta_ref[...] = meta.astype(jnp.int32)


def _plan_call(route, counts, tm=512):
    n = route.shape[0]
    mt = -(-_moe_tiles(n) // LANES) * LANES
    return pl.pallas_call(
        _plan_kernel,
        grid=(n // tm,),
        in_specs=[pl.BlockSpec((tm, ROUTER_LANES), lambda i: (i, 0)),
                  pl.BlockSpec((8, ROUTER_LANES), lambda i: (0, 0))],
        out_specs=[pl.BlockSpec((8, tm), lambda i: (0, i)),
                   pl.BlockSpec((8, mt), lambda i: (0, 0))],
        out_shape=[jax.ShapeDtypeStruct((8, n), jnp.int32),
                   jax.ShapeDtypeStruct((8, mt), jnp.int32)],
        compiler_params=pltpu.CompilerParams(dimension_semantics=("arbitrary",)),
        name="moe_plan",
    )(route, counts)


def _row_copy(src, dst, src_row, dst_row, sem, dst_lead=()):
    return pltpu.make_async_copy(_row_of(src, src_row), _row_of(dst, dst_row, dst_lead), sem)


def _dispatch_kernel(hi_ref, lo_ref, last_ref, nt_ref, x1t_ref, xs_hbm, zbuf, sem, zsem, *, n, tb, nt_max):
    tile_rows = MOE_TM // SUB

    def zero_copy(j):
        return pltpu.make_async_copy(zbuf, xs_hbm.at[pl.ds(j * tile_rows, tile_rows)], zsem)

    @pl.when(pl.program_id(0) == 0)
    def _():
        zbuf[...] = jnp.zeros_like(zbuf)
        for e in range(N_EXPERTS):
            @pl.when(last_ref[e] >= 0)
            def _():
                zero_copy(last_ref[e]).start()
        lax.fori_loop(nt_ref[0], nt_max, lambda j, c: (zero_copy(j).start(), c)[1], 0)
        for e in range(N_EXPERTS):
            @pl.when(last_ref[e] >= 0)
            def _():
                zero_copy(0).wait()
        lax.fori_loop(nt_ref[0], nt_max, lambda j, c: (zero_copy(0).wait(), c)[1], 0)

    base = pl.program_id(0) * tb

    def issue(j, c):
        for u in range(SUB):
            for k in range(TOP_K):
                a = k * n + base + j * SUB + u
                _row_copy(x1t_ref, xs_hbm, (j, u), (hi_ref[a], lo_ref[a]), sem).start(priority=k)
        return c

    lax.fori_loop(0, tb // SUB, issue, 0)

    def drain(t, c):
        for k in range(TOP_K):
            _row_copy(x1t_ref, xs_hbm, 0, 0, sem).wait()
        return c

    lax.fori_loop(0, tb, drain, 0, unroll=8)


def _dispatch_call(pos_hi, pos_lo, last_tile, ntile, x1t, tb=2048):
    n = x1t.shape[0] * SUB
    nt_max = _moe_tiles(n)
    return pl.pallas_call(
        functools.partial(_dispatch_kernel, n=n, tb=tb, nt_max=nt_max),
        grid_spec=pltpu.PrefetchScalarGridSpec(
            num_scalar_prefetch=4,
            grid=(n // tb,),
            in_specs=[pl.BlockSpec((tb // SUB, TOK_SUB, SUB, LANES), lambda i, h, o, l, m: (i, 0, 0, 0))],
            out_specs=pl.BlockSpec(memory_space=pl.ANY),
            scratch_shapes=[pltpu.VMEM((MOE_TM // SUB, TOK_SUB, SUB, LANES), F32),
                            pltpu.SemaphoreType.DMA, pltpu.SemaphoreType.DMA],
        ),
        out_shape=jax.ShapeDtypeStruct((nt_max * MOE_TM // SUB, TOK_SUB, SUB, LANES), F32),
        compiler_params=pltpu.CompilerParams(
            dimension_semantics=("arbitrary",), has_side_effects=True, vmem_limit_bytes=VMEM_LIMIT_BYTES),
        name="moe_dispatch",
    )(pos_hi, pos_lo, last_tile, ntile, x1t)


def _experts_kernel(te_ref, nt_ref, xs_ref, wgu_ref, wd_ref, ys_ref):
    @pl.when(pl.program_id(0) < nt_ref[0])
    def _():
        hgu = _dot(_from_tiles(xs_ref).astype(BF16), wgu_ref[0].astype(BF16))
        hid = _silu(hgu[:, :D_EXPERT]) * hgu[:, D_EXPERT:]
        y = _dot(hid.astype(BF16), wd_ref[0].astype(BF16))
        for kk, slab in enumerate(_to_tiles(y)):
            ys_ref[:, kk] = slab


def _experts_call(tile_e, ntile, xs, wgu, wd):
    nt_max = xs.shape[0] * SUB // MOE_TM

    def tile(j, te, nt):
        return (jnp.minimum(j, nt[0] - 1), 0, 0, 0)

    def expert(j, te, nt):
        return (te[jnp.minimum(j, nt[0] - 1)], 0, 0)

    return pl.pallas_call(
        _experts_kernel,
        grid_spec=pltpu.PrefetchScalarGridSpec(
            num_scalar_prefetch=2,
            grid=(nt_max,),
            in_specs=[pl.BlockSpec((MOE_TM // SUB, TOK_SUB, SUB, LANES), tile),
                      pl.BlockSpec((1, D_MODEL, 2 * D_EXPERT), expert),
                      pl.BlockSpec((1, D_EXPERT, D_MODEL), expert)],
            out_specs=pl.BlockSpec((MOE_TM // SUB, TOK_SUB, SUB, LANES), tile),
        ),
        out_shape=jax.ShapeDtypeStruct(xs.shape, F32),
        input_output_aliases={2: 0},
        compiler_params=pltpu.CompilerParams(
            dimension_semantics=("arbitrary",), vmem_limit_bytes=VMEM_LIMIT_BYTES),
        name="moe_experts",
    )(tile_e, ntile, xs, wgu, wd)


def _combine_kernel(hi_ref, lo_ref, x1t_ref, route_ref, g2_ref, b2_ref, ys_hbm, o_ref, gbuf, sems, *, n, tc):
    i = pl.program_id(0)
    slot = i % 2

    def issue(step, sl):
        def body(j, c):
            for u in range(SUB):
                for k in range(TOP_K):
                    a = k * n + step * tc + j * SUB + u
                    _row_copy(ys_hbm, gbuf, (hi_ref[a], lo_ref[a]), (j, u), sems.at[sl],
                              dst_lead=(sl, k)).start(priority=k)
            return c
        lax.fori_loop(0, tc // SUB, body, 0)

    @pl.when(i == 0)
    def _():
        issue(0, 0)

    @pl.when(i + 1 < pl.num_programs(0))
    def _():
        issue(i + 1, 1 - slot)

    def drain(t, c):
        for k in range(TOP_K):
            _row_copy(ys_hbm, gbuf, 0, 0, sems.at[slot], dst_lead=(slot, k)).wait()
        return c

    lax.fori_loop(0, tc, drain, 0, unroll=8)

    route = route_ref[...]
    ffn = route[:, 2:3] * _from_tiles(gbuf, (slot, 0)) + route[:, 3:4] * _from_tiles(gbuf, (slot, 1))
    o_ref[...] = _layer_norm(DEEPNORM_ALPHA * _from_tiles(x1t_ref) + ffn, g2_ref[...], b2_ref[...])


def _combine_call(pos_hi, pos_lo, x1t, route, ys, g2, b2, tc=512):
    n = x1t.shape[0] * SUB
    return pl.pallas_call(
        functools.partial(_combine_kernel, n=n, tc=tc),
        grid_spec=pltpu.PrefetchScalarGridSpec(
            num_scalar_prefetch=2,
            grid=(n // tc,),
            in_specs=[pl.BlockSpec((tc // SUB, TOK_SUB, SUB, LANES), lambda i, h, o: (i, 0, 0, 0)),
                      pl.BlockSpec((tc, ROUTER_LANES), lambda i, h, o: (i, 0)),
                      pl.BlockSpec((1, D_MODEL), lambda i, h, o: (0, 0)),
                      pl.BlockSpec((1, D_MODEL), lambda i, h, o: (0, 0)),
                      pl.BlockSpec(memory_space=pl.ANY)],
            out_specs=pl.BlockSpec((tc, D_MODEL), lambda i, h, o: (i, 0)),
            scratch_shapes=[pltpu.VMEM((2, TOP_K, tc // SUB, TOK_SUB, SUB, LANES), F32),
                            pltpu.SemaphoreType.DMA((2,))],
        ),
        out_shape=jax.ShapeDtypeStruct((n, D_MODEL), F32),
        compiler_params=pltpu.CompilerParams(
            dimension_semantics=("arbitrary",), vmem_limit_bytes=VMEM_LIMIT_BYTES),
        name="moe_combine_ln2",
    )(pos_hi, pos_lo, x1t, route, g2, b2, ys)


def _layer(x2, batch, seq, w_in, na_rpb, dn_conv_w, a_log_f, a_log_b, dt_bias_f, dt_bias_b, dn_norm_w,
           w_proj_na, w_proj_dn, w_out, ln1_g, ln1_b, w_rg, b_rg, w_re, b_re, w_gu, w_dn, ln2_g, ln2_b):
    n_act = 3 * NA_WIDTH + 4 * DN_WIDTH
    n_small = 4 * DN_HEADS
    w_main = jnp.concatenate([w_in[:, n_act + n_small:], w_in[:, :n_act]], 1).astype(BF16)
    ws = w_in[:, n_act:n_act + n_small].reshape(D_MODEL, 4, DN_HEADS // DN_HG, DN_HG)
    w_g = ws.transpose(2, 0, 1, 3).reshape(DN_HEADS // DN_HG, D_MODEL, 4 * DN_HG).astype(BF16)
    w_gt = w_g.transpose(0, 2, 1)

    def per_group(f, b):
        z = jnp.zeros((DN_HEADS // DN_HG, 2 * DN_HG), F32)
        return jnp.concatenate([z, f.reshape(-1, DN_HG), b.reshape(-1, DN_HG)], 1)

    pa, pd = per_group(a_log_f, a_log_b), per_group(dt_bias_f, dt_bias_b)
    prow = jnp.stack([pa, pd], 1)
    pcol = jnp.stack([pa, pd], 2)

    h_all, g_nat, g_t = _proj_call(x2, w_main, w_g, w_gt)
    bias = _na_bias_tables(na_rpb, seq // GRID_W)
    y_na = _na_call(h_all, bias, batch, seq)
    y_dn = _dn_call(h_all, g_nat, g_t, dn_conv_w, prow, pcol, dn_norm_w.reshape(1, DN_HEAD_DIM), batch, seq)

    wr = jnp.zeros((D_MODEL, ROUTER_LANES), F32)
    wr = wr.at[:, :N_GROUPS].set(w_rg).at[:, N_GROUPS:N_GROUPS + N_EXPERTS].set(w_re)
    br = jnp.zeros((1, ROUTER_LANES), F32)
    br = br.at[0, :N_GROUPS].set(b_rg).at[0, N_GROUPS:N_GROUPS + N_EXPERTS].set(b_re)
    x1t, route, counts = _merge_call(x2, y_na, y_dn, h_all, w_proj_na.astype(BF16), w_proj_dn.astype(BF16),
                                     w_out.astype(BF16), ln1_g.reshape(1, -1), ln1_b.reshape(1, -1), wr, br)
    pos, meta = _plan_call(route, counts)
    pos_hi = pos[TOP_K:2 * TOP_K].reshape(-1)
    pos_lo = pos[2 * TOP_K:3 * TOP_K].reshape(-1)
    ntile = meta[1, :1]
    xs = _dispatch_call(pos_hi, pos_lo, meta[2, N_GROUPS:N_GROUPS + N_EXPERTS], ntile, x1t)
    ys = _experts_call(meta[0, :_moe_tiles(x2.shape[0])], ntile, xs, w_gu, w_dn)
    return _combine_call(pos_hi, pos_lo, x1t, route, ys, ln2_g.reshape(1, -1), ln2_b.reshape(1, -1))


def kernel(x, w_in, na_rpb, dn_conv_w, dn_a_log_f, dn_a_log_b, dn_dt_bias_f, dn_dt_bias_b, dn_norm_w, w_proj_na, w_proj_dn, w_out, ln1_g, ln1_b, w_router_group, b_router_group, w_router_expert, b_router_expert, w_expert_gate_up, w_expert_down, ln2_g, ln2_b):
    batch, seq, d = x.shape
    x2 = x.reshape(batch * seq, d)
    for l in range(w_in.shape[0]):
        x2 = _layer(x2, batch, seq, w_in[l], na_rpb[l], dn_conv_w[l], dn_a_log_f[l], dn_a_log_b[l],
                    dn_dt_bias_f[l], dn_dt_bias_b[l], dn_norm_w[l], w_proj_na[l], w_proj_dn[l], w_out[l],
                    ln1_g[l], ln1_b[l], w_router_group[l], b_router_group[l], w_router_expert[l],
                    b_router_expert[l], w_expert_gate_up[l], w_expert_down[l], ln2_g[l], ln2_b[l])
    return x2.reshape(batch, seq, d)
```
